```python
import math
import jax, jax.numpy as jnp
from jax import lax
import numpy as np

D_MODEL = 1024
BATCH = 2
SEQ = 8192
DEPTH = 1

D_PLE = 256
D_ATTN = D_MODEL // 2
HEAD_DIM = 64
N_ATTN_HEADS = D_ATTN // HEAD_DIM
D_POOL = D_MODEL - D_ATTN
POOL_WINDOWS = (2, 4, 8, 16)
N_POOL_GROUPS = len(POOL_WINDOWS)
POOL_GROUP_DIM = D_POOL // N_POOL_GROUPS
D_MIX = D_ATTN + D_POOL
D_IN_PROJ = 3 * D_ATTN + D_POOL
MOBA_BLOCK = 256
MOBA_TOPK = 3
Q_CHUNK = 64
REL_BUCKETS = 32
REL_MAX_DIST = 128
N_EXPERT_GROUPS = 4
EXPERTS_PER_GROUP = 8
D_EXPERT = 512
TOP_K_INNER = 2
RMS_EPS = 1e-6

kernel_name = "hymba_moba_pool_hiermoe_block"


def rms_norm(x, g):
    xf = x.astype(jnp.float32)
    y = xf * lax.rsqrt(jnp.mean(xf * xf, axis=-1, keepdims=True) + RMS_EPS)
    return (y * g.astype(jnp.float32)).astype(x.dtype)


def rel_bucket(n):
    n = jnp.maximum(n, 0)
    max_exact = REL_BUCKETS // 2
    nf = jnp.maximum(n, 1).astype(jnp.float32)
    large = max_exact + (jnp.log(nf / max_exact) / math.log(REL_MAX_DIST / max_exact)
                         * (REL_BUCKETS - max_exact)).astype(jnp.int32)
    large = jnp.minimum(large, REL_BUCKETS - 1)
    return jnp.where(n < max_exact, n, large)


def moba_attention(q, k, v, rel_bias):
    B, S, H, hd = q.shape
    nb = -(-S // MOBA_BLOCK)
    s_pad = nb * MOBA_BLOCK
    pad = ((0, 0), (0, s_pad - S), (0, 0), (0, 0))
    q, k, v = (jnp.pad(t, pad).transpose(0, 2, 1, 3) for t in (q, k, v))
    kb = k.reshape(B, H, nb, MOBA_BLOCK, hd)
    vb = v.reshape(B, H, nb, MOBA_BLOCK, hd)
    kmean = jnp.mean(kb.astype(jnp.float32), axis=3)
    k_sel = min(MOBA_TOPK, max(nb - 1, 1))
    scale = HEAD_DIM ** -0.5
    bias_ht = rel_bias.T.astype(jnp.float32)
    n_chunks = s_pad // Q_CHUNK
    qc = jnp.moveaxis(q.reshape(B, H, n_chunks, Q_CHUNK, hd), 2, 0)
    b_ix = jnp.arange(B)[:, None, None, None]
    h_ix = jnp.arange(H)[None, :, None, None]
    h_ix5 = h_ix[..., None]
    blk_ar = jnp.arange(MOBA_BLOCK)

    def chunk_fn(args):
        ci, q_c = args
        q0 = ci * Q_CHUNK
        ob = q0 // MOBA_BLOCK
        q_pos = q0 + jnp.arange(Q_CHUNK)
        qf = q_c.astype(jnp.float32)
        gate = jnp.einsum('bhqd,bhnd->bhqn', qf, kmean)
        gate = jnp.where(jnp.arange(nb) < ob, gate, -jnp.inf)
        _, idx = lax.top_k(gate, k_sel)
        valid = idx < ob
        k_g = kb[b_ix, h_ix, idx].astype(jnp.float32)
        v_g = vb[b_ix, h_ix, idx].astype(jnp.float32)
        s_sel = jnp.einsum('bhqd,bhqknd->bhqkn', qf, k_g) * scale
        key_pos_sel = idx[..., None] * MOBA_BLOCK + blk_ar
        bucket_sel = rel_bucket(q_pos[None, None, :, None, None] - key_pos_sel)
        s_sel = s_sel + bias_ht[h_ix5, bucket_sel]
        s_sel = jnp.where(valid[..., None], s_sel, -jnp.inf)
        k_own = lax.dynamic_index_in_dim(kb, ob, axis=2, keepdims=False).astype(jnp.float32)
        v_own = lax.dynamic_index_in_dim(vb, ob, axis=2, keepdims=False).astype(jnp.float32)
        dist_own = q_pos[:, None] - (ob * MOBA_BLOCK + blk_ar)[None, :]
        s_own = (jnp.einsum('bhqd,bhnd->bhqn', qf, k_own) * scale
                 + bias_ht[:, rel_bucket(dist_own)])
        s_own = jnp.where(dist_own >= 0, s_own, -jnp.inf)
        logits = jnp.concatenate(
            [s_sel.reshape(B, H, Q_CHUNK, k_sel * MOBA_BLOCK), s_own], axis=-1)
        probs = jax.nn.softmax(logits, axis=-1)
        p_sel = probs[..., :k_sel * MOBA_BLOCK].reshape(B, H, Q_CHUNK, k_sel, MOBA_BLOCK)
        p_own = probs[..., k_sel * MOBA_BLOCK:]
        out = (jnp.einsum('bhqkn,bhqknd->bhqd', p_sel, v_g)
               + jnp.einsum('bhqn,bhnd->bhqd', p_own, v_own))
        return out.astype(q_c.dtype)

    out = lax.map(chunk_fn, (jnp.arange(n_chunks, dtype=jnp.int32), qc))
    out = out.transpose(1, 0, 3, 2, 4).reshape(B, s_pad, H * hd)
    return out[:, :S]


def causal_pool_mixer(u, w_pool, pool_scale):
    B, S, _ = u.shape
    uf = u.astype(jnp.float32)
    cs = jnp.concatenate([jnp.zeros((B, 1, D_POOL), jnp.float32), jnp.cumsum(uf, axis=1)], axis=1)
    t = jnp.arange(S)
    means = []
    for gi, w in enumerate(POOL_WINDOWS):
        c = cs[..., gi * POOL_GROUP_DIM:(gi + 1) * POOL_GROUP_DIM]
        start = jnp.maximum(t + 1 - w, 0)
        cnt = (t + 1 - start).astype(jnp.float32)[None, :, None]
        means.append((c[:, 1:] - c[:, start]) / cnt)
    pooled = (jnp.concatenate(means, axis=-1) - uf).reshape(B, S, N_POOL_GROUPS, POOL_GROUP_DIM)
    y = jnp.einsum('bsgc,gcd->bsgd', pooled, w_pool.astype(jnp.float32)).reshape(B, S, D_POOL)
    return (y * pool_scale.astype(jnp.float32)).astype(u.dtype)


def hier_moe(h, w_r1, b_r1, w_r2, b_r2, w_gate, w_up, w_down):
    B, S, D = h.shape
    hf = h.reshape(-1, D)
    logits1 = jnp.dot(hf, w_r1).astype(jnp.float32) + b_r1.astype(jnp.float32)
    p1 = jax.nn.softmax(logits1, axis=-1)
    g_sel = jnp.argmax(logits1, axis=-1)
    p_g = jnp.take_along_axis(p1, g_sel[:, None], axis=-1)[:, 0]
    logits2 = (jnp.einsum('td,gde->tge', hf, w_r2).astype(jnp.float32)
               + b_r2.astype(jnp.float32))
    l2 = jnp.take_along_axis(logits2, g_sel[:, None, None], axis=1)[:, 0]
    top_v, top_i = lax.top_k(l2, TOP_K_INNER)
    w2 = jax.nn.softmax(top_v, axis=-1)
    inner = jnp.sum(jax.nn.one_hot(top_i, EXPERTS_PER_GROUP, dtype=jnp.float32) * w2[..., None], axis=1)
    gates = (jax.nn.one_hot(g_sel, N_EXPERT_GROUPS, dtype=jnp.float32)[:, :, None]
             * (p_g[:, None] * inner)[:, None, :])
    y = jnp.zeros(hf.shape, jnp.float32)
    for g in range(N_EXPERT_GROUPS):
        a = jnp.einsum('td,edf->tef', hf, w_gate[g])
        u = jnp.einsum('td,edf->tef', hf, w_up[g])
        hid = (jax.nn.silu(a.astype(jnp.float32)) * u.astype(jnp.float32)
               * gates[:, g, :, None])
        y = y + jnp.einsum('tef,efd->td', hid.astype(h.dtype), w_down[g]).astype(jnp.float32)
    return y.astype(h.dtype).reshape(B, S, D)


def setup_inputs(seed: int = 0) -> dict:
    key = jax.random.key(seed)
    ks = jax.random.split(key, 24)
    f32 = jnp.float32
    nrm = lambda k, shape, s: jax.random.normal(k, shape, f32) * s
    L, G, E = DEPTH, N_EXPERT_GROUPS, EXPERTS_PER_GROUP
    return {
        "x": nrm(ks[0], (BATCH, SEQ, D_MODEL), 1.0),
        "p": nrm(ks[1], (DEPTH, BATCH, SEQ, D_PLE), 1.0),
        "g_mix": 1.0 + nrm(ks[2], (L, D_MODEL), 0.05),
        "w_in": nrm(ks[3], (L, D_MODEL, D_IN_PROJ), D_MODEL ** -0.5),
        "w_pool": nrm(ks[4], (L, N_POOL_GROUPS, POOL_GROUP_DIM, POOL_GROUP_DIM), POOL_GROUP_DIM ** -0.5),
        "pool_scale": 1.0 + nrm(ks[5], (L, D_POOL), 0.1),
        "w_out": nrm(ks[6], (L, D_MIX, D_MODEL), D_MIX ** -0.5),
        "rel_bias": nrm(ks[7], (REL_BUCKETS, N_ATTN_HEADS), 0.5),
        "g_ffn": 1.0 + nrm(ks[8], (L, D_MODEL), 0.05),
        "w_r1": nrm(ks[9], (L, D_MODEL, G), D_MODEL ** -0.5),
        "b_r1": nrm(ks[10], (L, G), 0.01),
        "w_r2": nrm(ks[11], (L, G, D_MODEL, E), D_MODEL ** -0.5),
        "b_r2": nrm(ks[12], (L, G, E), 0.01),
        "w_gate": nrm(ks[13], (L, G, E, D_MODEL, D_EXPERT), D_MODEL ** -0.5),
        "w_up": nrm(ks[14], (L, G, E, D_MODEL, D_EXPERT), D_MODEL ** -0.5),
        "w_down": nrm(ks[15], (L, G, E, D_EXPERT, D_MODEL), D_EXPERT ** -0.5),
        "g_ple": 1.0 + nrm(ks[16], (L, D_MODEL), 0.05),
        "w_ple_proj": nrm(ks[17], (L, D_PLE, D_MODEL), D_PLE ** -0.5),
        "w_ple_gate": nrm(ks[18], (L, D_MODEL, D_MODEL), D_MODEL ** -0.5),
        "g_final": 1.0 + nrm(ks[19], (D_MODEL,), 0.05),
    }


def reference(x, p, g_mix, w_in, w_pool, pool_scale, w_out, rel_bias, g_ffn, w_r1, b_r1,
              w_r2, b_r2, w_gate, w_up, w_down, g_ple, w_ple_proj, w_ple_gate, g_final):
    B, S, _ = x.shape
    for i in range(DEPTH):
        h = rms_norm(x, g_mix[i])
        z = jnp.dot(h, w_in[i])
        q = z[..., :D_ATTN].reshape(B, S, N_ATTN_HEADS, HEAD_DIM)
        k = z[..., D_ATTN:2 * D_ATTN].reshape(B, S, N_ATTN_HEADS, HEAD_DIM)
        v = z[..., 2 * D_ATTN:3 * D_ATTN].reshape(B, S, N_ATTN_HEADS, HEAD_DIM)
        u = z[..., 3 * D_ATTN:]
        a = moba_attention(q, k, v, rel_bias)
        b = causal_pool_mixer(u, w_pool[i], pool_scale[i])
        x = x + jnp.dot(jnp.concatenate([a, b], axis=-1), w_out[i])
        x = x + hier_moe(rms_norm(x, g_ffn[i]), w_r1[i], b_r1[i], w_r2[i], b_r2[i],
                         w_gate[i], w_up[i], w_down[i])
        gate = jax.nn.sigmoid(jnp.dot(rms_norm(x, g_ple[i]), w_ple_gate[i]).astype(jnp.float32))
        x = x + (jnp.dot(p[i], w_ple_proj[i]).astype(jnp.float32) * gate).astype(x.dtype)
    return rms_norm(x, g_final)
```

```python
import functools
import math

import numpy as np
import jax
import jax.numpy as jnp
from jax import lax
from jax.experimental import pallas as pl
from jax.experimental.pallas import tpu as pltpu

F32 = jnp.float32
BF16 = jnp.bfloat16

HEAD_DIM = 64
N_HEADS = 8
D_ATTN = N_HEADS * HEAD_DIM
POOL_WINDOWS = (2, 4, 8, 16)
POOL_GROUP_DIM = 128
D_POOL = POOL_GROUP_DIM * len(POOL_WINDOWS)
MOBA_BLOCK = 256
MOBA_TOPK = 3
REL_BUCKETS = 32
REL_MAX_DIST = 128
N_GROUPS = 4
E_PER_GROUP = 8
N_EXPERTS = N_GROUPS * E_PER_GROUP
RMS_EPS = 1e-6

LANES = 128
SUBLANES = 8
VMEM_LIMIT = 56 * 1024 * 1024

MASK_VALUE = -1e30
HALO = max(POOL_WINDOWS)
TM_PROJ = 512
TM = 256
PAIR = 2 * HEAD_DIM


def _rms(x, g):
    return x * lax.rsqrt(jnp.mean(x * x, axis=-1, keepdims=True) + RMS_EPS) * g


def _dot(a, b):
    return jnp.dot(a, b, preferred_element_type=F32)


def _dot_nt(a, b):
    return lax.dot_general(a, b, (((1,), (1,)), ((), ())), preferred_element_type=F32)


def _rel_bucket_np(n):
    n = np.maximum(n, 0)
    max_exact = REL_BUCKETS // 2
    nf = np.maximum(n, 1).astype(np.float32)
    large = max_exact + (np.log(nf / np.float32(max_exact)) / np.float32(math.log(REL_MAX_DIST / max_exact))
                         * np.float32(REL_BUCKETS - max_exact)).astype(np.int32)
    large = np.minimum(large, REL_BUCKETS - 1)
    return np.where(n < max_exact, n, large).astype(np.int32)


def _bucket_tiles(seq):
    r = np.arange(MOBA_BLOCK)
    d_own = r[:, None] - r[None, :]
    own = np.where(d_own >= 0, _rel_bucket_np(d_own), -1).astype(np.int32)
    prev = _rel_bucket_np(d_own + MOBA_BLOCK)
    far = _rel_bucket_np(np.arange(MOBA_BLOCK + 1, max(seq, MOBA_BLOCK + 2)))
    assert np.all(far == REL_BUCKETS - 1)
    return own, prev


def _bias_kernel(rb_ref, own_b_ref, prev_b_ref, own_ref, prev_ref):
    h = pl.program_id(0)
    far = rb_ref[REL_BUCKETS - 1, h]
    ob = own_b_ref[...]
    pb = prev_b_ref[...]
    own = jnp.where(ob < 0, MASK_VALUE, 0.0).astype(F32)
    prev = jnp.zeros(pb.shape, F32)
    for b in range(REL_BUCKETS):
        val = rb_ref[b, h] - far
        own = jnp.where(ob == b, val, own)
        prev = jnp.where(pb == b, val, prev)
    own_ref[...] = own
    prev_ref[...] = prev


def _bias_tables(rel_bias, seq):
    own_b, prev_b = _bucket_tiles(seq)
    blk = MOBA_BLOCK
    tile = pl.BlockSpec((blk, blk), lambda h: (0, 0))
    out = pl.BlockSpec((None, blk, blk), lambda h: (h, 0, 0))
    return pl.pallas_call(
        _bias_kernel,
        grid=(N_HEADS,),
        in_specs=[pl.BlockSpec(memory_space=pltpu.SMEM), tile, tile],
        out_specs=[out, out],
        out_shape=[jax.ShapeDtypeStruct((N_HEADS, blk, blk), F32)] * 2,
        name="bias_tables",
    )(rel_bias.astype(F32), jnp.asarray(own_b), jnp.asarray(prev_b))


def _inproj_kernel(x_ref, g_ref, w_ref, qkv_ref, u_ref):
    hb = _rms(x_ref[...], g_ref[...]).astype(BF16)
    scale = HEAD_DIM ** -0.5
    qkv_ref[:, :D_ATTN] = (_dot(hb, w_ref[:, :D_ATTN]) * scale).astype(BF16)
    qkv_ref[:, D_ATTN:2 * D_ATTN] = _dot(hb, w_ref[:, D_ATTN:2 * D_ATTN]).astype(BF16)
    qkv_ref[:, 2 * D_ATTN:] = _dot(hb, w_ref[:, 2 * D_ATTN:3 * D_ATTN]).astype(BF16)
    u_ref[...] = _dot(hb, w_ref[:, 3 * D_ATTN:])


def _in_proj(x2d, g_mix, w_in):
    t, d = x2d.shape
    n_in = w_in.shape[1]
    tm = min(TM_PROJ, t)
    return pl.pallas_call(
        _inproj_kernel,
        grid=(t // tm,),
        in_specs=[pl.BlockSpec((tm, d), lambda i: (i, 0)),
                  pl.BlockSpec((1, d), lambda i: (0, 0)),
                  pl.BlockSpec((d, n_in), lambda i: (0, 0))],
        out_specs=[pl.BlockSpec((tm, 3 * D_ATTN), lambda i: (i, 0)),
                   pl.BlockSpec((tm, D_POOL), lambda i: (i, 0))],
        out_shape=[jax.ShapeDtypeStruct((t, 3 * D_ATTN), BF16),
                   jax.ShapeDtypeStruct((t, D_POOL), F32)],
        compiler_params=pltpu.CompilerParams(dimension_semantics=("parallel",), vmem_limit_bytes=VMEM_LIMIT),
        name="in_proj",
    )(x2d, g_mix.reshape(1, d).astype(F32), w_in.astype(BF16))


def _attn_kernel(q_ref, k_ref, v_ref, ownb_ref, prevb_ref, o_ref,
                 kaug, vaug, kmean, qaug, m_s, l_s, acc_s, *, nb):
    blk = MOBA_BLOCK
    i = pl.program_id(2)

    @pl.when(i == 0)
    def _build_keys():
        kmean[...] = jnp.zeros(kmean.shape, F32)
        lane_b = lax.broadcasted_iota(jnp.int32, (blk, LANES), 1)

        def body(j, carry):
            r0 = pl.multiple_of(j * blk, blk)
            kb = k_ref[pl.ds(r0, blk), :]
            kaug[pl.ds(r0, blk), :LANES] = kb
            kaug[pl.ds(r0, blk), LANES:] = jnp.where(lane_b == j, 1.0, 0.0).astype(BF16)
            vaug[pl.ds(r0, blk), :LANES] = v_ref[pl.ds(r0, blk), :]
            vaug[pl.ds(r0, blk), LANES:] = jnp.ones((blk, LANES), BF16)
            kmean[pl.ds(j, 1), :] = jnp.mean(kb.astype(F32), axis=0, keepdims=True)
            return carry

        lax.fori_loop(0, nb, body, 0)

    lane = lax.broadcasted_iota(jnp.int32, (blk, LANES), 1)
    q = q_ref[...]
    kmean_b = kmean[...].astype(BF16)
    past = lane < i

    for hh in range(2):
        in_head = (lane < HEAD_DIM) if hh == 0 else (lane >= HEAD_DIM)
        qm = jnp.where(in_head, q, jnp.zeros_like(q))
        gate = jnp.where(past, _dot_nt(qm, kmean_b), -jnp.inf)
        chosen = jnp.zeros((blk, LANES), F32)
        for _ in range(MOBA_TOPK):
            top = jnp.max(gate, axis=-1, keepdims=True)
            idx = jnp.min(jnp.where(gate == top, lane, LANES), axis=-1, keepdims=True)
            hit = lane == idx
            chosen = jnp.where(hit, 1.0, chosen)
            gate = jnp.where(hit, -jnp.inf, gate)
        pen = jnp.where(past, jnp.where(chosen > 0.0, 0.0, MASK_VALUE), 0.0)
        qaug[hh, :, :LANES] = qm
        qaug[hh, :, LANES:] = pen.astype(BF16)

    def scores(hh, r0):
        return _dot_nt(qaug[hh], kaug[pl.ds(r0, blk), :])

    def fold(hh, s, r0, first):
        m_blk = jnp.max(s, axis=-1, keepdims=True)
        if first:
            m_new = jnp.broadcast_to(m_blk, (blk, LANES))
        else:
            m_old = m_s[hh]
            m_new = jnp.maximum(m_old, m_blk)
            alpha = jnp.exp(m_old - m_new)
        p = jnp.exp(s - jnp.concatenate([m_new, m_new], axis=1))
        pv = _dot(p.astype(BF16), vaug[pl.ds(r0, blk), :])
        if first:
            acc_s[hh] = pv[:, :LANES]
            l_s[hh] = pv[:, LANES:]
        else:
            acc_s[hh] = alpha * acc_s[hh] + pv[:, :LANES]
            l_s[hh] = alpha * l_s[hh] + pv[:, LANES:]
        m_s[hh] = m_new

    own0 = pl.multiple_of(i * blk, blk)
    for hh in range(2):
        fold(hh, scores(hh, own0) + ownb_ref[hh], own0, True)

    @pl.when(i >= 1)
    def _prev_block():
        r0 = pl.multiple_of((i - 1) * blk, blk)
        for hh in range(2):
            fold(hh, scores(hh, r0) + prevb_ref[hh], r0, False)

    def far_body(j, carry):
        r0 = pl.multiple_of(j * blk, blk)
        for hh in range(2):
            fold(hh, scores(hh, r0), r0, False)
        return carry

    lax.fori_loop(0, jnp.maximum(i - 1, 0), far_body, 0)

    out_a = acc_s[0] / l_s[0]
    out_b = acc_s[1] / l_s[1]
    o_ref[...] = jnp.where(lane < HEAD_DIM, out_a, out_b).astype(o_ref.dtype)


def _moba_attention(qkv, own_bias, prev_bias):
    b, s, _ = qkv.shape
    blk = MOBA_BLOCK
    nb = s // blk
    n_pairs = D_ATTN // PAIR
    assert s % blk == 0 and nb <= LANES
    kern = functools.partial(_attn_kernel, nb=nb)
    return pl.pallas_call(
        kern,
        grid=(b, n_pairs, nb),
        in_specs=[pl.BlockSpec((None, blk, PAIR), lambda bi, hp, i: (bi, i, hp)),
                  pl.BlockSpec((None, s, PAIR), lambda bi, hp, i: (bi, 0, n_pairs + hp)),
                  pl.BlockSpec((None, s, PAIR), lambda bi, hp, i: (bi, 0, 2 * n_pairs + hp)),
                  pl.BlockSpec((2, blk, blk), lambda bi, hp, i: (hp, 0, 0)),
                  pl.BlockSpec((2, blk, blk), lambda bi, hp, i: (hp, 0, 0))],
        out_specs=pl.BlockSpec((None, blk, PAIR), lambda bi, hp, i: (bi, i, hp)),
        out_shape=jax.ShapeDtypeStruct((b, s, D_ATTN), BF16),
        scratch_shapes=[pltpu.VMEM((s, 2 * LANES), BF16),
                        pltpu.VMEM((s, 2 * LANES), BF16),
                        pltpu.VMEM((LANES, LANES), F32),
                        pltpu.VMEM((2, blk, 2 * LANES), BF16),
                        pltpu.VMEM((2, blk, LANES), F32),
                        pltpu.VMEM((2, blk, LANES), F32),
                        pltpu.VMEM((2, blk, LANES), F32)],
        compiler_params=pltpu.CompilerParams(
            dimension_semantics=("parallel", "parallel", "arbitrary"), vmem_limit_bytes=VMEM_LIMIT),
        name="moba_attn",
    )(qkv, qkv, qkv, own_bias, prev_bias)


def _mix_kernel(x_ref, a_ref, u_ref, uh_ref, wpool_ref, pscale_ref, wout_ref, gffn_ref,
                wr_hi_ref, wr_lo_ref, br_ref,
                x2_ref, h2_ref, meta_ref, metat_ref, cnt_ref, ext_s, carry_s, *, tm, seq):
    t = pl.program_id(0)

    @pl.when(t == 0)
    def _init():
        carry_s[...] = jnp.zeros(carry_s.shape, F32)

    pos0 = (t * tm) % seq
    ext_s[0:HALO, :] = jnp.where(pos0 == 0, 0.0, uh_ref[...])
    ext_s[HALO:HALO + tm, :] = u_ref[...]
    pos = pos0 + lax.broadcasted_iota(jnp.int32, (tm, 1), 0)
    parts = []
    for gi, w in enumerate(POOL_WINDOWS):
        c0 = gi * POOL_GROUP_DIM
        cols = slice(c0, c0 + POOL_GROUP_DIM)
        tok = ext_s[HALO:HALO + tm, cols]
        win = tok
        for d in range(1, w):
            win = win + ext_s[HALO - d:HALO - d + tm, cols]
        cnt = jnp.minimum(pos + 1, w).astype(F32)
        pooled = win / cnt - tok
        parts.append(_dot(pooled.astype(BF16), wpool_ref[gi]) * pscale_ref[:, cols])
    b = jnp.concatenate(parts, axis=1).astype(BF16)

    mix = _dot(a_ref[...], wout_ref[:D_ATTN, :]) + _dot(b, wout_ref[D_ATTN:, :])
    x2 = x_ref[...] + mix
    x2_ref[...] = x2
    h2 = _rms(x2, gffn_ref[...])
    h2_ref[...] = h2

    hi = h2.astype(BF16)
    lo = (h2 - hi.astype(F32)).astype(BF16)
    logits = _dot(hi, wr_hi_ref[...]) + _dot(lo, wr_hi_ref[...]) + _dot(hi, wr_lo_ref[...]) + br_ref[...]

    lane = lax.broadcasted_iota(jnp.int32, (tm, LANES), 1)
    is_group = lane < N_GROUPS
    l1 = jnp.where(is_group, logits, -jnp.inf)
    top_g = jnp.max(l1, axis=-1, keepdims=True)
    g_idx = jnp.min(jnp.where(l1 == top_g, lane, LANES), axis=-1, keepdims=True)
    denom = jnp.sum(jnp.where(is_group, jnp.exp(logits - top_g), 0.0), axis=-1, keepdims=True)
    p_g = 1.0 / denom
    lane_group = jnp.where(lane >= N_GROUPS, (lane - N_GROUPS) // E_PER_GROUP, -1)
    l2 = jnp.where(lane_group == g_idx, logits, -jnp.inf)
    v1 = jnp.max(l2, axis=-1, keepdims=True)
    i1 = jnp.min(jnp.where(l2 == v1, lane, LANES), axis=-1, keepdims=True)
    l2 = jnp.where(lane == i1, -jnp.inf, l2)
    v2 = jnp.max(l2, axis=-1, keepdims=True)
    i2 = jnp.min(jnp.where(l2 == v2, lane, LANES), axis=-1, keepdims=True)
    e1 = i1 - N_GROUPS
    e2 = i2 - N_GROUPS
    t2 = jnp.exp(v2 - v1)
    w1 = 1.0 / (1.0 + t2)
    gate1 = p_g * w1
    gate2 = p_g * (t2 * w1)

    oh1 = jnp.where(lane == e1, 1.0, 0.0)
    oh2 = jnp.where(lane == e2, 1.0, 0.0)
    oh = oh1 + oh2
    row = lax.broadcasted_iota(jnp.int32, (tm, tm), 0)
    col = lax.broadcasted_iota(jnp.int32, (tm, tm), 1)
    before = jnp.where(col < row, 1.0, 0.0).astype(BF16)
    seen = _dot(before, oh.astype(BF16)) + carry_s[0:1, :]
    r1 = jnp.sum(oh1 * seen, axis=-1, keepdims=True)
    r2 = jnp.sum(oh2 * seen, axis=-1, keepdims=True)
    carry_s[...] = carry_s[...] + jnp.sum(oh, axis=0, keepdims=True)
    cnt_ref[...] = carry_s[...]

    meta = jnp.zeros((tm, LANES), F32)
    for k, val in enumerate((e1.astype(F32), e2.astype(F32), r1, r2, gate1, gate2)):
        meta = jnp.where(lane == k, val, meta)
    meta_ref[...] = meta
    metat_ref[...] = jnp.transpose(meta)[:SUBLANES, :].astype(jnp.int32)


def _mix_route(x2d, a2d, u2d, w_pool, pool_scale, w_out, g_ffn, wr, br, seq):
    t, d = x2d.shape
    tm = TM
    assert seq % tm == 0 and tm % HALO == 0
    hb = tm // HALO
    kern = functools.partial(_mix_kernel, tm=tm, seq=seq)
    tok = lambda w: pl.BlockSpec((tm, w), lambda i: (i, 0))
    full = lambda *shape: pl.BlockSpec(shape, lambda i: (0,) * len(shape))
    wr_hi = wr.astype(BF16)
    wr_lo = (wr - wr_hi.astype(F32)).astype(BF16)
    return pl.pallas_call(
        kern,
        grid=(t // tm,),
        in_specs=[tok(d), tok(D_ATTN), tok(D_POOL),
                  pl.BlockSpec((HALO, D_POOL), lambda i: (jnp.maximum(i * hb - 1, 0), 0)),
                  full(len(POOL_WINDOWS), POOL_GROUP_DIM, POOL_GROUP_DIM), full(1, D_POOL),
                  full(D_ATTN + D_POOL, d), full(1, d), full(d, LANES), full(d, LANES), full(1, LANES)],
        out_specs=[tok(d), tok(d), tok(LANES),
                   pl.BlockSpec((SUBLANES, tm), lambda i: (0, i)),
                   full(SUBLANES, LANES)],
        out_shape=[jax.ShapeDtypeStruct((t, d), F32), jax.ShapeDtypeStruct((t, d), F32),
                   jax.ShapeDtypeStruct((t, LANES), F32),
                   jax.ShapeDtypeStruct((SUBLANES, t), jnp.int32),
                   jax.ShapeDtypeStruct((SUBLANES, LANES), F32)],
        scratch_shapes=[pltpu.VMEM((HALO + tm, D_POOL), F32), pltpu.VMEM((SUBLANES, LANES), F32)],
        compiler_params=pltpu.CompilerParams(dimension_semantics=("arbitrary",), vmem_limit_bytes=VMEM_LIMIT),
        name="mix_route",
    )(x2d, a2d, u2d, u2d, w_pool.astype(BF16), pool_scale.reshape(1, D_POOL).astype(F32),
      w_out.astype(BF16), g_ffn.reshape(1, d).astype(F32), wr_hi, wr_lo, br)


def _row_copy(src_ref, src_row, dst_ref, dst_row, sem):
    return pltpu.make_async_copy(src_ref.at[pl.ds(src_row, 1)], dst_ref.at[pl.ds(dst_row, 1)], sem)


def _scatter_kernel(offs_ref, meta_ref, h2_ref, xs_ref, sem, *, tm):
    def issue(r, carry):
        for k in range(2):
            dst = offs_ref[meta_ref[k, r]] + meta_ref[2 + k, r]
            _row_copy(h2_ref, r, xs_ref, dst, sem).start()
        return carry

    lax.fori_loop(0, tm, issue, 0)

    def drain(r, carry):
        for k in range(2):
            _row_copy(h2_ref, 0, xs_ref, 0, sem).wait()
        return carry

    lax.fori_loop(0, tm, drain, 0)


def _moe_scatter(offs, metat, h2):
    t, d = h2.shape
    tm = TM
    kern = functools.partial(_scatter_kernel, tm=tm)
    grid_spec = pltpu.PrefetchScalarGridSpec(
        num_scalar_prefetch=1,
        grid=(t // tm,),
        in_specs=[pl.BlockSpec((SUBLANES, tm), lambda i, offs: (0, i), memory_space=pltpu.SMEM),
                  pl.BlockSpec((tm, d), lambda i, offs: (i, 0))],
        out_specs=pl.BlockSpec(memory_space=pl.ANY),
        scratch_shapes=[pltpu.SemaphoreType.DMA(())],
    )
    return pl.pallas_call(
        kern,
        grid_spec=grid_spec,
        out_shape=jax.ShapeDtypeStruct((2 * t, d), F32),
        compiler_params=pltpu.CompilerParams(dimension_semantics=("arbitrary",), vmem_limit_bytes=VMEM_LIMIT),
        name="moe_scatter",
    )(offs, metat, h2)


def _expert_kernel(tile_ref, exp_ref, lo_ref, xs_ref, wg_ref, wu_ref, wd_ref, ys_ref, *, tm):
    s = pl.program_id(0)
    lo = lo_ref[s]

    @pl.when(lo < tm)
    def _compute():
        x = xs_ref[...].astype(BF16)
        a = _dot(x, wg_ref[...].astype(BF16))
        u = _dot(x, wu_ref[...].astype(BF16))
        hid = (a * jax.nn.sigmoid(a) * u).astype(BF16)
        y = _dot(hid, wd_ref[...].astype(BF16))

        @pl.when(lo == 0)
        def _first():
            ys_ref[...] = y

        @pl.when(lo > 0)
        def _later():
            row = lax.broadcasted_iota(jnp.int32, (tm, 1), 0)
            ys_ref[...] = jnp.where(row >= lo, y, ys_ref[...])


def _moe_experts(step_tile, step_exp, step_lo, xs, w_gate, w_up, w_down):
    n_rows, d = xs.shape
    tm = TM
    n_e, _, f = w_gate.shape
    n_steps = step_tile.shape[0]
    kern = functools.partial(_expert_kernel, tm=tm)
    grid_spec = pltpu.PrefetchScalarGridSpec(
        num_scalar_prefetch=3,
        grid=(n_steps,),
        in_specs=[pl.BlockSpec((tm, d), lambda s, tl, ex, lo: (tl[s], 0)),
                  pl.BlockSpec((None, d, f), lambda s, tl, ex, lo: (ex[s], 0, 0)),
                  pl.BlockSpec((None, d, f), lambda s, tl, ex, lo: (ex[s], 0, 0)),
                  pl.BlockSpec((None, f, d), lambda s, tl, ex, lo: (ex[s], 0, 0))],
        out_specs=pl.BlockSpec((tm, d), lambda s, tl, ex, lo: (tl[s], 0)),
    )
    return pl.pallas_call(
        kern,
        grid_spec=grid_spec,
        out_shape=jax.ShapeDtypeStruct((n_rows, d), F32),
        compiler_params=pltpu.CompilerParams(dimension_semantics=("arbitrary",), vmem_limit_bytes=VMEM_LIMIT),
        name="moe_experts",
    )(step_tile, step_exp, step_lo, xs, w_gate, w_up, w_down)


def _expert_steps(counts, n_rows, tm):
    n_e = counts.shape[0]
    n_tiles = n_rows // tm
    n_steps = n_tiles + n_e
    ends = jnp.cumsum(counts)
    starts = ends - counts
    first_tile = starts // tm
    last_tile = jnp.maximum(ends - 1, 0) // tm
    tiles_e = jnp.where(counts > 0, last_tile - first_tile + 1, 0)
    step_end = jnp.cumsum(tiles_e)
    step_start = step_end - tiles_e
    total = step_end[-1]
    s = jnp.arange(n_steps, dtype=jnp.int32)
    s_eff = jnp.minimum(s, total - 1)
    exp = jnp.sum((step_end[None, :] <= s_eff[:, None]).astype(jnp.int32), axis=1)
    tile = first_tile[exp] + (s_eff - step_start[exp])
    lo = jnp.maximum(starts[exp] - tile * tm, 0)
    lo = jnp.where(s < total, lo, tm)
    return tile.astype(jnp.int32), exp, lo.astype(jnp.int32), starts.astype(jnp.int32)


def _combine_kernel(offs_ref, meta_ref, x2_ref, gates_ref, p_ref, wproj_ref, wgate_ref, gple_ref, gfin_ref,
                    ys_ref, out_ref, rows_s, sem, *, tm):
    def issue(r, carry):
        for k in range(2):
            src = offs_ref[meta_ref[k, r]] + meta_ref[2 + k, r]
            _row_copy(ys_ref, src, rows_s.at[k], r, sem).start()
        return carry

    lax.fori_loop(0, tm, issue, 0)

    def drain(r, carry):
        for k in range(2):
            _row_copy(ys_ref, 0, rows_s.at[k], 0, sem).wait()
        return carry

    lax.fori_loop(0, tm, drain, 0)

    gates = gates_ref[...]
    y = gates[:, 4:5] * rows_s[0] + gates[:, 5:6] * rows_s[1]
    x3 = x2_ref[...] + y
    h3 = _rms(x3, gple_ref[...]).astype(BF16)
    gate = jax.nn.sigmoid(_dot(h3, wgate_ref[...]))
    pe = _dot(p_ref[...].astype(BF16), wproj_ref[...])
    x4 = x3 + pe * gate
    out_ref[...] = _rms(x4, gfin_ref[...])


def _combine_ple(offs, metat, x2, meta, p2d, w_ple_proj, w_ple_gate, g_ple, g_final, ys):
    t, d = x2.shape
    d_ple = p2d.shape[1]
    tm = TM
    kern = functools.partial(_combine_kernel, tm=tm)
    tok = lambda w: pl.BlockSpec((tm, w), lambda i, offs: (i, 0))
    full = lambda *shape: pl.BlockSpec(shape, lambda i, offs: (0,) * len(shape))
    grid_spec = pltpu.PrefetchScalarGridSpec(
        num_scalar_prefetch=1,
        grid=(t // tm,),
        in_specs=[pl.BlockSpec((SUBLANES, tm), lambda i, offs: (0, i), memory_space=pltpu.SMEM),
                  tok(d), tok(LANES), tok(d_ple), full(d_ple, d), full(d, d), full(1, d), full(1, d),
                  pl.BlockSpec(memory_space=pl.ANY)],
        out_specs=tok(d),
        scratch_shapes=[pltpu.VMEM((2, tm, d), F32), pltpu.SemaphoreType.DMA(())],
    )
    return pl.pallas_call(
        kern,
        grid_spec=grid_spec,
        out_shape=jax.ShapeDtypeStruct((t, d), F32),
        compiler_params=pltpu.CompilerParams(dimension_semantics=("arbitrary",), vmem_limit_bytes=VMEM_LIMIT),
        name="combine_ple",
    )(offs, metat, x2, meta, p2d, w_ple_proj.astype(BF16), w_ple_gate.astype(BF16),
      g_ple.reshape(1, d).astype(F32), g_final.reshape(1, d).astype(F32), ys)


def _layer(x2d, p2d, seq, g_mix, w_in, w_pool, pool_scale, w_out, own_bias, prev_bias, g_ffn,
           w_r1, b_r1, w_r2, b_r2, w_gate, w_up, w_down, g_ple, w_ple_proj, w_ple_gate, g_final):
    t, d = x2d.shape
    batch = t // seq
    qkv, u = _in_proj(x2d, g_mix, w_in)
    a = _moba_attention(qkv.reshape(batch, seq, 3 * D_ATTN), own_bias, prev_bias).reshape(t, D_ATTN)

    wr2 = jnp.transpose(w_r2, (1, 0, 2)).reshape(d, N_EXPERTS)
    pad = LANES - N_GROUPS - N_EXPERTS
    wr = jnp.concatenate([w_r1, wr2, jnp.zeros((d, pad), F32)], axis=1).astype(F32)
    br = jnp.concatenate([b_r1, b_r2.reshape(N_EXPERTS), jnp.zeros((pad,), F32)]).reshape(1, LANES).astype(F32)
    x2, h2, meta, metat, cnt = _mix_route(x2d, a, u, w_pool, pool_scale, w_out, g_ffn, wr, br, seq)

    counts = cnt[0, :N_EXPERTS].astype(jnp.int32)
    step_tile, step_exp, step_lo, offs = _expert_steps(counts, 2 * t, TM)
    xs = _moe_scatter(offs, metat, h2)
    f = w_gate.shape[-1]
    ys = _moe_experts(step_tile, step_exp, step_lo, xs,
                      w_gate.reshape(N_EXPERTS, d, f), w_up.reshape(N_EXPERTS, d, f),
                      w_down.reshape(N_EXPERTS, f, d))
    return _combine_ple(offs, metat, x2, meta, p2d, w_ple_proj, w_ple_gate, g_ple, g_final, ys)


def kernel(x, p, g_mix, w_in, w_pool, pool_scale, w_out, rel_bias, g_ffn, w_r1, b_r1, w_r2, b_r2,
           w_gate, w_up, w_down, g_ple, w_ple_proj, w_ple_gate, g_final):
    batch, seq, d = x.shape
    depth = p.shape[0]
    assert depth == 1, "the final norm is fused into the last stage of a single layer"
    own_bias, prev_bias = _bias_tables(rel_bias, seq)
    out = _layer(x.reshape(batch * seq, d), p[0].reshape(batch * seq, -1), seq,
                 g_mix[0], w_in[0], w_pool[0], pool_scale[0], w_out[0], own_bias, prev_bias, g_ffn[0],
                 w_r1[0], b_r1[0], w_r2[0], b_r2[0], w_gate[0], w_up[0], w_down[0],
                 g_ple[0], w_ple_proj[0], w_ple_gate[0], g_final)
    return out.reshape(batch, seq, d)
```

```python
import functools
import math

import numpy as np
import jax
import jax.numpy as jnp
from jax import lax
from jax.experimental import pallas as pl
from jax.experimental.pallas import tpu as pltpu

F32 = jnp.float32
BF16 = jnp.bfloat16

HEAD_DIM = 64
N_HEADS = 8
D_ATTN = N_HEADS * HEAD_DIM
POOL_WINDOWS = (2, 4, 8, 16)
POOL_GROUP_DIM = 128
D_POOL = POOL_GROUP_DIM * len(POOL_WINDOWS)
MOBA_BLOCK = 256
MOBA_TOPK = 3
REL_BUCKETS = 32
REL_MAX_DIST = 128
N_GROUPS = 4
E_PER_GROUP = 8
N_EXPERTS = N_GROUPS * E_PER_GROUP
RMS_EPS = 1e-6

LANES = 128
SUBLANES = 8
VMEM_LIMIT = 56 * 1024 * 1024

MASK_VALUE = -1e30
HALO = max(POOL_WINDOWS)
TM_PROJ = 512
TM = 256
PAIR = 2 * HEAD_DIM


def _rms(x, g):
    return x * lax.rsqrt(jnp.mean(x * x, axis=-1, keepdims=True) + RMS_EPS) * g


def _dot(a, b):
    return jnp.dot(a, b, preferred_element_type=F32)


def _dot_nt(a, b):
    return lax.dot_general(a, b, (((1,), (1,)), ((), ())), preferred_element_type=F32)


def _rel_bucket_np(n):
    n = np.maximum(n, 0)
    max_exact = REL_BUCKETS // 2
    nf = np.maximum(n, 1).astype(np.float32)
    large = max_exact + (np.log(nf / np.float32(max_exact)) / np.float32(math.log(REL_MAX_DIST / max_exact))
                         * np.float32(REL_BUCKETS - max_exact)).astype(np.int32)
    large = np.minimum(large, REL_BUCKETS - 1)
    return np.where(n < max_exact, n, large).astype(np.int32)


def _bucket_tiles(seq):
    r = np.arange(MOBA_BLOCK)
    d_own = r[:, None] - r[None, :]
    own = np.where(d_own >= 0, _rel_bucket_np(d_own), -1).astype(np.int32)
    prev = _rel_bucket_np(d_own + MOBA_BLOCK)
    far = _rel_bucket_np(np.arange(MOBA_BLOCK + 1, max(seq, MOBA_BLOCK + 2)))
    assert np.all(far == REL_BUCKETS - 1)
    first = np.concatenate([own, np.full_like(own, -1)], axis=1)
    later = np.concatenate([prev, own], axis=1)
    return np.stack([first, later])


def _bias_kernel(rb_ref, bucket_ref, near_ref):
    h = pl.program_id(0)
    far = rb_ref[REL_BUCKETS - 1, h]
    bucket = bucket_ref[...]
    tile = jnp.where(bucket < 0, MASK_VALUE, 0.0).astype(F32)
    for b in range(REL_BUCKETS):
        tile = jnp.where(bucket == b, rb_ref[b, h] - far, tile)
    near_ref[...] = tile


def _bias_tables(rel_bias, seq):
    buckets = _bucket_tiles(seq)
    blk = MOBA_BLOCK
    return pl.pallas_call(
        _bias_kernel,
        grid=(N_HEADS, 2),
        in_specs=[pl.BlockSpec(memory_space=pltpu.SMEM),
                  pl.BlockSpec((None, blk, 2 * blk), lambda h, v: (v, 0, 0))],
        out_specs=pl.BlockSpec((None, None, blk, 2 * blk), lambda h, v: (h, v, 0, 0)),
        out_shape=jax.ShapeDtypeStruct((N_HEADS, 2, blk, 2 * blk), F32),
        name="bias_tables",
    )(rel_bias.astype(F32), jnp.asarray(buckets))


def _inproj_kernel(x_ref, g_ref, w_ref, qkv_ref, u_ref):
    hb = _rms(x_ref[...], g_ref[...]).astype(BF16)
    scale = HEAD_DIM ** -0.5
    qkv_ref[:, :D_ATTN] = (_dot(hb, w_ref[:, :D_ATTN]) * scale).astype(BF16)
    qkv_ref[:, D_ATTN:2 * D_ATTN] = _dot(hb, w_ref[:, D_ATTN:2 * D_ATTN]).astype(BF16)
    qkv_ref[:, 2 * D_ATTN:] = _dot(hb, w_ref[:, 2 * D_ATTN:3 * D_ATTN]).astype(BF16)
    u_ref[...] = _dot(hb, w_ref[:, 3 * D_ATTN:])


def _in_proj(x2d, g_mix, w_in):
    t, d = x2d.shape
    n_in = w_in.shape[1]
    tm = min(TM_PROJ, t)
    return pl.pallas_call(
        _inproj_kernel,
        grid=(t // tm,),
        in_specs=[pl.BlockSpec((tm, d), lambda i: (i, 0)),
                  pl.BlockSpec((1, d), lambda i: (0, 0)),
                  pl.BlockSpec((d, n_in), lambda i: (0, 0))],
        out_specs=[pl.BlockSpec((tm, 3 * D_ATTN), lambda i: (i, 0)),
                   pl.BlockSpec((tm, D_POOL), lambda i: (i, 0))],
        out_shape=[jax.ShapeDtypeStruct((t, 3 * D_ATTN), BF16),
                   jax.ShapeDtypeStruct((t, D_POOL), F32)],
        compiler_params=pltpu.CompilerParams(dimension_semantics=("parallel",), vmem_limit_bytes=VMEM_LIMIT),
        name="in_proj",
    )(x2d, g_mix.reshape(1, d).astype(F32), w_in.astype(BF16))


FAR_BLOCKS = 2


def _attn_kernel(q_ref, k_ref, v_ref, nearb_ref, o_ref,
                 kaug, vaug, kmean, qnear, qfar, sbuf, m_s, l_s, acc_s, *, nb):
    blk = MOBA_BLOCK
    i = pl.program_id(2)

    @pl.when(i == 0)
    def _build_keys():
        kmean[...] = jnp.zeros(kmean.shape, F32)
        lane_b = lax.broadcasted_iota(jnp.int32, (blk, LANES), 1)

        def body(j, carry):
            r0 = pl.multiple_of(j * blk, blk)
            kb = k_ref[pl.ds(r0, blk), :]
            kaug[pl.ds(r0, blk), :LANES] = kb
            kaug[pl.ds(r0, blk), LANES:] = jnp.where(lane_b == j, 1.0, 0.0).astype(BF16)
            vaug[pl.ds(r0, blk), :LANES] = v_ref[pl.ds(r0, blk), :]
            vaug[pl.ds(r0, blk), LANES:] = jnp.ones((blk, LANES), BF16)
            kmean[pl.ds(j, 1), :] = jnp.mean(kb.astype(F32), axis=0, keepdims=True)
            return carry

        lax.fori_loop(0, nb, body, 0)

    lane = lax.broadcasted_iota(jnp.int32, (blk, LANES), 1)
    q = q_ref[...]
    kmean_b = kmean[...].astype(BF16)
    nbp = kmean.shape[0]
    blk_id = lax.broadcasted_iota(jnp.int32, (nbp, blk), 0)

    for hh in range(2):
        in_head = (lane < HEAD_DIM) if hh == 0 else (lane >= HEAD_DIM)
        qm = jnp.where(in_head, q, jnp.zeros_like(q))
        gate = jnp.where(blk_id < i, _dot_nt(kmean_b, qm), -jnp.inf)
        chosen_t = jnp.zeros((nbp, blk), F32)
        for _ in range(MOBA_TOPK):
            top = jnp.max(gate, axis=0, keepdims=True)
            idx = jnp.min(jnp.where(gate == top, blk_id, nbp), axis=0, keepdims=True)
            hit = blk_id == idx
            chosen_t = jnp.where(hit, 1.0, chosen_t)
            gate = jnp.where(hit, -jnp.inf, gate)
        if nbp < LANES:
            chosen_t = jnp.concatenate([chosen_t, jnp.zeros((LANES - nbp, blk), F32)], axis=0)
        penalty = jnp.where(jnp.transpose(chosen_t) > 0.0, 0.0, MASK_VALUE)
        qnear[hh, :, :LANES] = qm
        qnear[hh, :, LANES:] = jnp.where(lane < i, penalty, 0.0).astype(BF16)
        qfar[hh, :, :LANES] = qm
        qfar[hh, :, LANES:] = jnp.where(lane < i - 1, penalty, MASK_VALUE).astype(BF16)

    def fold(hh, s, vals, first):
        width = s.shape[1]
        m_blk = jnp.max(s, axis=-1, keepdims=True)
        if first:
            m_new = jnp.broadcast_to(m_blk, (blk, LANES))
        else:
            m_old = m_s[hh]
            m_new = jnp.maximum(m_old, m_blk)
            alpha = jnp.exp(m_old - m_new)
        p = jnp.exp(s - jnp.concatenate([m_new] * (width // LANES), axis=1))
        pv = _dot(p.astype(BF16), vals)
        if first:
            acc_s[hh] = pv[:, :LANES]
            l_s[hh] = pv[:, LANES:]
        else:
            acc_s[hh] = alpha * acc_s[hh] + pv[:, :LANES]
            l_s[hh] = alpha * l_s[hh] + pv[:, LANES:]
        m_s[hh] = m_new

    near0 = pl.multiple_of(jnp.maximum(i - 1, 0) * blk, blk)
    keys = kaug[pl.ds(near0, 2 * blk), :]
    near_scores = [_dot_nt(qnear[hh], keys) + nearb_ref[hh] for hh in range(2)]

    chunk = FAR_BLOCKS * blk
    last_chunk = kaug.shape[0] // chunk - 1
    n_far = (i + FAR_BLOCKS - 2) // FAR_BLOCKS

    def chunk_start(c):
        return pl.multiple_of(jnp.minimum(c, last_chunk) * chunk, chunk)

    def far_scores(c, slot):
        keys = kaug[pl.ds(chunk_start(c), chunk), :]
        for hh in range(2):
            sbuf[slot, hh] = _dot_nt(qfar[hh], keys)

    def far_fold(c, slot):
        vals = vaug[pl.ds(chunk_start(c), chunk), :]
        for hh in range(2):
            fold(hh, sbuf[slot, hh], vals, False)

    far_scores(0, 0)
    vals = vaug[pl.ds(near0, 2 * blk), :]
    for hh in range(2):
        fold(hh, near_scores[hh], vals, True)

    def far_body(cc, carry):
        c = 2 * cc
        far_scores(c + 1, 1)
        far_fold(c, 0)
        far_scores(c + 2, 0)
        far_fold(c + 1, 1)
        return carry

    lax.fori_loop(0, (n_far + 1) // 2, far_body, 0)

    out_a = acc_s[0] / l_s[0]
    out_b = acc_s[1] / l_s[1]
    o_ref[...] = jnp.where(lane < HEAD_DIM, out_a, out_b).astype(o_ref.dtype)


def _moba_attention(qkv, near_bias):
    b, s, _ = qkv.shape
    blk = MOBA_BLOCK
    nb = s // blk
    nbp = -(-nb // SUBLANES) * SUBLANES
    n_pairs = D_ATTN // PAIR
    assert s % (FAR_BLOCKS * blk) == 0 and nbp <= LANES
    max_far = (nb - 1 + FAR_BLOCKS - 2) // FAR_BLOCKS
    assert max_far % 2 == 0 or max_far <= nb // FAR_BLOCKS - 1
    kern = functools.partial(_attn_kernel, nb=nb)
    return pl.pallas_call(
        kern,
        grid=(b, n_pairs, nb),
        in_specs=[pl.BlockSpec((None, blk, PAIR), lambda bi, hp, i: (bi, i, hp)),
                  pl.BlockSpec((None, s, PAIR), lambda bi, hp, i: (bi, 0, n_pairs + hp)),
                  pl.BlockSpec((None, s, PAIR), lambda bi, hp, i: (bi, 0, 2 * n_pairs + hp)),
                  pl.BlockSpec((2, None, blk, 2 * blk), lambda bi, hp, i: (hp, jnp.minimum(i, 1), 0, 0))],
        out_specs=pl.BlockSpec((None, blk, PAIR), lambda bi, hp, i: (bi, i, hp)),
        out_shape=jax.ShapeDtypeStruct((b, s, D_ATTN), BF16),
        scratch_shapes=[pltpu.VMEM((s, 2 * LANES), BF16),
                        pltpu.VMEM((s, 2 * LANES), BF16),
                        pltpu.VMEM((nbp, LANES), F32),
                        pltpu.VMEM((2, blk, 2 * LANES), BF16),
                        pltpu.VMEM((2, blk, 2 * LANES), BF16),
                        pltpu.VMEM((2, 2, blk, FAR_BLOCKS * blk), F32),
                        pltpu.VMEM((2, blk, LANES), F32),
                        pltpu.VMEM((2, blk, LANES), F32),
                        pltpu.VMEM((2, blk, LANES), F32)],
        compiler_params=pltpu.CompilerParams(
            dimension_semantics=("parallel", "parallel", "arbitrary"), vmem_limit_bytes=VMEM_LIMIT),
        name="moba_attn",
    )(qkv, qkv, qkv, near_bias)


def _mix_kernel(x_ref, a_ref, u_ref, uh_ref, wpool_ref, pscale_ref, wout_ref, gffn_ref,
                wr_hi_ref, wr_lo_ref, br_ref,
                x2_ref, h2_ref, meta_ref, metat_ref, cnt_ref, ext_s, carry_s, *, tm, seq):
    t = pl.program_id(0)

    @pl.when(t == 0)
    def _init():
        carry_s[...] = jnp.zeros(carry_s.shape, F32)

    pos0 = (t * tm) % seq
    ext_s[0:HALO, :] = jnp.where(pos0 == 0, 0.0, uh_ref[...])
    ext_s[HALO:HALO + tm, :] = u_ref[...]
    pos = pos0 + lax.broadcasted_iota(jnp.int32, (tm, 1), 0)
    parts = []
    for gi, w in enumerate(POOL_WINDOWS):
        c0 = gi * POOL_GROUP_DIM
        cols = slice(c0, c0 + POOL_GROUP_DIM)
        tok = ext_s[HALO:HALO + tm, cols]
        win = tok
        for d in range(1, w):
            win = win + ext_s[HALO - d:HALO - d + tm, cols]
        cnt = jnp.minimum(pos + 1, w).astype(F32)
        pooled = win / cnt - tok
        parts.append(_dot(pooled.astype(BF16), wpool_ref[gi]) * pscale_ref[:, cols])
    b = jnp.concatenate(parts, axis=1).astype(BF16)

    mix = _dot(a_ref[...], wout_ref[:D_ATTN, :]) + _dot(b, wout_ref[D_ATTN:, :])
    x2 = x_ref[...] + mix
    x2_ref[...] = x2
    h2 = _rms(x2, gffn_ref[...])
    h2_ref[...] = h2

    hi = h2.astype(BF16)
    lo = (h2 - hi.astype(F32)).astype(BF16)
    logits = _dot(hi, wr_hi_ref[...]) + _dot(lo, wr_hi_ref[...]) + _dot(hi, wr_lo_ref[...]) + br_ref[...]

    lane = lax.broadcasted_iota(jnp.int32, (tm, LANES), 1)
    is_group = lane < N_GROUPS
    l1 = jnp.where(is_group, logits, -jnp.inf)
    top_g = jnp.max(l1, axis=-1, keepdims=True)
    g_idx = jnp.min(jnp.where(l1 == top_g, lane, LANES), axis=-1, keepdims=True)
    denom = jnp.sum(jnp.where(is_group, jnp.exp(logits - top_g), 0.0), axis=-1, keepdims=True)
    p_g = 1.0 / denom
    lane_group = jnp.where(lane >= N_GROUPS, (lane - N_GROUPS) // E_PER_GROUP, -1)
    l2 = jnp.where(lane_group == g_idx, logits, -jnp.inf)
    v1 = jnp.max(l2, axis=-1, keepdims=True)
    i1 = jnp.min(jnp.where(l2 == v1, lane, LANES), axis=-1, keepdims=True)
    l2 = jnp.where(lane == i1, -jnp.inf, l2)
    v2 = jnp.max(l2, axis=-1, keepdims=True)
    i2 = jnp.min(jnp.where(l2 == v2, lane, LANES), axis=-1, keepdims=True)
    e1 = i1 - N_GROUPS
    e2 = i2 - N_GROUPS
    t2 = jnp.exp(v2 - v1)
    w1 = 1.0 / (1.0 + t2)
    gate1 = p_g * w1
    gate2 = p_g * (t2 * w1)

    oh1 = jnp.where(lane == e1, 1.0, 0.0)
    oh2 = jnp.where(lane == e2, 1.0, 0.0)
    oh = oh1 + oh2
    row = lax.broadcasted_iota(jnp.int32, (tm, tm), 0)
    col = lax.broadcasted_iota(jnp.int32, (tm, tm), 1)
    before = jnp.where(col < row, 1.0, 0.0).astype(BF16)
    seen = _dot(before, oh.astype(BF16)) + carry_s[0:1, :]
    r1 = jnp.sum(oh1 * seen, axis=-1, keepdims=True)
    r2 = jnp.sum(oh2 * seen, axis=-1, keepdims=True)
    carry_s[...] = carry_s[...] + jnp.sum(oh, axis=0, keepdims=True)
    cnt_ref[...] = carry_s[...]

    meta = jnp.zeros((tm, LANES), F32)
    for k, val in enumerate((e1.astype(F32), e2.astype(F32), r1, r2, gate1, gate2)):
        meta = jnp.where(lane == k, val, meta)
    meta_ref[...] = meta
    metat_ref[...] = jnp.transpose(meta)[:SUBLANES, :].astype(jnp.int32)


def _mix_route(x2d, a2d, u2d, w_pool, pool_scale, w_out, g_ffn, wr, br, seq):
    t, d = x2d.shape
    tm = TM
    assert seq % tm == 0 and tm % HALO == 0
    hb = tm // HALO
    kern = functools.partial(_mix_kernel, tm=tm, seq=seq)
    tok = lambda w: pl.BlockSpec((tm, w), lambda i: (i, 0))
    full = lambda *shape: pl.BlockSpec(shape, lambda i: (0,) * len(shape))
    wr_hi = wr.astype(BF16)
    wr_lo = (wr - wr_hi.astype(F32)).astype(BF16)
    return pl.pallas_call(
        kern,
        grid=(t // tm,),
        in_specs=[tok(d), tok(D_ATTN), tok(D_POOL),
                  pl.BlockSpec((HALO, D_POOL), lambda i: (jnp.maximum(i * hb - 1, 0), 0)),
                  full(len(POOL_WINDOWS), POOL_GROUP_DIM, POOL_GROUP_DIM), full(1, D_POOL),
                  full(D_ATTN + D_POOL, d), full(1, d), full(d, LANES), full(d, LANES), full(1, LANES)],
        out_specs=[tok(d), tok(d), tok(LANES),
                   pl.BlockSpec((SUBLANES, tm), lambda i: (0, i)),
                   full(SUBLANES, LANES)],
        out_shape=[jax.ShapeDtypeStruct((t, d), F32), jax.ShapeDtypeStruct((t, d), F32),
                   jax.ShapeDtypeStruct((t, LANES), F32),
                   jax.ShapeDtypeStruct((SUBLANES, t), jnp.int32),
                   jax.ShapeDtypeStruct((SUBLANES, LANES), F32)],
        scratch_shapes=[pltpu.VMEM((HALO + tm, D_POOL), F32), pltpu.VMEM((SUBLANES, LANES), F32)],
        compiler_params=pltpu.CompilerParams(dimension_semantics=("arbitrary",), vmem_limit_bytes=VMEM_LIMIT),
        name="mix_route",
    )(x2d, a2d, u2d, u2d, w_pool.astype(BF16), pool_scale.reshape(1, D_POOL).astype(F32),
      w_out.astype(BF16), g_ffn.reshape(1, d).astype(F32), wr_hi, wr_lo, br)


def _row_copy(src_ref, src_row, dst_ref, dst_row, sem):
    return pltpu.make_async_copy(src_ref.at[pl.ds(src_row, 1)], dst_ref.at[pl.ds(dst_row, 1)], sem)


def _scatter_kernel(offs_ref, meta_ref, h2_ref, xs_ref, sem, *, tm):
    def issue(r, carry):
        for k in range(2):
            dst = offs_ref[meta_ref[k, r]] + meta_ref[2 + k, r]
            _row_copy(h2_ref, r, xs_ref, dst, sem).start()
        return carry

    lax.fori_loop(0, tm, issue, 0)

    def drain(r, carry):
        for k in range(2):
            _row_copy(h2_ref, 0, xs_ref, 0, sem).wait()
        return carry

    lax.fori_loop(0, tm, drain, 0)


def _moe_scatter(offs, metat, h2):
    t, d = h2.shape
    tm = TM
    kern = functools.partial(_scatter_kernel, tm=tm)
    grid_spec = pltpu.PrefetchScalarGridSpec(
        num_scalar_prefetch=1,
        grid=(t // tm,),
        in_specs=[pl.BlockSpec((SUBLANES, tm), lambda i, offs: (0, i), memory_space=pltpu.SMEM),
                  pl.BlockSpec((tm, d), lambda i, offs: (i, 0))],
        out_specs=pl.BlockSpec(memory_space=pl.ANY),
        scratch_shapes=[pltpu.SemaphoreType.DMA(())],
    )
    return pl.pallas_call(
        kern,
        grid_spec=grid_spec,
        out_shape=jax.ShapeDtypeStruct((2 * t, d), F32),
        compiler_params=pltpu.CompilerParams(dimension_semantics=("arbitrary",), vmem_limit_bytes=VMEM_LIMIT),
        name="moe_scatter",
    )(offs, metat, h2)


def _expert_kernel(tile_ref, exp_ref, lo_ref, xs_ref, wg_ref, wu_ref, wd_ref, ys_ref, *, tm):
    s = pl.program_id(0)
    lo = lo_ref[s]

    @pl.when(lo < tm)
    def _compute():
        x = xs_ref[...].astype(BF16)
        a = _dot(x, wg_ref[...].astype(BF16))
        u = _dot(x, wu_ref[...].astype(BF16))
        hid = (a * jax.nn.sigmoid(a) * u).astype(BF16)
        y = _dot(hid, wd_ref[...].astype(BF16))

        @pl.when(lo == 0)
        def _first():
            ys_ref[...] = y

        @pl.when(lo > 0)
        def _later():
            row = lax.broadcasted_iota(jnp.int32, (tm, 1), 0)
            ys_ref[...] = jnp.where(row >= lo, y, ys_ref[...])


def _moe_experts(step_tile, step_exp, step_lo, xs, w_gate, w_up, w_down):
    n_rows, d = xs.shape
    tm = TM
    n_e, _, f = w_gate.shape
    n_steps = step_tile.shape[0]
    kern = functools.partial(_expert_kernel, tm=tm)
    grid_spec = pltpu.PrefetchScalarGridSpec(
        num_scalar_prefetch=3,
        grid=(n_steps,),
        in_specs=[pl.BlockSpec((tm, d), lambda s, tl, ex, lo: (tl[s], 0)),
                  pl.BlockSpec((None, d, f), lambda s, tl, ex, lo: (ex[s], 0, 0)),
                  pl.BlockSpec((None, d, f), lambda s, tl, ex, lo: (ex[s], 0, 0)),
                  pl.BlockSpec((None, f, d), lambda s, tl, ex, lo: (ex[s], 0, 0))],
        out_specs=pl.BlockSpec((tm, d), lambda s, tl, ex, lo: (tl[s], 0)),
    )
    return pl.pallas_call(
        kern,
        grid_spec=grid_spec,
        out_shape=jax.ShapeDtypeStruct((n_rows, d), F32),
        compiler_params=pltpu.CompilerParams(dimension_semantics=("arbitrary",), vmem_limit_bytes=VMEM_LIMIT),
        name="moe_experts",
    )(step_tile, step_exp, step_lo, xs, w_gate, w_up, w_down)


def _expert_steps(counts, n_rows, tm):
    n_e = counts.shape[0]
    n_tiles = n_rows // tm
    n_steps = n_tiles + n_e
    ends = jnp.cumsum(counts)
    starts = ends - counts
    first_tile = starts // tm
    last_tile = jnp.maximum(ends - 1, 0) // tm
    tiles_e = jnp.where(counts > 0, last_tile - first_tile + 1, 0)
    step_end = jnp.cumsum(tiles_e)
    step_start = step_end - tiles_e
    total = step_end[-1]
    s = jnp.arange(n_steps, dtype=jnp.int32)
    s_eff = jnp.minimum(s, total - 1)
    exp = jnp.sum((step_end[None, :] <= s_eff[:, None]).astype(jnp.int32), axis=1)
    tile = first_tile[exp] + (s_eff - step_start[exp])
    lo = jnp.maximum(starts[exp] - tile * tm, 0)
    lo = jnp.where(s < total, lo, tm)
    return tile.astype(jnp.int32), exp, lo.astype(jnp.int32), starts.astype(jnp.int32)


def _combine_kernel(offs_ref, meta_ref, x2_ref, gates_ref, p_ref, wproj_ref, wgate_ref, gple_ref, gfin_ref,
                    ys_ref, out_ref, rows_s, sem, *, tm):
    def issue(r, carry):
        for k in range(2):
            src = offs_ref[meta_ref[k, r]] + meta_ref[2 + k, r]
            _row_copy(ys_ref, src, rows_s.at[k], r, sem).start()
        return carry

    lax.fori_loop(0, tm, issue, 0)

    def drain(r, carry):
        for k in range(2):
            _row_copy(ys_ref, 0, rows_s.at[k], 0, sem).wait()
        return carry

    lax.fori_loop(0, tm, drain, 0)

    gates = gates_ref[...]
    y = gates[:, 4:5] * rows_s[0] + gates[:, 5:6] * rows_s[1]
    x3 = x2_ref[...] + y
    h3 = _rms(x3, gple_ref[...]).astype(BF16)
    gate = jax.nn.sigmoid(_dot(h3, wgate_ref[...]))
    pe = _dot(p_ref[...].astype(BF16), wproj_ref[...])
    x4 = x3 + pe * gate
    out_ref[...] = _rms(x4, gfin_ref[...])


def _combine_ple(offs, metat, x2, meta, p2d, w_ple_proj, w_ple_gate, g_ple, g_final, ys):
    t, d = x2.shape
    d_ple = p2d.shape[1]
    tm = TM
    kern = functools.partial(_combine_kernel, tm=tm)
    tok = lambda w: pl.BlockSpec((tm, w), lambda i, offs: (i, 0))
    full = lambda *shape: pl.BlockSpec(shape, lambda i, offs: (0,) * len(shape))
    grid_spec = pltpu.PrefetchScalarGridSpec(
        num_scalar_prefetch=1,
        grid=(t // tm,),
        in_specs=[pl.BlockSpec((SUBLANES, tm), lambda i, offs: (0, i), memory_space=pltpu.SMEM),
                  tok(d), tok(LANES), tok(d_ple), full(d_ple, d), full(d, d), full(1, d), full(1, d),
                  pl.BlockSpec(memory_space=pl.ANY)],
        out_specs=tok(d),
        scratch_shapes=[pltpu.VMEM((2, tm, d), F32), pltpu.SemaphoreType.DMA(())],
    )
    return pl.pallas_call(
        kern,
        grid_spec=grid_spec,
        out_shape=jax.ShapeDtypeStruct((t, d), F32),
        compiler_params=pltpu.CompilerParams(dimension_semantics=("arbitrary",), vmem_limit_bytes=VMEM_LIMIT),
        name="combine_ple",
    )(offs, metat, x2, meta, p2d, w_ple_proj.astype(BF16), w_ple_gate.astype(BF16),
      g_ple.reshape(1, d).astype(F32), g_final.reshape(1, d).astype(F32), ys)


def _layer(x2d, p2d, seq, g_mix, w_in, w_pool, pool_scale, w_out, near_bias, g_ffn,
           w_r1, b_r1, w_r2, b_r2, w_gate, w_up, w_down, g_ple, w_ple_proj, w_ple_gate, g_final):
    t, d = x2d.shape
    batch = t // seq
    qkv, u = _in_proj(x2d, g_mix, w_in)
    a = _moba_attention(qkv.reshape(batch, seq, 3 * D_ATTN), near_bias).reshape(t, D_ATTN)

    wr2 = jnp.transpose(w_r2, (1, 0, 2)).reshape(d, N_EXPERTS)
    pad = LANES - N_GROUPS - N_EXPERTS
    wr = jnp.concatenate([w_r1, wr2, jnp.zeros((d, pad), F32)], axis=1).astype(F32)
    br = jnp.concatenate([b_r1, b_r2.reshape(N_EXPERTS), jnp.zeros((pad,), F32)]).reshape(1, LANES).astype(F32)
    x2, h2, meta, metat, cnt = _mix_route(x2d, a, u, w_pool, pool_scale, w_out, g_ffn, wr, br, seq)

    counts = cnt[0, :N_EXPERTS].astype(jnp.int32)
    step_tile, step_exp, step_lo, offs = _expert_steps(counts, 2 * t, TM)
    xs = _moe_scatter(offs, metat, h2)
    f = w_gate.shape[-1]
    ys = _moe_experts(step_tile, step_exp, step_lo, xs,
                      w_gate.reshape(N_EXPERTS, d, f), w_up.reshape(N_EXPERTS, d, f),
                      w_down.reshape(N_EXPERTS, f, d))
    return _combine_ple(offs, metat, x2, meta, p2d, w_ple_proj, w_ple_gate, g_ple, g_final, ys)


def kernel(x, p, g_mix, w_in, w_pool, pool_scale, w_out, rel_bias, g_ffn, w_r1, b_r1, w_r2, b_r2,
           w_gate, w_up, w_down, g_ple, w_ple_proj, w_ple_gate, g_final):
    batch, seq, d = x.shape
    depth = p.shape[0]
    assert depth == 1, "the final norm is fused into the last stage of a single layer"
    near_bias = _bias_tables(rel_bias, seq)
    out = _layer(x.reshape(batch * seq, d), p[0].reshape(batch * seq, -1), seq,
                 g_mix[0], w_in[0], w_pool[0], pool_scale[0], w_out[0], near_bias, g_ffn[0],
                 w_r1[0], b_r1[0], w_r2[0], b_r2[0], w_gate[0], w_up[0], w_down[0],
                 g_ple[0], w_ple_proj[0], w_ple_gate[0], g_final)
    return out.reshape(batch, seq, d)
```

```python
import functools
import math

import numpy as np
import jax
import jax.numpy as jnp
from jax import lax
from jax.experimental import pallas as pl
from jax.experimental.pallas import tpu as pltpu

F32 = jnp.float32
BF16 = jnp.bfloat16

HEAD_DIM = 64
N_HEADS = 8
D_ATTN = N_HEADS * HEAD_DIM
POOL_WINDOWS = (2, 4, 8, 16)
POOL_GROUP_DIM = 128
D_POOL = POOL_GROUP_DIM * len(POOL_WINDOWS)
MOBA_BLOCK = 256
MOBA_TOPK = 3
REL_BUCKETS = 32
REL_MAX_DIST = 128
N_GROUPS = 4
E_PER_GROUP = 8
N_EXPERTS = N_GROUPS * E_PER_GROUP
RMS_EPS = 1e-6

LANES = 128
SUBLANES = 8
VMEM_LIMIT = 56 * 1024 * 1024

MASK_VALUE = -1e30
HALO = max(POOL_WINDOWS)
TM_PROJ = 512
TM = 256
PAIR = 2 * HEAD_DIM


def _rms(x, g):
    return x * lax.rsqrt(jnp.mean(x * x, axis=-1, keepdims=True) + RMS_EPS) * g


def _dot(a, b):
    return jnp.dot(a, b, preferred_element_type=F32)


def _dot_nt(a, b):
    return lax.dot_general(a, b, (((1,), (1,)), ((), ())), preferred_element_type=F32)


def _store_token_tiles(ref, x):
    n = x.shape[0]
    for c in range(x.shape[1] // LANES):
        ref[pl.ds(c, n, stride=SUBLANES), :] = x[:, c * LANES:(c + 1) * LANES]


def _load_token_tiles(ref, n):
    return jnp.concatenate([ref[pl.ds(c, n, stride=SUBLANES), :] for c in range(SUBLANES)], axis=1)


def _token_copy(src_ref, src_tok, dst_ref, dst_tok, sem):
    src = src_ref.at[pl.ds(pl.multiple_of(src_tok * SUBLANES, SUBLANES), SUBLANES)]
    dst = dst_ref.at[pl.ds(pl.multiple_of(dst_tok * SUBLANES, SUBLANES), SUBLANES)]
    return pltpu.make_async_copy(src, dst, sem)


def _rel_bucket_np(n):
    n = np.maximum(n, 0)
    max_exact = REL_BUCKETS // 2
    nf = np.maximum(n, 1).astype(np.float32)
    large = max_exact + (np.log(nf / np.float32(max_exact)) / np.float32(math.log(REL_MAX_DIST / max_exact))
                         * np.float32(REL_BUCKETS - max_exact)).astype(np.int32)
    large = np.minimum(large, REL_BUCKETS - 1)
    return np.where(n < max_exact, n, large).astype(np.int32)


def _bucket_tiles(seq):
    r = np.arange(MOBA_BLOCK)
    d_own = r[:, None] - r[None, :]
    own = np.where(d_own >= 0, _rel_bucket_np(d_own), -1).astype(np.int32)
    prev = _rel_bucket_np(d_own + MOBA_BLOCK)
    far = _rel_bucket_np(np.arange(MOBA_BLOCK + 1, max(seq, MOBA_BLOCK + 2)))
    assert np.all(far == REL_BUCKETS - 1)
    first = np.concatenate([own, np.full_like(own, -1)], axis=1)
    later = np.concatenate([prev, own], axis=1)
    return np.stack([first, later])


def _bias_kernel(rb_ref, bucket_ref, near_ref):
    h = pl.program_id(0)
    far = rb_ref[REL_BUCKETS - 1, h]
    bucket = bucket_ref[...]
    tile = jnp.where(bucket < 0, MASK_VALUE, 0.0).astype(F32)
    for b in range(REL_BUCKETS):
        tile = jnp.where(bucket == b, rb_ref[b, h] - far, tile)
    near_ref[...] = tile


def _bias_tables(rel_bias, seq):
    buckets = _bucket_tiles(seq)
    blk = MOBA_BLOCK
    return pl.pallas_call(
        _bias_kernel,
        grid=(N_HEADS, 2),
        in_specs=[pl.BlockSpec(memory_space=pltpu.SMEM),
                  pl.BlockSpec((None, blk, 2 * blk), lambda h, v: (v, 0, 0))],
        out_specs=pl.BlockSpec((None, None, blk, 2 * blk), lambda h, v: (h, v, 0, 0)),
        out_shape=jax.ShapeDtypeStruct((N_HEADS, 2, blk, 2 * blk), F32),
        name="bias_tables",
    )(rel_bias.astype(F32), jnp.asarray(buckets))


def _inproj_kernel(x_ref, g_ref, w_ref, qkv_ref, u_ref):
    hb = _rms(x_ref[...], g_ref[...]).astype(BF16)
    scale = HEAD_DIM ** -0.5
    qkv_ref[:, :D_ATTN] = (_dot(hb, w_ref[:, :D_ATTN]) * scale).astype(BF16)
    qkv_ref[:, D_ATTN:2 * D_ATTN] = _dot(hb, w_ref[:, D_ATTN:2 * D_ATTN]).astype(BF16)
    qkv_ref[:, 2 * D_ATTN:] = _dot(hb, w_ref[:, 2 * D_ATTN:3 * D_ATTN]).astype(BF16)
    u_ref[...] = _dot(hb, w_ref[:, 3 * D_ATTN:])


def _in_proj(x2d, g_mix, w_in):
    t, d = x2d.shape
    n_in = w_in.shape[1]
    tm = min(TM_PROJ, t)
    return pl.pallas_call(
        _inproj_kernel,
        grid=(t // tm,),
        in_specs=[pl.BlockSpec((tm, d), lambda i: (i, 0)),
                  pl.BlockSpec((1, d), lambda i: (0, 0)),
                  pl.BlockSpec((d, n_in), lambda i: (0, 0))],
        out_specs=[pl.BlockSpec((tm, 3 * D_ATTN), lambda i: (i, 0)),
                   pl.BlockSpec((tm, D_POOL), lambda i: (i, 0))],
        out_shape=[jax.ShapeDtypeStruct((t, 3 * D_ATTN), BF16),
                   jax.ShapeDtypeStruct((t, D_POOL), F32)],
        compiler_params=pltpu.CompilerParams(dimension_semantics=("parallel",), vmem_limit_bytes=VMEM_LIMIT),
        name="in_proj",
    )(x2d, g_mix.reshape(1, d).astype(F32), w_in.astype(BF16))


FAR_BLOCKS = 2


def _attn_kernel(q_ref, k_ref, v_ref, nearb_ref, o_ref,
                 kaug, vaug, kmean, qnear, qfar, sbuf, m_s, l_s, acc_s, *, nb):
    blk = MOBA_BLOCK
    i = pl.program_id(2)

    @pl.when(i == 0)
    def _build_keys():
        kmean[...] = jnp.zeros(kmean.shape, F32)
        lane_b = lax.broadcasted_iota(jnp.int32, (blk, LANES), 1)

        def body(j, carry):
            r0 = pl.multiple_of(j * blk, blk)
            kb = k_ref[pl.ds(r0, blk), :]
            kaug[pl.ds(r0, blk), :LANES] = kb
            kaug[pl.ds(r0, blk), LANES:] = jnp.where(lane_b == j, 1.0, 0.0).astype(BF16)
            vaug[pl.ds(r0, blk), :LANES] = v_ref[pl.ds(r0, blk), :]
            vaug[pl.ds(r0, blk), LANES:] = jnp.ones((blk, LANES), BF16)
            kmean[pl.ds(j, 1), :] = jnp.mean(kb.astype(F32), axis=0, keepdims=True)
            return carry

        lax.fori_loop(0, nb, body, 0)

    lane = lax.broadcasted_iota(jnp.int32, (blk, LANES), 1)
    q = q_ref[...]
    kmean_b = kmean[...].astype(BF16)
    nbp = kmean.shape[0]
    blk_id = lax.broadcasted_iota(jnp.int32, (nbp, blk), 0)

    for hh in range(2):
        in_head = (lane < HEAD_DIM) if hh == 0 else (lane >= HEAD_DIM)
        qm = jnp.where(in_head, q, jnp.zeros_like(q))
        gate = jnp.where(blk_id < i, _dot_nt(kmean_b, qm), -jnp.inf)
        chosen_t = jnp.zeros((nbp, blk), F32)
        for _ in range(MOBA_TOPK):
            top = jnp.max(gate, axis=0, keepdims=True)
            idx = jnp.min(jnp.where(gate == top, blk_id, nbp), axis=0, keepdims=True)
            hit = blk_id == idx
            chosen_t = jnp.where(hit, 1.0, chosen_t)
            gate = jnp.where(hit, -jnp.inf, gate)
        if nbp < LANES:
            chosen_t = jnp.concatenate([chosen_t, jnp.zeros((LANES - nbp, blk), F32)], axis=0)
        penalty = jnp.where(jnp.transpose(chosen_t) > 0.0, 0.0, MASK_VALUE)
        qnear[hh, :, :LANES] = qm
        qnear[hh, :, LANES:] = jnp.where(lane < i, penalty, 0.0).astype(BF16)
        qfar[hh, :, :LANES] = qm
        qfar[hh, :, LANES:] = jnp.where(lane < i - 1, penalty, MASK_VALUE).astype(BF16)

    def fold(hh, s, vals, first):
        width = s.shape[1]
        m_blk = jnp.max(s, axis=-1, keepdims=True)
        if first:
            m_new = jnp.broadcast_to(m_blk, (blk, LANES))
        else:
            m_old = m_s[hh]
            m_new = jnp.maximum(m_old, m_blk)
            alpha = jnp.exp(m_old - m_new)
        p = jnp.exp(s - jnp.concatenate([m_new] * (width // LANES), axis=1))
        pv = _dot(p.astype(BF16), vals)
        if first:
            acc_s[hh] = pv[:, :LANES]
            l_s[hh] = pv[:, LANES:]
        else:
            acc_s[hh] = alpha * acc_s[hh] + pv[:, :LANES]
            l_s[hh] = alpha * l_s[hh] + pv[:, LANES:]
        m_s[hh] = m_new

    near0 = pl.multiple_of(jnp.maximum(i - 1, 0) * blk, blk)
    keys = kaug[pl.ds(near0, 2 * blk), :]
    near_scores = [_dot_nt(qnear[hh], keys) + nearb_ref[hh] for hh in range(2)]

    chunk = FAR_BLOCKS * blk
    last_chunk = kaug.shape[0] // chunk - 1
    n_far = (i + FAR_BLOCKS - 2) // FAR_BLOCKS

    def chunk_start(c):
        return pl.multiple_of(jnp.minimum(c, last_chunk) * chunk, chunk)

    def far_scores(c, slot):
        keys = kaug[pl.ds(chunk_start(c), chunk), :]
        for hh in range(2):
            sbuf[slot, hh] = _dot_nt(qfar[hh], keys)

    def far_fold(c, slot):
        vals = vaug[pl.ds(chunk_start(c), chunk), :]
        for hh in range(2):
            fold(hh, sbuf[slot, hh], vals, False)

    far_scores(0, 0)
    vals = vaug[pl.ds(near0, 2 * blk), :]
    for hh in range(2):
        fold(hh, near_scores[hh], vals, True)

    def far_body(cc, carry):
        c = 2 * cc
        far_scores(c + 1, 1)
        far_fold(c, 0)
        far_scores(c + 2, 0)
        far_fold(c + 1, 1)
        return carry

    lax.fori_loop(0, (n_far + 1) // 2, far_body, 0)

    out_a = acc_s[0] / l_s[0]
    out_b = acc_s[1] / l_s[1]
    o_ref[...] = jnp.where(lane < HEAD_DIM, out_a, out_b).astype(o_ref.dtype)


def _moba_attention(qkv, near_bias):
    b, s, _ = qkv.shape
    blk = MOBA_BLOCK
    nb = s // blk
    nbp = -(-nb // SUBLANES) * SUBLANES
    n_pairs = D_ATTN // PAIR
    assert s % (FAR_BLOCKS * blk) == 0 and nbp <= LANES
    max_far = (nb - 1 + FAR_BLOCKS - 2) // FAR_BLOCKS
    assert max_far % 2 == 0 or max_far <= nb // FAR_BLOCKS - 1
    kern = functools.partial(_attn_kernel, nb=nb)
    return pl.pallas_call(
        kern,
        grid=(b, n_pairs, nb),
        in_specs=[pl.BlockSpec((None, blk, PAIR), lambda bi, hp, i: (bi, i, hp)),
                  pl.BlockSpec((None, s, PAIR), lambda bi, hp, i: (bi, 0, n_pairs + hp)),
                  pl.BlockSpec((None, s, PAIR), lambda bi, hp, i: (bi, 0, 2 * n_pairs + hp)),
                  pl.BlockSpec((2, None, blk, 2 * blk), lambda bi, hp, i: (hp, jnp.minimum(i, 1), 0, 0))],
        out_specs=pl.BlockSpec((None, blk, PAIR), lambda bi, hp, i: (bi, i, hp)),
        out_shape=jax.ShapeDtypeStruct((b, s, D_ATTN), BF16),
        scratch_shapes=[pltpu.VMEM((s, 2 * LANES), BF16),
                        pltpu.VMEM((s, 2 * LANES), BF16),
                        pltpu.VMEM((nbp, LANES), F32),
                        pltpu.VMEM((2, blk, 2 * LANES), BF16),
                        pltpu.VMEM((2, blk, 2 * LANES), BF16),
                        pltpu.VMEM((2, 2, blk, FAR_BLOCKS * blk), F32),
                        pltpu.VMEM((2, blk, LANES), F32),
                        pltpu.VMEM((2, blk, LANES), F32),
                        pltpu.VMEM((2, blk, LANES), F32)],
        compiler_params=pltpu.CompilerParams(
            dimension_semantics=("parallel", "parallel", "arbitrary"), vmem_limit_bytes=VMEM_LIMIT),
        name="moba_attn",
    )(qkv, qkv, qkv, near_bias)


def _mix_kernel(x_ref, a_ref, u_ref, uh_ref, wpool_ref, pscale_ref, wout_ref, gffn_ref,
                wr_hi_ref, wr_lo_ref, br_ref,
                x2_ref, h2_ref, meta_ref, metat_ref, cnt_ref, ext_s, carry_s, *, tm, seq):
    t = pl.program_id(0)

    @pl.when(t == 0)
    def _init():
        carry_s[...] = jnp.zeros(carry_s.shape, F32)

    pos0 = (t * tm) % seq
    ext_s[0:HALO, :] = jnp.where(pos0 == 0, 0.0, uh_ref[...])
    ext_s[HALO:HALO + tm, :] = u_ref[...]
    pos = pos0 + lax.broadcasted_iota(jnp.int32, (tm, 1), 0)
    parts = []
    for gi, w in enumerate(POOL_WINDOWS):
        c0 = gi * POOL_GROUP_DIM
        cols = slice(c0, c0 + POOL_GROUP_DIM)
        tok = ext_s[HALO:HALO + tm, cols]
        win = tok
        for d in range(1, w):
            win = win + ext_s[HALO - d:HALO - d + tm, cols]
        cnt = jnp.minimum(pos + 1, w).astype(F32)
        pooled = win / cnt - tok
        parts.append(_dot(pooled.astype(BF16), wpool_ref[gi]) * pscale_ref[:, cols])
    b = jnp.concatenate(parts, axis=1).astype(BF16)

    mix = _dot(a_ref[...], wout_ref[:D_ATTN, :]) + _dot(b, wout_ref[D_ATTN:, :])
    x2 = x_ref[...] + mix
    x2_ref[...] = x2
    h2 = _rms(x2, gffn_ref[...])
    _store_token_tiles(h2_ref, h2)

    hi = h2.astype(BF16)
    lo = (h2 - hi.astype(F32)).astype(BF16)
    logits = _dot(hi, wr_hi_ref[...]) + _dot(lo, wr_hi_ref[...]) + _dot(hi, wr_lo_ref[...]) + br_ref[...]

    lane = lax.broadcasted_iota(jnp.int32, (tm, LANES), 1)
    is_group = lane < N_GROUPS
    l1 = jnp.where(is_group, logits, -jnp.inf)
    top_g = jnp.max(l1, axis=-1, keepdims=True)
    g_idx = jnp.min(jnp.where(l1 == top_g, lane, LANES), axis=-1, keepdims=True)
    denom = jnp.sum(jnp.where(is_group, jnp.exp(logits - top_g), 0.0), axis=-1, keepdims=True)
    p_g = 1.0 / denom
    lane_group = jnp.where(lane >= N_GROUPS, (lane - N_GROUPS) // E_PER_GROUP, -1)
    l2 = jnp.where(lane_group == g_idx, logits, -jnp.inf)
    v1 = jnp.max(l2, axis=-1, keepdims=True)
    i1 = jnp.min(jnp.where(l2 == v1, lane, LANES), axis=-1, keepdims=True)
    l2 = jnp.where(lane == i1, -jnp.inf, l2)
    v2 = jnp.max(l2, axis=-1, keepdims=True)
    i2 = jnp.min(jnp.where(l2 == v2, lane, LANES), axis=-1, keepdims=True)
    e1 = i1 - N_GROUPS
    e2 = i2 - N_GROUPS
    t2 = jnp.exp(v2 - v1)
    w1 = 1.0 / (1.0 + t2)
    gate1 = p_g * w1
    gate2 = p_g * (t2 * w1)

    oh1 = jnp.where(lane == e1, 1.0, 0.0)
    oh2 = jnp.where(lane == e2, 1.0, 0.0)
    oh = oh1 + oh2
    row = lax.broadcasted_iota(jnp.int32, (tm, tm), 0)
    col = lax.broadcasted_iota(jnp.int32, (tm, tm), 1)
    before = jnp.where(col < row, 1.0, 0.0).astype(BF16)
    seen = _dot(before, oh.astype(BF16)) + carry_s[0:1, :]
    r1 = jnp.sum(oh1 * seen, axis=-1, keepdims=True)
    r2 = jnp.sum(oh2 * seen, axis=-1, keepdims=True)
    carry_s[...] = carry_s[...] + jnp.sum(oh, axis=0, keepdims=True)
    cnt_ref[...] = carry_s[...]

    meta = jnp.zeros((tm, LANES), F32)
    for k, val in enumerate((e1.astype(F32), e2.astype(F32), r1, r2, gate1, gate2)):
        meta = jnp.where(lane == k, val, meta)
    meta_ref[...] = meta
    metat_ref[...] = jnp.transpose(meta)[:SUBLANES, :].astype(jnp.int32)


def _mix_route(x2d, a2d, u2d, w_pool, pool_scale, w_out, g_ffn, wr, br, seq):
    t, d = x2d.shape
    tm = TM
    assert seq % tm == 0 and tm % HALO == 0
    hb = tm // HALO
    kern = functools.partial(_mix_kernel, tm=tm, seq=seq)
    tok = lambda w: pl.BlockSpec((tm, w), lambda i: (i, 0))
    full = lambda *shape: pl.BlockSpec(shape, lambda i: (0,) * len(shape))
    wr_hi = wr.astype(BF16)
    wr_lo = (wr - wr_hi.astype(F32)).astype(BF16)
    return pl.pallas_call(
        kern,
        grid=(t // tm,),
        in_specs=[tok(d), tok(D_ATTN), tok(D_POOL),
                  pl.BlockSpec((HALO, D_POOL), lambda i: (jnp.maximum(i * hb - 1, 0), 0)),
                  full(len(POOL_WINDOWS), POOL_GROUP_DIM, POOL_GROUP_DIM), full(1, D_POOL),
                  full(D_ATTN + D_POOL, d), full(1, d), full(d, LANES), full(d, LANES), full(1, LANES)],
        out_specs=[tok(d), pl.BlockSpec((tm * SUBLANES, LANES), lambda i: (i, 0)), tok(LANES),
                   pl.BlockSpec((SUBLANES, tm), lambda i: (0, i)),
                   full(SUBLANES, LANES)],
        out_shape=[jax.ShapeDtypeStruct((t, d), F32), jax.ShapeDtypeStruct((t * SUBLANES, LANES), F32),
                   jax.ShapeDtypeStruct((t, LANES), F32),
                   jax.ShapeDtypeStruct((SUBLANES, t), jnp.int32),
                   jax.ShapeDtypeStruct((SUBLANES, LANES), F32)],
        scratch_shapes=[pltpu.VMEM((HALO + tm, D_POOL), F32), pltpu.VMEM((SUBLANES, LANES), F32)],
        compiler_params=pltpu.CompilerParams(dimension_semantics=("arbitrary",), vmem_limit_bytes=VMEM_LIMIT),
        name="mix_route",
    )(x2d, a2d, u2d, u2d, w_pool.astype(BF16), pool_scale.reshape(1, D_POOL).astype(F32),
      w_out.astype(BF16), g_ffn.reshape(1, d).astype(F32), wr_hi, wr_lo, br)


ISSUE_UNROLL = 8


def _pos_kernel(offs_ref, meta_ref, pos_ref):
    meta = meta_ref[...]
    start = jnp.zeros(meta.shape, jnp.int32)
    for e in range(N_EXPERTS):
        start = jnp.where(meta == e, offs_ref[e], start)
    pos_ref[...] = start[0:2, :] + meta[2:4, :]


def _moe_positions(offs, metat, tm):
    t = metat.shape[1]
    pos = pl.pallas_call(
        _pos_kernel,
        in_specs=[pl.BlockSpec(memory_space=pltpu.SMEM), pl.BlockSpec(memory_space=pltpu.VMEM)],
        out_specs=pl.BlockSpec(memory_space=pltpu.VMEM),
        out_shape=jax.ShapeDtypeStruct((2, t), jnp.int32),
        name="moe_pos",
    )(offs, metat)
    return pos.reshape(2, t // tm, tm).transpose(1, 0, 2).reshape(2 * t)


def _scatter_kernel(pos_ref, h2_ref, xs_ref, sem, *, tm):
    def issue(r, carry):
        for k in range(2):
            _token_copy(h2_ref, r, xs_ref, pos_ref[k * tm + r], sem).start()
        return carry

    lax.fori_loop(0, tm, issue, 0, unroll=ISSUE_UNROLL)

    def drain(r, carry):
        for k in range(2):
            _token_copy(h2_ref, 0, xs_ref, 0, sem).wait()
        return carry

    lax.fori_loop(0, tm, drain, 0, unroll=ISSUE_UNROLL)


def _moe_scatter(pos, h2t):
    rows = h2t.shape[0]
    tm = TM
    kern = functools.partial(_scatter_kernel, tm=tm)
    return pl.pallas_call(
        kern,
        grid=(rows // (tm * SUBLANES),),
        in_specs=[pl.BlockSpec((2 * tm,), lambda i: (i,), memory_space=pltpu.SMEM),
                  pl.BlockSpec((tm * SUBLANES, LANES), lambda i: (i, 0))],
        out_specs=pl.BlockSpec(memory_space=pl.ANY),
        out_shape=jax.ShapeDtypeStruct((2 * rows, LANES), F32),
        scratch_shapes=[pltpu.SemaphoreType.DMA(())],
        compiler_params=pltpu.CompilerParams(dimension_semantics=("arbitrary",), vmem_limit_bytes=VMEM_LIMIT),
        name="moe_scatter",
    )(pos, h2t)


def _expert_kernel(tile_ref, exp_ref, lo_ref, xs_ref, wg_ref, wu_ref, wd_ref, ys_ref, *, tm):
    s = pl.program_id(0)
    lo = lo_ref[s]

    @pl.when(lo < tm)
    def _compute():
        x = _load_token_tiles(xs_ref, tm).astype(BF16)
        a = _dot(x, wg_ref[...].astype(BF16))
        u = _dot(x, wu_ref[...].astype(BF16))
        hid = (a * jax.nn.sigmoid(a) * u).astype(BF16)
        y = _dot(hid, wd_ref[...].astype(BF16))

        @pl.when(lo == 0)
        def _first():
            _store_token_tiles(ys_ref, y)

        @pl.when(lo > 0)
        def _later():
            row = lax.broadcasted_iota(jnp.int32, (tm, 1), 0)
            _store_token_tiles(ys_ref, jnp.where(row >= lo, y, _load_token_tiles(ys_ref, tm)))


def _moe_experts(step_tile, step_exp, step_lo, xs, w_gate, w_up, w_down):
    n_rows = xs.shape[0]
    tm = TM
    n_e, d, f = w_gate.shape
    n_steps = step_tile.shape[0]
    kern = functools.partial(_expert_kernel, tm=tm)
    grid_spec = pltpu.PrefetchScalarGridSpec(
        num_scalar_prefetch=3,
        grid=(n_steps,),
        in_specs=[pl.BlockSpec((tm * SUBLANES, LANES), lambda s, tl, ex, lo: (tl[s], 0)),
                  pl.BlockSpec((None, d, f), lambda s, tl, ex, lo: (ex[s], 0, 0)),
                  pl.BlockSpec((None, d, f), lambda s, tl, ex, lo: (ex[s], 0, 0)),
                  pl.BlockSpec((None, f, d), lambda s, tl, ex, lo: (ex[s], 0, 0))],
        out_specs=pl.BlockSpec((tm * SUBLANES, LANES), lambda s, tl, ex, lo: (tl[s], 0)),
    )
    return pl.pallas_call(
        kern,
        grid_spec=grid_spec,
        out_shape=jax.ShapeDtypeStruct((n_rows, LANES), F32),
        compiler_params=pltpu.CompilerParams(dimension_semantics=("arbitrary",), vmem_limit_bytes=VMEM_LIMIT),
        name="moe_experts",
    )(step_tile, step_exp, step_lo, xs, w_gate, w_up, w_down)


def _expert_steps(counts, n_rows, tm):
    n_e = counts.shape[0]
    n_tiles = n_rows // tm
    n_steps = n_tiles + n_e
    ends = jnp.cumsum(counts)
    starts = ends - counts
    first_tile = starts // tm
    last_tile = jnp.maximum(ends - 1, 0) // tm
    tiles_e = jnp.where(counts > 0, last_tile - first_tile + 1, 0)
    step_end = jnp.cumsum(tiles_e)
    step_start = step_end - tiles_e
    total = step_end[-1]
    s = jnp.arange(n_steps, dtype=jnp.int32)
    s_eff = jnp.minimum(s, total - 1)
    exp = jnp.sum((step_end[None, :] <= s_eff[:, None]).astype(jnp.int32), axis=1)
    tile = first_tile[exp] + (s_eff - step_start[exp])
    lo = jnp.maximum(starts[exp] - tile * tm, 0)
    lo = jnp.where(s < total, lo, tm)
    return tile.astype(jnp.int32), exp, lo.astype(jnp.int32), starts.astype(jnp.int32)


def _combine_kernel(pos_ref, x2_ref, gates_ref, p_ref, wproj_ref, wgate_ref, gple_ref, gfin_ref,
                    ys_ref, out_ref, rows_s, sem, *, tm):
    def issue(r, carry):
        for k in range(2):
            _token_copy(ys_ref, pos_ref[k * tm + r], rows_s.at[k], r, sem).start()
        return carry

    lax.fori_loop(0, tm, issue, 0, unroll=ISSUE_UNROLL)

    def drain(r, carry):
        for k in range(2):
            _token_copy(ys_ref, 0, rows_s.at[k], 0, sem).wait()
        return carry

    lax.fori_loop(0, tm, drain, 0, unroll=ISSUE_UNROLL)

    gates = gates_ref[...]
    y = gates[:, 4:5] * _load_token_tiles(rows_s.at[0], tm) + gates[:, 5:6] * _load_token_tiles(rows_s.at[1], tm)
    x3 = x2_ref[...] + y
    h3 = _rms(x3, gple_ref[...]).astype(BF16)
    gate = jax.nn.sigmoid(_dot(h3, wgate_ref[...]))
    pe = _dot(p_ref[...].astype(BF16), wproj_ref[...])
    x4 = x3 + pe * gate
    out_ref[...] = _rms(x4, gfin_ref[...])


def _combine_ple(pos, x2, meta, p2d, w_ple_proj, w_ple_gate, g_ple, g_final, ys):
    t, d = x2.shape
    d_ple = p2d.shape[1]
    tm = TM
    kern = functools.partial(_combine_kernel, tm=tm)
    tok = lambda w: pl.BlockSpec((tm, w), lambda i: (i, 0))
    full = lambda *shape: pl.BlockSpec(shape, lambda i: (0,) * len(shape))
    return pl.pallas_call(
        kern,
        grid=(t // tm,),
        in_specs=[pl.BlockSpec((2 * tm,), lambda i: (i,), memory_space=pltpu.SMEM),
                  tok(d), tok(LANES), tok(d_ple), full(d_ple, d), full(d, d), full(1, d), full(1, d),
                  pl.BlockSpec(memory_space=pl.ANY)],
        out_specs=tok(d),
        out_shape=jax.ShapeDtypeStruct((t, d), F32),
        scratch_shapes=[pltpu.VMEM((2, tm * SUBLANES, LANES), F32), pltpu.SemaphoreType.DMA(())],
        compiler_params=pltpu.CompilerParams(dimension_semantics=("arbitrary",), vmem_limit_bytes=VMEM_LIMIT),
        name="combine_ple",
    )(pos, x2, meta, p2d, w_ple_proj.astype(BF16), w_ple_gate.astype(BF16),
      g_ple.reshape(1, d).astype(F32), g_final.reshape(1, d).astype(F32), ys)


def _layer(x2d, p2d, seq, g_mix, w_in, w_pool, pool_scale, w_out, near_bias, g_ffn,
           w_r1, b_r1, w_r2, b_r2, w_gate, w_up, w_down, g_ple, w_ple_proj, w_ple_gate, g_final):
    t, d = x2d.shape
    batch = t // seq
    qkv, u = _in_proj(x2d, g_mix, w_in)
    a = _moba_attention(qkv.reshape(batch, seq, 3 * D_ATTN), near_bias).reshape(t, D_ATTN)

    wr2 = jnp.transpose(w_r2, (1, 0, 2)).reshape(d, N_EXPERTS)
    pad = LANES - N_GROUPS - N_EXPERTS
    wr = jnp.concatenate([w_r1, wr2, jnp.zeros((d, pad), F32)], axis=1).astype(F32)
    br = jnp.concatenate([b_r1, b_r2.reshape(N_EXPERTS), jnp.zeros((pad,), F32)]).reshape(1, LANES).astype(F32)
    assert d == SUBLANES * LANES, "token-tiled rows assume one (8,128) tile per token"
    x2, h2t, meta, metat, cnt = _mix_route(x2d, a, u, w_pool, pool_scale, w_out, g_ffn, wr, br, seq)

    counts = cnt[0, :N_EXPERTS].astype(jnp.int32)
    step_tile, step_exp, step_lo, offs = _expert_steps(counts, 2 * t, TM)
    pos = _moe_positions(offs, metat, TM)
    xs = _moe_scatter(pos, h2t)
    f = w_gate.shape[-1]
    ys = _moe_experts(step_tile, step_exp, step_lo, xs,
                      w_gate.reshape(N_EXPERTS, d, f), w_up.reshape(N_EXPERTS, d, f),
                      w_down.reshape(N_EXPERTS, f, d))
    return _combine_ple(pos, x2, meta, p2d, w_ple_proj, w_ple_gate, g_ple, g_final, ys)


def kernel(x, p, g_mix, w_in, w_pool, pool_scale, w_out, rel_bias, g_ffn, w_r1, b_r1, w_r2, b_r2,
           w_gate, w_up, w_down, g_ple, w_ple_proj, w_ple_gate, g_final):
    batch, seq, d = x.shape
    depth = p.shape[0]
    assert depth == 1, "the final norm is fused into the last stage of a single layer"
    near_bias = _bias_tables(rel_bias, seq)
    out = _layer(x.reshape(batch * seq, d), p[0].reshape(batch * seq, -1), seq,
                 g_mix[0], w_in[0], w_pool[0], pool_scale[0], w_out[0], near_bias, g_ffn[0],
                 w_r1[0], b_r1[0], w_r2[0], b_r2[0], w_gate[0], w_up[0], w_down[0],
                 g_ple[0], w_ple_proj[0], w_ple_gate[0], g_final)
    return out.reshape(batch, seq, d)
```

```python
import functools
import math

import numpy as np
import jax
import jax.numpy as jnp
from jax import lax
from jax.experimental import pallas as pl
from jax.experimental.pallas import tpu as pltpu

F32 = jnp.float32
BF16 = jnp.bfloat16

HEAD_DIM = 64
N_HEADS = 8
D_ATTN = N_HEADS * HEAD_DIM
POOL_WINDOWS = (2, 4, 8, 16)
POOL_GROUP_DIM = 128
D_POOL = POOL_GROUP_DIM * len(POOL_WINDOWS)
MOBA_BLOCK = 256
MOBA_TOPK = 3
REL_BUCKETS = 32
REL_MAX_DIST = 128
N_GROUPS = 4
E_PER_GROUP = 8
N_EXPERTS = N_GROUPS * E_PER_GROUP
RMS_EPS = 1e-6

LANES = 128
SUBLANES = 8
VMEM_LIMIT = 56 * 1024 * 1024

MASK_VALUE = -1e30
LOG2E = math.log2(math.e)
HALO = max(POOL_WINDOWS)
TM_PROJ = 512
TM = 256
PAIR = 2 * HEAD_DIM


def _rms(x, g):
    return x * lax.rsqrt(jnp.mean(x * x, axis=-1, keepdims=True) + RMS_EPS) * g


def _dot(a, b):
    return jnp.dot(a, b, preferred_element_type=F32)


def _dot_nt(a, b):
    return lax.dot_general(a, b, (((1,), (1,)), ((), ())), preferred_element_type=F32)


def _store_token_tiles(ref, x):
    n = x.shape[0]
    for c in range(x.shape[1] // LANES):
        ref[pl.ds(c, n, stride=SUBLANES), :] = x[:, c * LANES:(c + 1) * LANES]


def _load_token_tiles(ref, n):
    return jnp.concatenate([ref[pl.ds(c, n, stride=SUBLANES), :] for c in range(SUBLANES)], axis=1)


def _token_copy(src_ref, src_tok, dst_ref, dst_tok, sem):
    src = src_ref.at[pl.ds(pl.multiple_of(src_tok * SUBLANES, SUBLANES), SUBLANES)]
    dst = dst_ref.at[pl.ds(pl.multiple_of(dst_tok * SUBLANES, SUBLANES), SUBLANES)]
    return pltpu.make_async_copy(src, dst, sem)


def _rel_bucket_np(n):
    n = np.maximum(n, 0)
    max_exact = REL_BUCKETS // 2
    nf = np.maximum(n, 1).astype(np.float32)
    large = max_exact + (np.log(nf / np.float32(max_exact)) / np.float32(math.log(REL_MAX_DIST / max_exact))
                         * np.float32(REL_BUCKETS - max_exact)).astype(np.int32)
    large = np.minimum(large, REL_BUCKETS - 1)
    return np.where(n < max_exact, n, large).astype(np.int32)


def _bucket_tiles(seq):
    r = np.arange(MOBA_BLOCK)
    d_own = r[:, None] - r[None, :]
    own = np.where(d_own >= 0, _rel_bucket_np(d_own), -1).astype(np.int32)
    prev = _rel_bucket_np(d_own + MOBA_BLOCK)
    far = _rel_bucket_np(np.arange(MOBA_BLOCK + 1, max(seq, MOBA_BLOCK + 2)))
    assert np.all(far == REL_BUCKETS - 1)
    first = np.concatenate([own, np.full_like(own, -1)], axis=1)
    later = np.concatenate([prev, own], axis=1)
    return np.stack([first, later])


def _bias_kernel(rb_ref, bucket_ref, near_ref):
    h = pl.program_id(0)
    far = rb_ref[REL_BUCKETS - 1, h]
    bucket = bucket_ref[...]
    tile = jnp.where(bucket < 0, MASK_VALUE, 0.0).astype(F32)
    for b in range(REL_BUCKETS):
        tile = jnp.where(bucket == b, (rb_ref[b, h] - far) * LOG2E, tile)
    near_ref[...] = tile


def _bias_tables(rel_bias, seq):
    buckets = _bucket_tiles(seq)
    blk = MOBA_BLOCK
    return pl.pallas_call(
        _bias_kernel,
        grid=(N_HEADS, 2),
        in_specs=[pl.BlockSpec(memory_space=pltpu.SMEM),
                  pl.BlockSpec((None, blk, 2 * blk), lambda h, v: (v, 0, 0))],
        out_specs=pl.BlockSpec((None, None, blk, 2 * blk), lambda h, v: (h, v, 0, 0)),
        out_shape=jax.ShapeDtypeStruct((N_HEADS, 2, blk, 2 * blk), F32),
        name="bias_tables",
    )(rel_bias.astype(F32), jnp.asarray(buckets))


def _inproj_kernel(x_ref, g_ref, w_ref, qkv_ref, u_ref):
    hb = _rms(x_ref[...], g_ref[...]).astype(BF16)
    scale = HEAD_DIM ** -0.5 * LOG2E
    qkv_ref[:, :D_ATTN] = (_dot(hb, w_ref[:, :D_ATTN]) * scale).astype(BF16)
    qkv_ref[:, D_ATTN:2 * D_ATTN] = _dot(hb, w_ref[:, D_ATTN:2 * D_ATTN]).astype(BF16)
    qkv_ref[:, 2 * D_ATTN:] = _dot(hb, w_ref[:, 2 * D_ATTN:3 * D_ATTN]).astype(BF16)
    u_ref[...] = _dot(hb, w_ref[:, 3 * D_ATTN:])


def _in_proj(x2d, g_mix, w_in):
    t, d = x2d.shape
    n_in = w_in.shape[1]
    tm = min(TM_PROJ, t)
    return pl.pallas_call(
        _inproj_kernel,
        grid=(t // tm,),
        in_specs=[pl.BlockSpec((tm, d), lambda i: (i, 0)),
                  pl.BlockSpec((1, d), lambda i: (0, 0)),
                  pl.BlockSpec((d, n_in), lambda i: (0, 0))],
        out_specs=[pl.BlockSpec((tm, 3 * D_ATTN), lambda i: (i, 0)),
                   pl.BlockSpec((tm, D_POOL), lambda i: (i, 0))],
        out_shape=[jax.ShapeDtypeStruct((t, 3 * D_ATTN), BF16),
                   jax.ShapeDtypeStruct((t, D_POOL), F32)],
        compiler_params=pltpu.CompilerParams(dimension_semantics=("parallel",), vmem_limit_bytes=VMEM_LIMIT),
        name="in_proj",
    )(x2d, g_mix.reshape(1, d).astype(F32), w_in.astype(BF16))


FAR_BLOCKS = 2


def _attn_kernel(q_ref, k_ref, v_ref, nearb_ref, o_ref,
                 kaug, vaug, kmean, qnear, qfar, sbuf, m_s, l_s, acc_s, *, nb):
    blk = MOBA_BLOCK
    i = pl.program_id(2)

    @pl.when(i == 0)
    def _build_keys():
        kmean[...] = jnp.zeros(kmean.shape, F32)
        lane_b = lax.broadcasted_iota(jnp.int32, (blk, LANES), 1)

        def body(j, carry):
            r0 = pl.multiple_of(j * blk, blk)
            kb = k_ref[pl.ds(r0, blk), :]
            kaug[pl.ds(r0, blk), :LANES] = kb
            kaug[pl.ds(r0, blk), LANES:] = jnp.where(lane_b == j, 1.0, 0.0).astype(BF16)
            vaug[pl.ds(r0, blk), :LANES] = v_ref[pl.ds(r0, blk), :]
            vaug[pl.ds(r0, blk), LANES:] = jnp.ones((blk, LANES), BF16)
            kmean[pl.ds(j, 1), :] = jnp.mean(kb.astype(F32), axis=0, keepdims=True)
            return carry

        lax.fori_loop(0, nb, body, 0)

    lane = lax.broadcasted_iota(jnp.int32, (blk, LANES), 1)
    q = q_ref[...]
    kmean_b = kmean[...].astype(BF16)
    nbp = kmean.shape[0]
    blk_id = lax.broadcasted_iota(jnp.int32, (nbp, blk), 0)

    for hh in range(2):
        in_head = (lane < HEAD_DIM) if hh == 0 else (lane >= HEAD_DIM)
        qm = jnp.where(in_head, q, jnp.zeros_like(q))
        gate = jnp.where(blk_id < i, _dot_nt(kmean_b, qm), -jnp.inf)
        chosen_t = jnp.zeros((nbp, blk), F32)
        for _ in range(MOBA_TOPK):
            top = jnp.max(gate, axis=0, keepdims=True)
            idx = jnp.min(jnp.where(gate == top, blk_id, nbp), axis=0, keepdims=True)
            hit = blk_id == idx
            chosen_t = jnp.where(hit, 1.0, chosen_t)
            gate = jnp.where(hit, -jnp.inf, gate)
        if nbp < LANES:
            chosen_t = jnp.concatenate([chosen_t, jnp.zeros((LANES - nbp, blk), F32)], axis=0)
        penalty = jnp.where(jnp.transpose(chosen_t) > 0.0, 0.0, MASK_VALUE)
        qnear[hh, :, :LANES] = qm
        qnear[hh, :, LANES:] = jnp.where(lane < i, penalty, 0.0).astype(BF16)
        qfar[hh, :, :LANES] = qm
        qfar[hh, :, LANES:] = jnp.where(lane < i - 1, penalty, MASK_VALUE).astype(BF16)

    def fold(hh, s, vals, first):
        width = s.shape[1]
        m_blk = jnp.max(s, axis=-1, keepdims=True)
        if first:
            m_new = jnp.broadcast_to(m_blk, (blk, LANES))
        else:
            m_old = m_s[hh]
            m_new = jnp.maximum(m_old, m_blk)
            alpha = jnp.exp2(m_old - m_new)
        p = jnp.exp2(s - jnp.concatenate([m_new] * (width // LANES), axis=1))
        pv = _dot(p.astype(BF16), vals)
        if first:
            acc_s[hh] = pv[:, :LANES]
            l_s[hh] = pv[:, LANES:]
        else:
            acc_s[hh] = alpha * acc_s[hh] + pv[:, :LANES]
            l_s[hh] = alpha * l_s[hh] + pv[:, LANES:]
        m_s[hh] = m_new

    near0 = pl.multiple_of(jnp.maximum(i - 1, 0) * blk, blk)
    keys = kaug[pl.ds(near0, 2 * blk), :]
    near_scores = [_dot_nt(qnear[hh], keys) + nearb_ref[hh] for hh in range(2)]

    chunk = FAR_BLOCKS * blk
    last_chunk = kaug.shape[0] // chunk - 1
    n_far = (i + FAR_BLOCKS - 2) // FAR_BLOCKS

    def chunk_start(c):
        return pl.multiple_of(jnp.minimum(c, last_chunk) * chunk, chunk)

    def far_scores(c, slot):
        keys = kaug[pl.ds(chunk_start(c), chunk), :]
        for hh in range(2):
            sbuf[slot, hh] = _dot_nt(qfar[hh], keys)

    def far_fold(c, slot):
        vals = vaug[pl.ds(chunk_start(c), chunk), :]
        for hh in range(2):
            fold(hh, sbuf[slot, hh], vals, False)

    far_scores(0, 0)
    vals = vaug[pl.ds(near0, 2 * blk), :]
    for hh in range(2):
        fold(hh, near_scores[hh], vals, True)

    def far_body(cc, carry):
        c = 2 * cc
        far_scores(c + 1, 1)
        far_fold(c, 0)
        far_scores(c + 2, 0)
        far_fold(c + 1, 1)
        return carry

    lax.fori_loop(0, (n_far + 1) // 2, far_body, 0)

    out_a = acc_s[0] / l_s[0]
    out_b = acc_s[1] / l_s[1]
    o_ref[...] = jnp.where(lane < HEAD_DIM, out_a, out_b).astype(o_ref.dtype)


def _moba_attention(qkv, near_bias):
    b, s, _ = qkv.shape
    blk = MOBA_BLOCK
    nb = s // blk
    nbp = -(-nb // SUBLANES) * SUBLANES
    n_pairs = D_ATTN // PAIR
    assert s % (FAR_BLOCKS * blk) == 0 and nbp <= LANES
    max_far = (nb - 1 + FAR_BLOCKS - 2) // FAR_BLOCKS
    assert max_far % 2 == 0 or max_far <= nb // FAR_BLOCKS - 1
    kern = functools.partial(_attn_kernel, nb=nb)
    return pl.pallas_call(
        kern,
        grid=(b, n_pairs, nb),
        in_specs=[pl.BlockSpec((None, blk, PAIR), lambda bi, hp, i: (bi, i, hp)),
                  pl.BlockSpec((None, s, PAIR), lambda bi, hp, i: (bi, 0, n_pairs + hp)),
                  pl.BlockSpec((None, s, PAIR), lambda bi, hp, i: (bi, 0, 2 * n_pairs + hp)),
                  pl.BlockSpec((2, None, blk, 2 * blk), lambda bi, hp, i: (hp, jnp.minimum(i, 1), 0, 0))],
        out_specs=pl.BlockSpec((None, blk, PAIR), lambda bi, hp, i: (bi, i, hp)),
        out_shape=jax.ShapeDtypeStruct((b, s, D_ATTN), BF16),
        scratch_shapes=[pltpu.VMEM((s, 2 * LANES), BF16),
                        pltpu.VMEM((s, 2 * LANES), BF16),
                        pltpu.VMEM((nbp, LANES), F32),
                        pltpu.VMEM((2, blk, 2 * LANES), BF16),
                        pltpu.VMEM((2, blk, 2 * LANES), BF16),
                        pltpu.VMEM((2, 2, blk, FAR_BLOCKS * blk), F32),
                        pltpu.VMEM((2, blk, LANES), F32),
                        pltpu.VMEM((2, blk, LANES), F32),
                        pltpu.VMEM((2, blk, LANES), F32)],
        compiler_params=pltpu.CompilerParams(
            dimension_semantics=("parallel", "parallel", "arbitrary"), vmem_limit_bytes=VMEM_LIMIT),
        name="moba_attn",
    )(qkv, qkv, qkv, near_bias)


def _mix_kernel(x_ref, a_ref, u_ref, uh_ref, wpool_ref, pscale_ref, wout_ref, gffn_ref,
                wr_hi_ref, wr_lo_ref, br_ref,
                x2_ref, h2_ref, meta_ref, metat_ref, cnt_ref, ext_s, carry_s, *, tm, seq):
    t = pl.program_id(0)

    @pl.when(t == 0)
    def _init():
        carry_s[...] = jnp.zeros(carry_s.shape, F32)

    pos0 = (t * tm) % seq
    ext_s[0:HALO, :] = jnp.where(pos0 == 0, 0.0, uh_ref[...])
    ext_s[HALO:HALO + tm, :] = u_ref[...]
    pos = pos0 + lax.broadcasted_iota(jnp.int32, (tm, 1), 0)
    parts = []
    for gi, w in enumerate(POOL_WINDOWS):
        c0 = gi * POOL_GROUP_DIM
        cols = slice(c0, c0 + POOL_GROUP_DIM)
        tok = ext_s[HALO:HALO + tm, cols]
        win = tok
        for d in range(1, w):
            win = win + ext_s[HALO - d:HALO - d + tm, cols]
        cnt = jnp.minimum(pos + 1, w).astype(F32)
        pooled = win / cnt - tok
        parts.append(_dot(pooled.astype(BF16), wpool_ref[gi]) * pscale_ref[:, cols])
    b = jnp.concatenate(parts, axis=1).astype(BF16)

    mix = _dot(a_ref[...], wout_ref[:D_ATTN, :]) + _dot(b, wout_ref[D_ATTN:, :])
    x2 = x_ref[...] + mix
    x2_ref[...] = x2
    h2 = _rms(x2, gffn_ref[...])
    _store_token_tiles(h2_ref, h2)

    hi = h2.astype(BF16)
    lo = (h2 - hi.astype(F32)).astype(BF16)
    logits = _dot(hi, wr_hi_ref[...]) + _dot(lo, wr_hi_ref[...]) + _dot(hi, wr_lo_ref[...]) + br_ref[...]

    lane = lax.broadcasted_iota(jnp.int32, (tm, LANES), 1)
    is_group = lane < N_GROUPS
    l1 = jnp.where(is_group, logits, -jnp.inf)
    top_g = jnp.max(l1, axis=-1, keepdims=True)
    g_idx = jnp.min(jnp.where(l1 == top_g, lane, LANES), axis=-1, keepdims=True)
    denom = jnp.sum(jnp.where(is_group, jnp.exp(logits - top_g), 0.0), axis=-1, keepdims=True)
    p_g = 1.0 / denom
    lane_group = jnp.where(lane >= N_GROUPS, (lane - N_GROUPS) // E_PER_GROUP, -1)
    l2 = jnp.where(lane_group == g_idx, logits, -jnp.inf)
    v1 = jnp.max(l2, axis=-1, keepdims=True)
    i1 = jnp.min(jnp.where(l2 == v1, lane, LANES), axis=-1, keepdims=True)
    l2 = jnp.where(lane == i1, -jnp.inf, l2)
    v2 = jnp.max(l2, axis=-1, keepdims=True)
    i2 = jnp.min(jnp.where(l2 == v2, lane, LANES), axis=-1, keepdims=True)
    e1 = i1 - N_GROUPS
    e2 = i2 - N_GROUPS
    t2 = jnp.exp(v2 - v1)
    w1 = 1.0 / (1.0 + t2)
    gate1 = p_g * w1
    gate2 = p_g * (t2 * w1)

    oh1 = jnp.where(lane == e1, 1.0, 0.0)
    oh2 = jnp.where(lane == e2, 1.0, 0.0)
    oh = oh1 + oh2
    row = lax.broadcasted_iota(jnp.int32, (tm, tm), 0)
    col = lax.broadcasted_iota(jnp.int32, (tm, tm), 1)
    before = jnp.where(col < row, 1.0, 0.0).astype(BF16)
    seen = _dot(before, oh.astype(BF16)) + carry_s[0:1, :]
    r1 = jnp.sum(oh1 * seen, axis=-1, keepdims=True)
    r2 = jnp.sum(oh2 * seen, axis=-1, keepdims=True)
    carry_s[...] = carry_s[...] + jnp.sum(oh, axis=0, keepdims=True)
    cnt_ref[...] = carry_s[...]

    meta = jnp.zeros((tm, LANES), F32)
    for k, val in enumerate((e1.astype(F32), e2.astype(F32), r1, r2, gate1, gate2)):
        meta = jnp.where(lane == k, val, meta)
    meta_ref[...] = meta
    metat_ref[...] = jnp.transpose(meta)[:SUBLANES, :].astype(jnp.int32)


def _mix_route(x2d, a2d, u2d, w_pool, pool_scale, w_out, g_ffn, wr, br, seq):
    t, d = x2d.shape
    tm = TM
    assert seq % tm == 0 and tm % HALO == 0
    hb = tm // HALO
    kern = functools.partial(_mix_kernel, tm=tm, seq=seq)
    tok = lambda w: pl.BlockSpec((tm, w), lambda i: (i, 0))
    full = lambda *shape: pl.BlockSpec(shape, lambda i: (0,) * len(shape))
    wr_hi = wr.astype(BF16)
    wr_lo = (wr - wr_hi.astype(F32)).astype(BF16)
    return pl.pallas_call(
        kern,
        grid=(t // tm,),
        in_specs=[tok(d), tok(D_ATTN), tok(D_POOL),
                  pl.BlockSpec((HALO, D_POOL), lambda i: (jnp.maximum(i * hb - 1, 0), 0)),
                  full(len(POOL_WINDOWS), POOL_GROUP_DIM, POOL_GROUP_DIM), full(1, D_POOL),
                  full(D_ATTN + D_POOL, d), full(1, d), full(d, LANES), full(d, LANES), full(1, LANES)],
        out_specs=[tok(d), pl.BlockSpec((tm * SUBLANES, LANES), lambda i: (i, 0)), tok(LANES),
                   pl.BlockSpec((SUBLANES, tm), lambda i: (0, i)),
                   full(SUBLANES, LANES)],
        out_shape=[jax.ShapeDtypeStruct((t, d), F32), jax.ShapeDtypeStruct((t * SUBLANES, LANES), F32),
                   jax.ShapeDtypeStruct((t, LANES), F32),
                   jax.ShapeDtypeStruct((SUBLANES, t), jnp.int32),
                   jax.ShapeDtypeStruct((SUBLANES, LANES), F32)],
        scratch_shapes=[pltpu.VMEM((HALO + tm, D_POOL), F32), pltpu.VMEM((SUBLANES, LANES), F32)],
        compiler_params=pltpu.CompilerParams(dimension_semantics=("arbitrary",), vmem_limit_bytes=VMEM_LIMIT),
        name="mix_route",
    )(x2d, a2d, u2d, u2d, w_pool.astype(BF16), pool_scale.reshape(1, D_POOL).astype(F32),
      w_out.astype(BF16), g_ffn.reshape(1, d).astype(F32), wr_hi, wr_lo, br)


ISSUE_UNROLL = 8


def _pos_kernel(offs_ref, meta_ref, pos_ref):
    meta = meta_ref[...]
    start = jnp.zeros(meta.shape, jnp.int32)
    for e in range(N_EXPERTS):
        start = jnp.where(meta == e, offs_ref[e], start)
    pos_ref[...] = start[0:2, :] + meta[2:4, :]


def _moe_positions(offs, metat, tm):
    t = metat.shape[1]
    pos = pl.pallas_call(
        _pos_kernel,
        in_specs=[pl.BlockSpec(memory_space=pltpu.SMEM), pl.BlockSpec(memory_space=pltpu.VMEM)],
        out_specs=pl.BlockSpec(memory_space=pltpu.VMEM),
        out_shape=jax.ShapeDtypeStruct((2, t), jnp.int32),
        name="moe_pos",
    )(offs, metat)
    return pos.reshape(2, t // tm, tm).transpose(1, 0, 2).reshape(2 * t)


def _scatter_kernel(pos_ref, h2_ref, xs_ref, sem, *, tm):
    def issue(r, carry):
        for k in range(2):
            _token_copy(h2_ref, r, xs_ref, pos_ref[k * tm + r], sem).start(priority=k)
        return carry

    lax.fori_loop(0, tm, issue, 0, unroll=ISSUE_UNROLL)

    def drain(r, carry):
        for k in range(2):
            _token_copy(h2_ref, 0, xs_ref, 0, sem).wait()
        return carry

    lax.fori_loop(0, tm, drain, 0, unroll=ISSUE_UNROLL)


def _moe_scatter(pos, h2t):
    rows = h2t.shape[0]
    tm = TM
    kern = functools.partial(_scatter_kernel, tm=tm)
    return pl.pallas_call(
        kern,
        grid=(rows // (tm * SUBLANES),),
        in_specs=[pl.BlockSpec((2 * tm,), lambda i: (i,), memory_space=pltpu.SMEM),
                  pl.BlockSpec((tm * SUBLANES, LANES), lambda i: (i, 0))],
        out_specs=pl.BlockSpec(memory_space=pl.ANY),
        out_shape=jax.ShapeDtypeStruct((2 * rows, LANES), F32),
        scratch_shapes=[pltpu.SemaphoreType.DMA(())],
        compiler_params=pltpu.CompilerParams(dimension_semantics=("arbitrary",), vmem_limit_bytes=VMEM_LIMIT),
        name="moe_scatter",
    )(pos, h2t)


def _expert_kernel(tile_ref, exp_ref, lo_ref, xs_ref, wg_ref, wu_ref, wd_ref, ys_ref,
                   wg_s, wu_s, wd_s, *, tm):
    s = pl.program_id(0)
    lo = lo_ref[s]
    new_expert = jnp.logical_or(s == 0, exp_ref[s] != exp_ref[jnp.maximum(s - 1, 0)])

    @pl.when(jnp.logical_and(lo < tm, new_expert))
    def _round_weights():
        wg_s[...] = wg_ref[...].astype(BF16)
        wu_s[...] = wu_ref[...].astype(BF16)
        wd_s[...] = wd_ref[...].astype(BF16)

    @pl.when(lo < tm)
    def _compute():
        x = _load_token_tiles(xs_ref, tm).astype(BF16)
        a = _dot(x, wg_s[...])
        u = _dot(x, wu_s[...])
        hid = (a * jax.nn.sigmoid(a) * u).astype(BF16)
        y = _dot(hid, wd_s[...])

        @pl.when(lo == 0)
        def _first():
            _store_token_tiles(ys_ref, y)

        @pl.when(lo > 0)
        def _later():
            row = lax.broadcasted_iota(jnp.int32, (tm, 1), 0)
            _store_token_tiles(ys_ref, jnp.where(row >= lo, y, _load_token_tiles(ys_ref, tm)))


def _moe_experts(step_tile, step_exp, step_lo, xs, w_gate, w_up, w_down):
    n_rows = xs.shape[0]
    tm = TM
    n_e, d, f = w_gate.shape
    n_steps = step_tile.shape[0]
    kern = functools.partial(_expert_kernel, tm=tm)
    grid_spec = pltpu.PrefetchScalarGridSpec(
        num_scalar_prefetch=3,
        grid=(n_steps,),
        in_specs=[pl.BlockSpec((tm * SUBLANES, LANES), lambda s, tl, ex, lo: (tl[s], 0)),
                  pl.BlockSpec((None, d, f), lambda s, tl, ex, lo: (ex[s], 0, 0)),
                  pl.BlockSpec((None, d, f), lambda s, tl, ex, lo: (ex[s], 0, 0)),
                  pl.BlockSpec((None, f, d), lambda s, tl, ex, lo: (ex[s], 0, 0))],
        out_specs=pl.BlockSpec((tm * SUBLANES, LANES), lambda s, tl, ex, lo: (tl[s], 0)),
        scratch_shapes=[pltpu.VMEM((d, f), BF16), pltpu.VMEM((d, f), BF16), pltpu.VMEM((f, d), BF16)],
    )
    return pl.pallas_call(
        kern,
        grid_spec=grid_spec,
        out_shape=jax.ShapeDtypeStruct((n_rows, LANES), F32),
        compiler_params=pltpu.CompilerParams(dimension_semantics=("arbitrary",), vmem_limit_bytes=VMEM_LIMIT),
        name="moe_experts",
    )(step_tile, step_exp, step_lo, xs, w_gate, w_up, w_down)


def _expert_steps(counts, n_rows, tm):
    n_e = counts.shape[0]
    n_tiles = n_rows // tm
    n_steps = n_tiles + n_e
    ends = jnp.cumsum(counts)
    starts = ends - counts
    first_tile = starts // tm
    last_tile = jnp.maximum(ends - 1, 0) // tm
    tiles_e = jnp.where(counts > 0, last_tile - first_tile + 1, 0)
    step_end = jnp.cumsum(tiles_e)
    step_start = step_end - tiles_e
    total = step_end[-1]
    s = jnp.arange(n_steps, dtype=jnp.int32)
    s_eff = jnp.minimum(s, total - 1)
    mine = ((step_start[None, :] <= s_eff[:, None]) & (s_eff[:, None] < step_end[None, :])).astype(jnp.int32)
    pick = lambda table: jnp.sum(mine * table[None, :], axis=1)
    exp = pick(jnp.arange(n_e, dtype=jnp.int32))
    tile = pick(first_tile) + (s_eff - pick(step_start))
    lo = jnp.maximum(pick(starts) - tile * tm, 0)
    lo = jnp.where(s < total, lo, tm)
    return tile.astype(jnp.int32), exp, lo.astype(jnp.int32), starts.astype(jnp.int32)


def _combine_kernel(pos_ref, pos_next_ref, x2_ref, gates_ref, p_ref, wproj_ref, wgate_ref, gple_ref, gfin_ref,
                    ys_ref, out_ref, rows_s, sem, *, tm):
    i = pl.program_id(0)
    slot = i % 2

    def gather(p_ref, sl):
        def issue(r, carry):
            for k in range(2):
                _token_copy(ys_ref, p_ref[k * tm + r], rows_s.at[sl, k], r, sem.at[sl]).start(priority=k)
            return carry

        lax.fori_loop(0, tm, issue, 0, unroll=ISSUE_UNROLL)

    @pl.when(i == 0)
    def _first_tile():
        gather(pos_ref, 0)

    @pl.when(i + 1 < pl.num_programs(0))
    def _next_tile():
        gather(pos_next_ref, 1 - slot)

    def drain(r, carry):
        for k in range(2):
            _token_copy(ys_ref, 0, rows_s.at[slot, k], 0, sem.at[slot]).wait()
        return carry

    lax.fori_loop(0, tm, drain, 0, unroll=ISSUE_UNROLL)

    gates = gates_ref[...]
    y = (gates[:, 4:5] * _load_token_tiles(rows_s.at[slot, 0], tm)
         + gates[:, 5:6] * _load_token_tiles(rows_s.at[slot, 1], tm))
    x3 = x2_ref[...] + y
    h3 = _rms(x3, gple_ref[...]).astype(BF16)
    gate = jax.nn.sigmoid(_dot(h3, wgate_ref[...]))
    pe = _dot(p_ref[...].astype(BF16), wproj_ref[...])
    x4 = x3 + pe * gate
    out_ref[...] = _rms(x4, gfin_ref[...])


def _combine_ple(pos, x2, meta, p2d, w_ple_proj, w_ple_gate, g_ple, g_final, ys):
    t, d = x2.shape
    d_ple = p2d.shape[1]
    tm = TM
    n_tiles = t // tm
    kern = functools.partial(_combine_kernel, tm=tm)
    tok = lambda w: pl.BlockSpec((tm, w), lambda i: (i, 0))
    full = lambda *shape: pl.BlockSpec(shape, lambda i: (0,) * len(shape))
    return pl.pallas_call(
        kern,
        grid=(t // tm,),
        in_specs=[pl.BlockSpec((2 * tm,), lambda i: (i,), memory_space=pltpu.SMEM),
                  pl.BlockSpec((2 * tm,), lambda i: (jnp.minimum(i + 1, n_tiles - 1),), memory_space=pltpu.SMEM),
                  tok(d), tok(LANES), tok(d_ple), full(d_ple, d), full(d, d), full(1, d), full(1, d),
                  pl.BlockSpec(memory_space=pl.ANY)],
        out_specs=tok(d),
        out_shape=jax.ShapeDtypeStruct((t, d), F32),
        scratch_shapes=[pltpu.VMEM((2, 2, tm * SUBLANES, LANES), F32), pltpu.SemaphoreType.DMA((2,))],
        compiler_params=pltpu.CompilerParams(dimension_semantics=("arbitrary",), vmem_limit_bytes=VMEM_LIMIT),
        name="combine_ple",
    )(pos, pos, x2, meta, p2d, w_ple_proj.astype(BF16), w_ple_gate.astype(BF16),
      g_ple.reshape(1, d).astype(F32), g_final.reshape(1, d).astype(F32), ys)


def _layer(x2d, p2d, seq, g_mix, w_in, w_pool, pool_scale, w_out, near_bias, g_ffn,
           w_r1, b_r1, w_r2, b_r2, w_gate, w_up, w_down, g_ple, w_ple_proj, w_ple_gate, g_final):
    t, d = x2d.shape
    batch = t // seq
    qkv, u = _in_proj(x2d, g_mix, w_in)
    a = _moba_attention(qkv.reshape(batch, seq, 3 * D_ATTN), near_bias).reshape(t, D_ATTN)

    wr2 = jnp.transpose(w_r2, (1, 0, 2)).reshape(d, N_EXPERTS)
    pad = LANES - N_GROUPS - N_EXPERTS
    wr = jnp.concatenate([w_r1, wr2, jnp.zeros((d, pad), F32)], axis=1).astype(F32)
    br = jnp.concatenate([b_r1, b_r2.reshape(N_EXPERTS), jnp.zeros((pad,), F32)]).reshape(1, LANES).astype(F32)
    assert d == SUBLANES * LANES, "token-tiled rows assume one (8,128) tile per token"
    x2, h2t, meta, metat, cnt = _mix_route(x2d, a, u, w_pool, pool_scale, w_out, g_ffn, wr, br, seq)

    counts = cnt[0, :N_EXPERTS].astype(jnp.int32)
    step_tile, step_exp, step_lo, offs = _expert_steps(counts, 2 * t, TM)
    pos = _moe_positions(offs, metat, TM)
    xs = _moe_scatter(pos, h2t)
    f = w_gate.shape[-1]
    ys = _moe_experts(step_tile, step_exp, step_lo, xs,
                      w_gate.reshape(N_EXPERTS, d, f), w_up.reshape(N_EXPERTS, d, f),
                      w_down.reshape(N_EXPERTS, f, d))
    return _combine_ple(pos, x2, meta, p2d, w_ple_proj, w_ple_gate, g_ple, g_final, ys)


def kernel(x, p, g_mix, w_in, w_pool, pool_scale, w_out, rel_bias, g_ffn, w_r1, b_r1, w_r2, b_r2,
           w_gate, w_up, w_down, g_ple, w_ple_proj, w_ple_gate, g_final):
    batch, seq, d = x.shape
    depth = p.shape[0]
    assert depth == 1, "the final norm is fused into the last stage of a single layer"
    near_bias = _bias_tables(rel_bias, seq)
    out = _layer(x.reshape(batch * seq, d), p[0].reshape(batch * seq, -1), seq,
                 g_mix[0], w_in[0], w_pool[0], pool_scale[0], w_out[0], near_bias, g_ffn[0],
                 w_r1[0], b_r1[0], w_r2[0], b_r2[0], w_gate[0], w_up[0], w_down[0],
                 g_ple[0], w_ple_proj[0], w_ple_gate[0], g_final)
    return out.reshape(batch, seq, d)
```

```python
import functools
import math

import numpy as np
import jax
import jax.numpy as jnp
from jax import lax
from jax.experimental import pallas as pl
from jax.experimental.pallas import tpu as pltpu

F32 = jnp.float32
BF16 = jnp.bfloat16

HEAD_DIM = 64
N_HEADS = 8
D_ATTN = N_HEADS * HEAD_DIM
POOL_WINDOWS = (2, 4, 8, 16)
POOL_GROUP_DIM = 128
D_POOL = POOL_GROUP_DIM * len(POOL_WINDOWS)
MOBA_BLOCK = 256
MOBA_TOPK = 3
REL_BUCKETS = 32
REL_MAX_DIST = 128
N_GROUPS = 4
E_PER_GROUP = 8
N_EXPERTS = N_GROUPS * E_PER_GROUP
RMS_EPS = 1e-6

LANES = 128
SUBLANES = 8
VMEM_LIMIT = 56 * 1024 * 1024

MASK_VALUE = -1e30
LOG2E = math.log2(math.e)
HALO = max(POOL_WINDOWS)
TM_PROJ = 512
TM = 256
TM_EXPERT = 512
PAIR = 2 * HEAD_DIM


def _rms(x, g):
    return x * lax.rsqrt(jnp.mean(x * x, axis=-1, keepdims=True) + RMS_EPS) * g


def _dot(a, b):
    return jnp.dot(a, b, preferred_element_type=F32)


def _dot_nt(a, b):
    return lax.dot_general(a, b, (((1,), (1,)), ((), ())), preferred_element_type=F32)


def _store_token_tiles(ref, x):
    n = x.shape[0]
    for c in range(x.shape[1] // LANES):
        ref[pl.ds(c, n, stride=SUBLANES), :] = x[:, c * LANES:(c + 1) * LANES]


def _load_token_tiles(ref, n):
    return jnp.concatenate([ref[pl.ds(c, n, stride=SUBLANES), :] for c in range(SUBLANES)], axis=1)


def _token_copy(src_ref, src_tok, dst_ref, dst_tok, sem):
    src = src_ref.at[pl.ds(pl.multiple_of(src_tok * SUBLANES, SUBLANES), SUBLANES)]
    dst = dst_ref.at[pl.ds(pl.multiple_of(dst_tok * SUBLANES, SUBLANES), SUBLANES)]
    return pltpu.make_async_copy(src, dst, sem)


def _rel_bucket_np(n):
    n = np.maximum(n, 0)
    max_exact = REL_BUCKETS // 2
    nf = np.maximum(n, 1).astype(np.float32)
    large = max_exact + (np.log(nf / np.float32(max_exact)) / np.float32(math.log(REL_MAX_DIST / max_exact))
                         * np.float32(REL_BUCKETS - max_exact)).astype(np.int32)
    large = np.minimum(large, REL_BUCKETS - 1)
    return np.where(n < max_exact, n, large).astype(np.int32)


def _bucket_tiles(seq):
    r = np.arange(MOBA_BLOCK)
    d_own = r[:, None] - r[None, :]
    own = np.where(d_own >= 0, _rel_bucket_np(d_own), -1).astype(np.int32)
    prev = _rel_bucket_np(d_own + MOBA_BLOCK)
    far = _rel_bucket_np(np.arange(MOBA_BLOCK + 1, max(seq, MOBA_BLOCK + 2)))
    assert np.all(far == REL_BUCKETS - 1)
    first = np.concatenate([own, np.full_like(own, -1)], axis=1)
    later = np.concatenate([prev, own], axis=1)
    return np.stack([first, later])


def _bias_kernel(rb_ref, bucket_ref, near_ref):
    h = pl.program_id(0)
    far = rb_ref[REL_BUCKETS - 1, h]
    bucket = bucket_ref[...]
    tile = jnp.where(bucket < 0, MASK_VALUE, 0.0).astype(F32)
    for b in range(REL_BUCKETS):
        tile = jnp.where(bucket == b, (rb_ref[b, h] - far) * LOG2E, tile)
    near_ref[...] = tile


def _bias_tables(rel_bias, seq):
    buckets = _bucket_tiles(seq)
    blk = MOBA_BLOCK
    return pl.pallas_call(
        _bias_kernel,
        grid=(N_HEADS, 2),
        in_specs=[pl.BlockSpec(memory_space=pltpu.SMEM),
                  pl.BlockSpec((None, blk, 2 * blk), lambda h, v: (v, 0, 0))],
        out_specs=pl.BlockSpec((None, None, blk, 2 * blk), lambda h, v: (h, v, 0, 0)),
        out_shape=jax.ShapeDtypeStruct((N_HEADS, 2, blk, 2 * blk), F32),
        name="bias_tables",
    )(rel_bias.astype(F32), jnp.asarray(buckets))


def _inproj_kernel(x_ref, g_ref, w_ref, qkv_ref, u_ref):
    hb = _rms(x_ref[...], g_ref[...]).astype(BF16)
    scale = HEAD_DIM ** -0.5 * LOG2E
    qkv_ref[:, :D_ATTN] = (_dot(hb, w_ref[:, :D_ATTN]) * scale).astype(BF16)
    qkv_ref[:, D_ATTN:2 * D_ATTN] = _dot(hb, w_ref[:, D_ATTN:2 * D_ATTN]).astype(BF16)
    qkv_ref[:, 2 * D_ATTN:] = _dot(hb, w_ref[:, 2 * D_ATTN:3 * D_ATTN]).astype(BF16)
    u_ref[...] = _dot(hb, w_ref[:, 3 * D_ATTN:])


def _in_proj(x2d, g_mix, w_in):
    t, d = x2d.shape
    n_in = w_in.shape[1]
    tm = min(TM_PROJ, t)
    return pl.pallas_call(
        _inproj_kernel,
        grid=(t // tm,),
        in_specs=[pl.BlockSpec((tm, d), lambda i: (i, 0)),
                  pl.BlockSpec((1, d), lambda i: (0, 0)),
                  pl.BlockSpec((d, n_in), lambda i: (0, 0))],
        out_specs=[pl.BlockSpec((tm, 3 * D_ATTN), lambda i: (i, 0)),
                   pl.BlockSpec((tm, D_POOL), lambda i: (i, 0))],
        out_shape=[jax.ShapeDtypeStruct((t, 3 * D_ATTN), BF16),
                   jax.ShapeDtypeStruct((t, D_POOL), F32)],
        compiler_params=pltpu.CompilerParams(dimension_semantics=("parallel",), vmem_limit_bytes=VMEM_LIMIT),
        name="in_proj",
    )(x2d, g_mix.reshape(1, d).astype(F32), w_in.astype(BF16))


FAR_BLOCKS = 2
PAIRS_PER_STEP = 2
HEADS_PER_STEP = 2 * PAIRS_PER_STEP


def _attn_kernel(q_ref, k_ref, v_ref, nearb_ref, o_ref,
                 kaug, vaug, kmean, qnear, qfar, sbuf, m_s, l_s, acc_s, *, nb):
    blk = MOBA_BLOCK
    i = pl.program_id(2)
    pair_lanes = [slice(pp * LANES, (pp + 1) * LANES) for pp in range(PAIRS_PER_STEP)]

    @pl.when(i == 0)
    def _build_keys():
        kmean[...] = jnp.zeros(kmean.shape, F32)
        lane_b = lax.broadcasted_iota(jnp.int32, (blk, LANES), 1)

        def body(j, carry):
            r0 = pl.multiple_of(j * blk, blk)
            rows = pl.ds(r0, blk)
            kmean[pl.ds(j, 1), :] = jnp.mean(k_ref[rows, :].astype(F32), axis=0, keepdims=True)
            for pp in range(PAIRS_PER_STEP):
                kaug[pp, rows, :LANES] = k_ref[rows, pair_lanes[pp]]
                kaug[pp, rows, LANES:] = jnp.where(lane_b == j, 1.0, 0.0).astype(BF16)
                vaug[pp, rows, :LANES] = v_ref[rows, pair_lanes[pp]]
                vaug[pp, rows, LANES:] = jnp.ones((blk, LANES), BF16)
            return carry

        lax.fori_loop(0, nb, body, 0)

    lane = lax.broadcasted_iota(jnp.int32, (blk, LANES), 1)
    nbp = kmean.shape[0]
    blk_id = lax.broadcasted_iota(jnp.int32, (nbp, blk), 0)

    for hh in range(HEADS_PER_STEP):
        pp, half = divmod(hh, 2)
        q = q_ref[:, pair_lanes[pp]]
        in_head = (lane < HEAD_DIM) if half == 0 else (lane >= HEAD_DIM)
        qm = jnp.where(in_head, q, jnp.zeros_like(q))
        kmean_b = kmean[:, pair_lanes[pp]].astype(BF16)
        gate = jnp.where(blk_id < i, _dot_nt(kmean_b, qm), -jnp.inf)
        chosen_t = jnp.zeros((nbp, blk), F32)
        for _ in range(MOBA_TOPK):
            top = jnp.max(gate, axis=0, keepdims=True)
            idx = jnp.min(jnp.where(gate == top, blk_id, nbp), axis=0, keepdims=True)
            hit = blk_id == idx
            chosen_t = jnp.where(hit, 1.0, chosen_t)
            gate = jnp.where(hit, -jnp.inf, gate)
        if nbp < LANES:
            chosen_t = jnp.concatenate([chosen_t, jnp.zeros((LANES - nbp, blk), F32)], axis=0)
        penalty = jnp.where(jnp.transpose(chosen_t) > 0.0, 0.0, MASK_VALUE)
        qnear[hh, :, :LANES] = qm
        qnear[hh, :, LANES:] = jnp.where(lane < i, penalty, 0.0).astype(BF16)
        qfar[hh, :, :LANES] = qm
        qfar[hh, :, LANES:] = jnp.where(lane < i - 1, penalty, MASK_VALUE).astype(BF16)

    def fold(hh, s, vals, first):
        width = s.shape[1]
        m_blk = jnp.max(s, axis=-1, keepdims=True)
        if first:
            m_new = jnp.broadcast_to(m_blk, (blk, LANES))
        else:
            m_old = m_s[hh]
            m_new = jnp.maximum(m_old, m_blk)
            alpha = jnp.exp2(m_old - m_new)
        p = jnp.exp2(s - jnp.concatenate([m_new] * (width // LANES), axis=1))
        pv = _dot(p.astype(BF16), vals)
        if first:
            acc_s[hh] = pv[:, :LANES]
            l_s[hh] = pv[:, LANES:]
        else:
            acc_s[hh] = alpha * acc_s[hh] + pv[:, :LANES]
            l_s[hh] = alpha * l_s[hh] + pv[:, LANES:]
        m_s[hh] = m_new

    near = pl.ds(pl.multiple_of(jnp.maximum(i - 1, 0) * blk, blk), 2 * blk)
    near_scores = [_dot_nt(qnear[hh], kaug[hh // 2, near, :]) + nearb_ref[hh] for hh in range(HEADS_PER_STEP)]

    chunk = FAR_BLOCKS * blk
    last_chunk = kaug.shape[1] // chunk - 1
    n_far = (i + FAR_BLOCKS - 2) // FAR_BLOCKS

    def chunk_rows(c):
        return pl.ds(pl.multiple_of(jnp.minimum(c, last_chunk) * chunk, chunk), chunk)

    def far_scores(c, slot):
        rows = chunk_rows(c)
        for hh in range(HEADS_PER_STEP):
            sbuf[slot, hh] = _dot_nt(qfar[hh], kaug[hh // 2, rows, :])

    def far_fold(c, slot):
        rows = chunk_rows(c)
        for hh in range(HEADS_PER_STEP):
            fold(hh, sbuf[slot, hh], vaug[hh // 2, rows, :], False)

    far_scores(0, 0)
    for hh in range(HEADS_PER_STEP):
        fold(hh, near_scores[hh], vaug[hh // 2, near, :], True)

    def far_body(cc, carry):
        c = 2 * cc
        far_scores(c + 1, 1)
        far_fold(c, 0)
        far_scores(c + 2, 0)
        far_fold(c + 1, 1)
        return carry

    lax.fori_loop(0, (n_far + 1) // 2, far_body, 0)

    for pp in range(PAIRS_PER_STEP):
        out_a = acc_s[2 * pp] / l_s[2 * pp]
        out_b = acc_s[2 * pp + 1] / l_s[2 * pp + 1]
        o_ref[:, pair_lanes[pp]] = jnp.where(lane < HEAD_DIM, out_a, out_b).astype(o_ref.dtype)


def _moba_attention(qkv, near_bias):
    b, s, _ = qkv.shape
    blk = MOBA_BLOCK
    nb = s // blk
    nbp = -(-nb // SUBLANES) * SUBLANES
    width = PAIRS_PER_STEP * PAIR
    n_groups = D_ATTN // width
    assert s % (FAR_BLOCKS * blk) == 0 and nbp <= LANES and D_ATTN % width == 0
    max_far = (nb - 1 + FAR_BLOCKS - 2) // FAR_BLOCKS
    assert max_far % 2 == 0 or max_far <= nb // FAR_BLOCKS - 1
    kern = functools.partial(_attn_kernel, nb=nb)
    hps = HEADS_PER_STEP
    return pl.pallas_call(
        kern,
        grid=(b, n_groups, nb),
        in_specs=[pl.BlockSpec((None, blk, width), lambda bi, g, i: (bi, i, g)),
                  pl.BlockSpec((None, s, width), lambda bi, g, i: (bi, 0, n_groups + g)),
                  pl.BlockSpec((None, s, width), lambda bi, g, i: (bi, 0, 2 * n_groups + g)),
                  pl.BlockSpec((hps, None, blk, 2 * blk), lambda bi, g, i: (g, jnp.minimum(i, 1), 0, 0))],
        out_specs=pl.BlockSpec((None, blk, width), lambda bi, g, i: (bi, i, g)),
        out_shape=jax.ShapeDtypeStruct((b, s, D_ATTN), BF16),
        scratch_shapes=[pltpu.VMEM((PAIRS_PER_STEP, s, 2 * LANES), BF16),
                        pltpu.VMEM((PAIRS_PER_STEP, s, 2 * LANES), BF16),
                        pltpu.VMEM((nbp, width), F32),
                        pltpu.VMEM((hps, blk, 2 * LANES), BF16),
                        pltpu.VMEM((hps, blk, 2 * LANES), BF16),
                        pltpu.VMEM((2, hps, blk, FAR_BLOCKS * blk), F32),
                        pltpu.VMEM((hps, blk, LANES), F32),
                        pltpu.VMEM((hps, blk, LANES), F32),
                        pltpu.VMEM((hps, blk, LANES), F32)],
        compiler_params=pltpu.CompilerParams(
            dimension_semantics=("parallel", "parallel", "arbitrary"), vmem_limit_bytes=VMEM_LIMIT),
        name="moba_attn",
    )(qkv, qkv, qkv, near_bias)


def _mix_kernel(x_ref, a_ref, u_ref, uh_ref, wpool_ref, pscale_ref, wout_ref, gffn_ref,
                wr_hi_ref, wr_lo_ref, br_ref,
                x2_ref, h2_ref, meta_ref, metat_ref, cnt_ref, ext_s, carry_s, *, tm, seq):
    t = pl.program_id(0)

    @pl.when(t == 0)
    def _init():
        carry_s[...] = jnp.zeros(carry_s.shape, F32)

    pos0 = (t * tm) % seq
    ext_s[0:HALO, :] = jnp.where(pos0 == 0, 0.0, uh_ref[...])
    ext_s[HALO:HALO + tm, :] = u_ref[...]
    pos = pos0 + lax.broadcasted_iota(jnp.int32, (tm, 1), 0)
    parts = []
    for gi, w in enumerate(POOL_WINDOWS):
        c0 = gi * POOL_GROUP_DIM
        cols = slice(c0, c0 + POOL_GROUP_DIM)
        tok = ext_s[HALO:HALO + tm, cols]
        win = tok
        for d in range(1, w):
            win = win + ext_s[HALO - d:HALO - d + tm, cols]
        cnt = jnp.minimum(pos + 1, w).astype(F32)
        pooled = win / cnt - tok
        parts.append(_dot(pooled.astype(BF16), wpool_ref[gi]) * pscale_ref[:, cols])
    b = jnp.concatenate(parts, axis=1).astype(BF16)

    mix = _dot(a_ref[...], wout_ref[:D_ATTN, :]) + _dot(b, wout_ref[D_ATTN:, :])
    x2 = x_ref[...] + mix
    x2_ref[...] = x2
    h2 = _rms(x2, gffn_ref[...])
    _store_token_tiles(h2_ref, h2)

    hi = h2.astype(BF16)
    lo = (h2 - hi.astype(F32)).astype(BF16)
    logits = _dot(hi, wr_hi_ref[...]) + _dot(lo, wr_hi_ref[...]) + _dot(hi, wr_lo_ref[...]) + br_ref[...]

    lane = lax.broadcasted_iota(jnp.int32, (tm, LANES), 1)
    is_group = lane < N_GROUPS
    l1 = jnp.where(is_group, logits, -jnp.inf)
    top_g = jnp.max(l1, axis=-1, keepdims=True)
    g_idx = jnp.min(jnp.where(l1 == top_g, lane, LANES), axis=-1, keepdims=True)
    denom = jnp.sum(jnp.where(is_group, jnp.exp(logits - top_g), 0.0), axis=-1, keepdims=True)
    p_g = 1.0 / denom
    lane_group = jnp.where(lane >= N_GROUPS, (lane - N_GROUPS) // E_PER_GROUP, -1)
    l2 = jnp.where(lane_group == g_idx, logits, -jnp.inf)
    v1 = jnp.max(l2, axis=-1, keepdims=True)
    i1 = jnp.min(jnp.where(l2 == v1, lane, LANES), axis=-1, keepdims=True)
    l2 = jnp.where(lane == i1, -jnp.inf, l2)
    v2 = jnp.max(l2, axis=-1, keepdims=True)
    i2 = jnp.min(jnp.where(l2 == v2, lane, LANES), axis=-1, keepdims=True)
    e1 = i1 - N_GROUPS
    e2 = i2 - N_GROUPS
    t2 = jnp.exp(v2 - v1)
    w1 = 1.0 / (1.0 + t2)
    gate1 = p_g * w1
    gate2 = p_g * (t2 * w1)

    oh1 = jnp.where(lane == e1, 1.0, 0.0)
    oh2 = jnp.where(lane == e2, 1.0, 0.0)
    oh = oh1 + oh2
    row = lax.broadcasted_iota(jnp.int32, (tm, tm), 0)
    col = lax.broadcasted_iota(jnp.int32, (tm, tm), 1)
    before = jnp.where(col < row, 1.0, 0.0).astype(BF16)
    seen = _dot(before, oh.astype(BF16)) + carry_s[0:1, :]
    r1 = jnp.sum(oh1 * seen, axis=-1, keepdims=True)
    r2 = jnp.sum(oh2 * seen, axis=-1, keepdims=True)
    carry_s[...] = carry_s[...] + jnp.sum(oh, axis=0, keepdims=True)
    cnt_ref[...] = carry_s[...]

    meta = jnp.zeros((tm, LANES), F32)
    for k, val in enumerate((e1.astype(F32), e2.astype(F32), r1, r2, gate1, gate2)):
        meta = jnp.where(lane == k, val, meta)
    meta_ref[...] = meta
    metat_ref[...] = jnp.transpose(meta)[:SUBLANES, :].astype(jnp.int32)


def _mix_route(x2d, a2d, u2d, w_pool, pool_scale, w_out, g_ffn, wr, br, seq):
    t, d = x2d.shape
    tm = TM
    assert seq % tm == 0 and tm % HALO == 0
    hb = tm // HALO
    kern = functools.partial(_mix_kernel, tm=tm, seq=seq)
    tok = lambda w: pl.BlockSpec((tm, w), lambda i: (i, 0))
    full = lambda *shape: pl.BlockSpec(shape, lambda i: (0,) * len(shape))
    wr_hi = wr.astype(BF16)
    wr_lo = (wr - wr_hi.astype(F32)).astype(BF16)
    return pl.pallas_call(
        kern,
        grid=(t // tm,),
        in_specs=[tok(d), tok(D_ATTN), tok(D_POOL),
                  pl.BlockSpec((HALO, D_POOL), lambda i: (jnp.maximum(i * hb - 1, 0), 0)),
                  full(len(POOL_WINDOWS), POOL_GROUP_DIM, POOL_GROUP_DIM), full(1, D_POOL),
                  full(D_ATTN + D_POOL, d), full(1, d), full(d, LANES), full(d, LANES), full(1, LANES)],
        out_specs=[tok(d), pl.BlockSpec((tm * SUBLANES, LANES), lambda i: (i, 0)), tok(LANES),
                   pl.BlockSpec((SUBLANES, tm), lambda i: (0, i)),
                   full(SUBLANES, LANES)],
        out_shape=[jax.ShapeDtypeStruct((t, d), F32), jax.ShapeDtypeStruct((t * SUBLANES, LANES), F32),
                   jax.ShapeDtypeStruct((t, LANES), F32),
                   jax.ShapeDtypeStruct((SUBLANES, t), jnp.int32),
                   jax.ShapeDtypeStruct((SUBLANES, LANES), F32)],
        scratch_shapes=[pltpu.VMEM((HALO + tm, D_POOL), F32), pltpu.VMEM((SUBLANES, LANES), F32)],
        compiler_params=pltpu.CompilerParams(dimension_semantics=("arbitrary",), vmem_limit_bytes=VMEM_LIMIT),
        name="mix_route",
    )(x2d, a2d, u2d, u2d, w_pool.astype(BF16), pool_scale.reshape(1, D_POOL).astype(F32),
      w_out.astype(BF16), g_ffn.reshape(1, d).astype(F32), wr_hi, wr_lo, br)


ISSUE_UNROLL = 8


def _pos_kernel(offs_ref, meta_ref, pos_ref):
    meta = meta_ref[...]
    start = jnp.zeros(meta.shape, jnp.int32)
    for e in range(N_EXPERTS):
        start = jnp.where(meta == e, offs_ref[e], start)
    pos_ref[...] = start[0:2, :] + meta[2:4, :]


def _moe_positions(offs, metat, tm):
    t = metat.shape[1]
    pos = pl.pallas_call(
        _pos_kernel,
        in_specs=[pl.BlockSpec(memory_space=pltpu.SMEM), pl.BlockSpec(memory_space=pltpu.VMEM)],
        out_specs=pl.BlockSpec(memory_space=pltpu.VMEM),
        out_shape=jax.ShapeDtypeStruct((2, t), jnp.int32),
        name="moe_pos",
    )(offs, metat)
    return pos.reshape(2, t // tm, tm).transpose(1, 0, 2).reshape(2 * t)


def _scatter_kernel(pos_ref, h2_ref, xs_ref, sem, *, tm):
    def issue(r, carry):
        for k in range(2):
            _token_copy(h2_ref, r, xs_ref, pos_ref[k * tm + r], sem).start(priority=k)
        return carry

    lax.fori_loop(0, tm, issue, 0, unroll=ISSUE_UNROLL)

    def drain(r, carry):
        for k in range(2):
            _token_copy(h2_ref, 0, xs_ref, 0, sem).wait()
        return carry

    lax.fori_loop(0, tm, drain, 0, unroll=ISSUE_UNROLL)


def _moe_scatter(pos, h2t):
    rows = h2t.shape[0]
    tm = TM
    kern = functools.partial(_scatter_kernel, tm=tm)
    return pl.pallas_call(
        kern,
        grid=(rows // (tm * SUBLANES),),
        in_specs=[pl.BlockSpec((2 * tm,), lambda i: (i,), memory_space=pltpu.SMEM),
                  pl.BlockSpec((tm * SUBLANES, LANES), lambda i: (i, 0))],
        out_specs=pl.BlockSpec(memory_space=pl.ANY),
        out_shape=jax.ShapeDtypeStruct((2 * rows, LANES), F32),
        scratch_shapes=[pltpu.SemaphoreType.DMA(())],
        compiler_params=pltpu.CompilerParams(dimension_semantics=("arbitrary",), vmem_limit_bytes=VMEM_LIMIT),
        name="moe_scatter",
    )(pos, h2t)


def _expert_kernel(tile_ref, exp_ref, lo_ref, xs_ref, wg_ref, wu_ref, wd_ref, ys_ref,
                   wg_s, wu_s, wd_s, *, tm):
    s = pl.program_id(0)
    lo = lo_ref[s]
    new_expert = jnp.logical_or(s == 0, exp_ref[s] != exp_ref[jnp.maximum(s - 1, 0)])

    @pl.when(jnp.logical_and(lo < tm, new_expert))
    def _round_weights():
        wg_s[...] = wg_ref[...].astype(BF16)
        wu_s[...] = wu_ref[...].astype(BF16)
        wd_s[...] = wd_ref[...].astype(BF16)

    @pl.when(lo < tm)
    def _compute():
        x = _load_token_tiles(xs_ref, tm).astype(BF16)
        a = _dot(x, wg_s[...])
        u = _dot(x, wu_s[...])
        hid = (a * jax.nn.sigmoid(a) * u).astype(BF16)
        y = _dot(hid, wd_s[...])

        @pl.when(lo == 0)
        def _first():
            _store_token_tiles(ys_ref, y)

        @pl.when(lo > 0)
        def _later():
            row = lax.broadcasted_iota(jnp.int32, (tm, 1), 0)
            _store_token_tiles(ys_ref, jnp.where(row >= lo, y, _load_token_tiles(ys_ref, tm)))


def _moe_experts(step_tile, step_exp, step_lo, xs, w_gate, w_up, w_down):
    n_rows = xs.shape[0]
    tm = TM_EXPERT
    n_e, d, f = w_gate.shape
    n_steps = step_tile.shape[0]
    kern = functools.partial(_expert_kernel, tm=tm)
    grid_spec = pltpu.PrefetchScalarGridSpec(
        num_scalar_prefetch=3,
        grid=(n_steps,),
        in_specs=[pl.BlockSpec((tm * SUBLANES, LANES), lambda s, tl, ex, lo: (tl[s], 0)),
                  pl.BlockSpec((None, d, f), lambda s, tl, ex, lo: (ex[s], 0, 0)),
                  pl.BlockSpec((None, d, f), lambda s, tl, ex, lo: (ex[s], 0, 0)),
                  pl.BlockSpec((None, f, d), lambda s, tl, ex, lo: (ex[s], 0, 0))],
        out_specs=pl.BlockSpec((tm * SUBLANES, LANES), lambda s, tl, ex, lo: (tl[s], 0)),
        scratch_shapes=[pltpu.VMEM((d, f), BF16), pltpu.VMEM((d, f), BF16), pltpu.VMEM((f, d), BF16)],
    )
    return pl.pallas_call(
        kern,
        grid_spec=grid_spec,
        out_shape=jax.ShapeDtypeStruct((n_rows, LANES), F32),
        compiler_params=pltpu.CompilerParams(dimension_semantics=("arbitrary",), vmem_limit_bytes=VMEM_LIMIT),
        name="moe_experts",
    )(step_tile, step_exp, step_lo, xs, w_gate, w_up, w_down)


def _expert_steps(counts, n_rows, tm):
    n_e = counts.shape[0]
    n_tiles = n_rows // tm
    n_steps = n_tiles + n_e
    ends = jnp.cumsum(counts)
    starts = ends - counts
    first_tile = starts // tm
    last_tile = jnp.maximum(ends - 1, 0) // tm
    tiles_e = jnp.where(counts > 0, last_tile - first_tile + 1, 0)
    step_end = jnp.cumsum(tiles_e)
    step_start = step_end - tiles_e
    total = step_end[-1]
    s = jnp.arange(n_steps, dtype=jnp.int32)
    s_eff = jnp.minimum(s, total - 1)
    mine = ((step_start[None, :] <= s_eff[:, None]) & (s_eff[:, None] < step_end[None, :])).astype(jnp.int32)
    pick = lambda table: jnp.sum(mine * table[None, :], axis=1)
    exp = pick(jnp.arange(n_e, dtype=jnp.int32))
    tile = pick(first_tile) + (s_eff - pick(step_start))
    lo = jnp.maximum(pick(starts) - tile * tm, 0)
    lo = jnp.where(s < total, lo, tm)
    return tile.astype(jnp.int32), exp, lo.astype(jnp.int32), starts.astype(jnp.int32)


def _combine_kernel(pos_ref, pos_next_ref, x2_ref, gates_ref, p_ref, wproj_ref, wgate_ref, gple_ref, gfin_ref,
                    ys_ref, out_ref, rows_s, sem, *, tm):
    i = pl.program_id(0)
    slot = i % 2

    def gather(p_ref, sl):
        def issue(r, carry):
            for k in range(2):
                _token_copy(ys_ref, p_ref[k * tm + r], rows_s.at[sl, k], r, sem.at[sl]).start(priority=k)
            return carry

        lax.fori_loop(0, tm, issue, 0, unroll=ISSUE_UNROLL)

    @pl.when(i == 0)
    def _first_tile():
        gather(pos_ref, 0)

    @pl.when(i + 1 < pl.num_programs(0))
    def _next_tile():
        gather(pos_next_ref, 1 - slot)

    def drain(r, carry):
        for k in range(2):
            _token_copy(ys_ref, 0, rows_s.at[slot, k], 0, sem.at[slot]).wait()
        return carry

    lax.fori_loop(0, tm, drain, 0, unroll=ISSUE_UNROLL)

    gates = gates_ref[...]
    y = (gates[:, 4:5] * _load_token_tiles(rows_s.at[slot, 0], tm)
         + gates[:, 5:6] * _load_token_tiles(rows_s.at[slot, 1], tm))
    x3 = x2_ref[...] + y
    h3 = _rms(x3, gple_ref[...]).astype(BF16)
    gate = jax.nn.sigmoid(_dot(h3, wgate_ref[...]))
    pe = _dot(p_ref[...].astype(BF16), wproj_ref[...])
    x4 = x3 + pe * gate
    out_ref[...] = _rms(x4, gfin_ref[...])


def _combine_ple(pos, x2, meta, p2d, w_ple_proj, w_ple_gate, g_ple, g_final, ys):
    t, d = x2.shape
    d_ple = p2d.shape[1]
    tm = TM
    n_tiles = t // tm
    kern = functools.partial(_combine_kernel, tm=tm)
    tok = lambda w: pl.BlockSpec((tm, w), lambda i: (i, 0))
    full = lambda *shape: pl.BlockSpec(shape, lambda i: (0,) * len(shape))
    return pl.pallas_call(
        kern,
        grid=(t // tm,),
        in_specs=[pl.BlockSpec((2 * tm,), lambda i: (i,), memory_space=pltpu.SMEM),
                  pl.BlockSpec((2 * tm,), lambda i: (jnp.minimum(i + 1, n_tiles - 1),), memory_space=pltpu.SMEM),
                  tok(d), tok(LANES), tok(d_ple), full(d_ple, d), full(d, d), full(1, d), full(1, d),
                  pl.BlockSpec(memory_space=pl.ANY)],
        out_specs=tok(d),
        out_shape=jax.ShapeDtypeStruct((t, d), F32),
        scratch_shapes=[pltpu.VMEM((2, 2, tm * SUBLANES, LANES), F32), pltpu.SemaphoreType.DMA((2,))],
        compiler_params=pltpu.CompilerParams(dimension_semantics=("arbitrary",), vmem_limit_bytes=VMEM_LIMIT),
        name="combine_ple",
    )(pos, pos, x2, meta, p2d, w_ple_proj.astype(BF16), w_ple_gate.astype(BF16),
      g_ple.reshape(1, d).astype(F32), g_final.reshape(1, d).astype(F32), ys)


def _layer(x2d, p2d, seq, g_mix, w_in, w_pool, pool_scale, w_out, near_bias, g_ffn,
           w_r1, b_r1, w_r2, b_r2, w_gate, w_up, w_down, g_ple, w_ple_proj, w_ple_gate, g_final):
    t, d = x2d.shape
    batch = t // seq
    qkv, u = _in_proj(x2d, g_mix, w_in)
    a = _moba_attention(qkv.reshape(batch, seq, 3 * D_ATTN), near_bias).reshape(t, D_ATTN)

    wr2 = jnp.transpose(w_r2, (1, 0, 2)).reshape(d, N_EXPERTS)
    pad = LANES - N_GROUPS - N_EXPERTS
    wr = jnp.concatenate([w_r1, wr2, jnp.zeros((d, pad), F32)], axis=1).astype(F32)
    br = jnp.concatenate([b_r1, b_r2.reshape(N_EXPERTS), jnp.zeros((pad,), F32)]).reshape(1, LANES).astype(F32)
    assert d == SUBLANES * LANES, "token-tiled rows assume one (8,128) tile per token"
    x2, h2t, meta, metat, cnt = _mix_route(x2d, a, u, w_pool, pool_scale, w_out, g_ffn, wr, br, seq)

    counts = cnt[0, :N_EXPERTS].astype(jnp.int32)
    step_tile, step_exp, step_lo, offs = _expert_steps(counts, 2 * t, TM_EXPERT)
    pos = _moe_positions(offs, metat, TM)
    xs = _moe_scatter(pos, h2t)
    f = w_gate.shape[-1]
    ys = _moe_experts(step_tile, step_exp, step_lo, xs,
                      w_gate.reshape(N_EXPERTS, d, f), w_up.reshape(N_EXPERTS, d, f),
                      w_down.reshape(N_EXPERTS, f, d))
    return _combine_ple(pos, x2, meta, p2d, w_ple_proj, w_ple_gate, g_ple, g_final, ys)


def kernel(x, p, g_mix, w_in, w_pool, pool_scale, w_out, rel_bias, g_ffn, w_r1, b_r1, w_r2, b_r2,
           w_gate, w_up, w_down, g_ple, w_ple_proj, w_ple_gate, g_final):
    batch, seq, d = x.shape
    depth = p.shape[0]
    assert depth == 1, "the final norm is fused into the last stage of a single layer"
    near_bias = _bias_tables(rel_bias, seq)
    out = _layer(x.reshape(batch * seq, d), p[0].reshape(batch * seq, -1), seq,
                 g_mix[0], w_in[0], w_pool[0], pool_scale[0], w_out[0], near_bias, g_ffn[0],
                 w_r1[0], b_r1[0], w_r2[0], b_r2[0], w_gate[0], w_up[0], w_down[0],
                 g_ple[0], w_ple_proj[0], w_ple_gate[0], g_final)
    return out.reshape(batch, seq, d)
```

```python
import functools
import math

import numpy as np
import jax
import jax.numpy as jnp
from jax import lax
from jax.experimental import pallas as pl
from jax.experimental.pallas import tpu as pltpu

F32 = jnp.float32
BF16 = jnp.bfloat16

HEAD_DIM = 64
N_HEADS = 8
D_ATTN = N_HEADS * HEAD_DIM
POOL_WINDOWS = (2, 4, 8, 16)
POOL_GROUP_DIM = 128
D_POOL = POOL_GROUP_DIM * len(POOL_WINDOWS)
MOBA_BLOCK = 256
MOBA_TOPK = 3
REL_BUCKETS = 32
REL_MAX_DIST = 128
N_GROUPS = 4
E_PER_GROUP = 8
N_EXPERTS = N_GROUPS * E_PER_GROUP
RMS_EPS = 1e-6

LANES = 128
SUBLANES = 8
VMEM_LIMIT = 56 * 1024 * 1024

MASK_VALUE = -1e30
LOG2E = math.log2(math.e)
HALO = max(POOL_WINDOWS)
TM_PROJ = 512
TM = 256
TM_EXPERT = 512
PAIR = 2 * HEAD_DIM


def _rms(x, g):
    return x * lax.rsqrt(jnp.mean(x * x, axis=-1, keepdims=True) + RMS_EPS) * g


def _dot(a, b):
    return jnp.dot(a, b, preferred_element_type=F32)


def _dot_nt(a, b):
    return lax.dot_general(a, b, (((1,), (1,)), ((), ())), preferred_element_type=F32)


def _store_token_tiles(ref, x):
    n = x.shape[0]
    for c in range(x.shape[1] // LANES):
        ref[pl.ds(c, n, stride=SUBLANES), :] = x[:, c * LANES:(c + 1) * LANES]


def _load_token_tiles(ref, n):
    return jnp.concatenate([ref[pl.ds(c, n, stride=SUBLANES), :] for c in range(SUBLANES)], axis=1)


def _token_copy(src_ref, src_tok, dst_ref, dst_tok, sem):
    src = src_ref.at[pl.ds(pl.multiple_of(src_tok * SUBLANES, SUBLANES), SUBLANES)]
    dst = dst_ref.at[pl.ds(pl.multiple_of(dst_tok * SUBLANES, SUBLANES), SUBLANES)]
    return pltpu.make_async_copy(src, dst, sem)


def _rel_bucket_np(n):
    n = np.maximum(n, 0)
    max_exact = REL_BUCKETS // 2
    nf = np.maximum(n, 1).astype(np.float32)
    large = max_exact + (np.log(nf / np.float32(max_exact)) / np.float32(math.log(REL_MAX_DIST / max_exact))
                         * np.float32(REL_BUCKETS - max_exact)).astype(np.int32)
    large = np.minimum(large, REL_BUCKETS - 1)
    return np.where(n < max_exact, n, large).astype(np.int32)


def _bucket_tiles(seq):
    r = np.arange(MOBA_BLOCK)
    d_own = r[:, None] - r[None, :]
    own = np.where(d_own >= 0, _rel_bucket_np(d_own), -1).astype(np.int32)
    prev = _rel_bucket_np(d_own + MOBA_BLOCK)
    far = _rel_bucket_np(np.arange(MOBA_BLOCK + 1, max(seq, MOBA_BLOCK + 2)))
    assert np.all(far == REL_BUCKETS - 1)
    first = np.concatenate([own, np.full_like(own, -1)], axis=1)
    later = np.concatenate([prev, own], axis=1)
    return np.stack([first, later])


def _bias_kernel(rb_ref, bucket_ref, near_ref):
    h = pl.program_id(0)
    far = rb_ref[REL_BUCKETS - 1, h]
    bucket = bucket_ref[...]
    tile = jnp.where(bucket < 0, MASK_VALUE, 0.0).astype(F32)
    for b in range(REL_BUCKETS):
        tile = jnp.where(bucket == b, (rb_ref[b, h] - far) * LOG2E, tile)
    near_ref[...] = tile


def _bias_tables(rel_bias, seq):
    buckets = _bucket_tiles(seq)
    blk = MOBA_BLOCK
    return pl.pallas_call(
        _bias_kernel,
        grid=(N_HEADS, 2),
        in_specs=[pl.BlockSpec(memory_space=pltpu.SMEM),
                  pl.BlockSpec((None, blk, 2 * blk), lambda h, v: (v, 0, 0))],
        out_specs=pl.BlockSpec((None, None, blk, 2 * blk), lambda h, v: (h, v, 0, 0)),
        out_shape=jax.ShapeDtypeStruct((N_HEADS, 2, blk, 2 * blk), F32),
        name="bias_tables",
    )(rel_bias.astype(F32), jnp.asarray(buckets))


def _inproj_kernel(x_ref, g_ref, w_ref, qkv_ref, u_ref):
    hb = _rms(x_ref[...], g_ref[...]).astype(BF16)
    scale = HEAD_DIM ** -0.5 * LOG2E
    qkv_ref[:, :D_ATTN] = (_dot(hb, w_ref[:, :D_ATTN]) * scale).astype(BF16)
    qkv_ref[:, D_ATTN:2 * D_ATTN] = _dot(hb, w_ref[:, D_ATTN:2 * D_ATTN]).astype(BF16)
    qkv_ref[:, 2 * D_ATTN:] = _dot(hb, w_ref[:, 2 * D_ATTN:3 * D_ATTN]).astype(BF16)
    u_ref[...] = _dot(hb, w_ref[:, 3 * D_ATTN:])


def _in_proj(x2d, g_mix, w_in):
    t, d = x2d.shape
    n_in = w_in.shape[1]
    tm = min(TM_PROJ, t)
    return pl.pallas_call(
        _inproj_kernel,
        grid=(t // tm,),
        in_specs=[pl.BlockSpec((tm, d), lambda i: (i, 0)),
                  pl.BlockSpec((1, d), lambda i: (0, 0)),
                  pl.BlockSpec((d, n_in), lambda i: (0, 0))],
        out_specs=[pl.BlockSpec((tm, 3 * D_ATTN), lambda i: (i, 0)),
                   pl.BlockSpec((tm, D_POOL), lambda i: (i, 0))],
        out_shape=[jax.ShapeDtypeStruct((t, 3 * D_ATTN), BF16),
                   jax.ShapeDtypeStruct((t, D_POOL), F32)],
        compiler_params=pltpu.CompilerParams(dimension_semantics=("parallel",), vmem_limit_bytes=VMEM_LIMIT),
        name="in_proj",
    )(x2d, g_mix.reshape(1, d).astype(F32), w_in.astype(BF16))


FAR_BLOCKS = 2
PAIRS_PER_STEP = 2
HEADS_PER_STEP = 2 * PAIRS_PER_STEP


def _attn_kernel(q_ref, k_ref, v_ref, nearb_ref, o_ref,
                 kaug, vaug, kmean, qnear, qfar, sbuf, m_s, l_s, acc_s, *, nb):
    blk = MOBA_BLOCK
    i = pl.program_id(2)
    pair_lanes = [slice(pp * LANES, (pp + 1) * LANES) for pp in range(PAIRS_PER_STEP)]

    @pl.when(i == 0)
    def _build_keys():
        kmean[...] = jnp.zeros(kmean.shape, F32)
        lane_b = lax.broadcasted_iota(jnp.int32, (blk, LANES), 1)

        def body(j, carry):
            r0 = pl.multiple_of(j * blk, blk)
            rows = pl.ds(r0, blk)
            kmean[pl.ds(j, 1), :] = jnp.mean(k_ref[rows, :].astype(F32), axis=0, keepdims=True)
            for pp in range(PAIRS_PER_STEP):
                kaug[pp, rows, :LANES] = k_ref[rows, pair_lanes[pp]]
                kaug[pp, rows, LANES:] = jnp.where(lane_b == j, 1.0, 0.0).astype(BF16)
                vaug[pp, rows, :LANES] = v_ref[rows, pair_lanes[pp]]
                vaug[pp, rows, LANES:] = jnp.ones((blk, LANES), BF16)
            return carry

        lax.fori_loop(0, nb, body, 0)

    lane = lax.broadcasted_iota(jnp.int32, (blk, LANES), 1)
    nbp = kmean.shape[0]
    blk_id = lax.broadcasted_iota(jnp.int32, (nbp, blk), 0)

    for hh in range(HEADS_PER_STEP):
        pp, half = divmod(hh, 2)
        q = q_ref[:, pair_lanes[pp]]
        in_head = (lane < HEAD_DIM) if half == 0 else (lane >= HEAD_DIM)
        qm = jnp.where(in_head, q, jnp.zeros_like(q))
        kmean_b = kmean[:, pair_lanes[pp]].astype(BF16)
        gate = jnp.where(blk_id < i, _dot_nt(kmean_b, qm), -jnp.inf)
        chosen_t = jnp.zeros((nbp, blk), F32)
        for _ in range(MOBA_TOPK):
            top = jnp.max(gate, axis=0, keepdims=True)
            idx = jnp.min(jnp.where(gate == top, blk_id, nbp), axis=0, keepdims=True)
            hit = blk_id == idx
            chosen_t = jnp.where(hit, 1.0, chosen_t)
            gate = jnp.where(hit, -jnp.inf, gate)
        if nbp < LANES:
            chosen_t = jnp.concatenate([chosen_t, jnp.zeros((LANES - nbp, blk), F32)], axis=0)
        penalty = jnp.where(jnp.transpose(chosen_t) > 0.0, 0.0, MASK_VALUE)
        qnear[hh, :, :LANES] = qm
        qnear[hh, :, LANES:] = jnp.where(lane < i, penalty, 0.0).astype(BF16)
        qfar[hh, :, :LANES] = qm
        qfar[hh, :, LANES:] = jnp.where(lane < i - 1, penalty, MASK_VALUE).astype(BF16)

    def fold(hh, s, vals, first):
        width = s.shape[1]
        m_blk = jnp.max(s, axis=-1, keepdims=True)
        if first:
            m_new = jnp.broadcast_to(m_blk, (blk, LANES))
        else:
            m_old = m_s[hh]
            m_new = jnp.maximum(m_old, m_blk)
            alpha = jnp.exp2(m_old - m_new)
        p = jnp.exp2(s - jnp.concatenate([m_new] * (width // LANES), axis=1))
        pv = _dot(p.astype(BF16), vals)
        if first:
            acc_s[hh] = pv[:, :LANES]
            l_s[hh] = pv[:, LANES:]
        else:
            acc_s[hh] = alpha * acc_s[hh] + pv[:, :LANES]
            l_s[hh] = alpha * l_s[hh] + pv[:, LANES:]
        m_s[hh] = m_new

    near = pl.ds(pl.multiple_of(jnp.maximum(i - 1, 0) * blk, blk), 2 * blk)
    near_scores = [_dot_nt(qnear[hh], kaug[hh // 2, near, :]) + nearb_ref[hh] for hh in range(HEADS_PER_STEP)]

    chunk = FAR_BLOCKS * blk
    last_chunk = kaug.shape[1] // chunk - 1
    n_far = (i + FAR_BLOCKS - 2) // FAR_BLOCKS

    def chunk_rows(c):
        return pl.ds(pl.multiple_of(jnp.minimum(c, last_chunk) * chunk, chunk), chunk)

    def far_scores(c, slot):
        rows = chunk_rows(c)
        for hh in range(HEADS_PER_STEP):
            sbuf[slot, hh] = _dot_nt(qfar[hh], kaug[hh // 2, rows, :])

    def far_fold(c, slot):
        rows = chunk_rows(c)
        for hh in range(HEADS_PER_STEP):
            fold(hh, sbuf[slot, hh], vaug[hh // 2, rows, :], False)

    far_scores(0, 0)
    for hh in range(HEADS_PER_STEP):
        fold(hh, near_scores[hh], vaug[hh // 2, near, :], True)

    def far_body(cc, carry):
        c = 2 * cc
        far_scores(c + 1, 1)
        far_fold(c, 0)
        far_scores(c + 2, 0)
        far_fold(c + 1, 1)
        return carry

    lax.fori_loop(0, (n_far + 1) // 2, far_body, 0)

    for pp in range(PAIRS_PER_STEP):
        out_a = acc_s[2 * pp] / l_s[2 * pp]
        out_b = acc_s[2 * pp + 1] / l_s[2 * pp + 1]
        o_ref[:, pair_lanes[pp]] = jnp.where(lane < HEAD_DIM, out_a, out_b).astype(o_ref.dtype)


def _moba_attention(qkv, near_bias):
    b, s, _ = qkv.shape
    blk = MOBA_BLOCK
    nb = s // blk
    nbp = -(-nb // SUBLANES) * SUBLANES
    width = PAIRS_PER_STEP * PAIR
    n_groups = D_ATTN // width
    assert s % (FAR_BLOCKS * blk) == 0 and nbp <= LANES and D_ATTN % width == 0
    max_far = (nb - 1 + FAR_BLOCKS - 2) // FAR_BLOCKS
    assert max_far % 2 == 0 or max_far <= nb // FAR_BLOCKS - 1
    kern = functools.partial(_attn_kernel, nb=nb)
    hps = HEADS_PER_STEP
    return pl.pallas_call(
        kern,
        grid=(b, n_groups, nb),
        in_specs=[pl.BlockSpec((None, blk, width), lambda bi, g, i: (bi, i, g)),
                  pl.BlockSpec((None, s, width), lambda bi, g, i: (bi, 0, n_groups + g)),
                  pl.BlockSpec((None, s, width), lambda bi, g, i: (bi, 0, 2 * n_groups + g)),
                  pl.BlockSpec((hps, None, blk, 2 * blk), lambda bi, g, i: (g, jnp.minimum(i, 1), 0, 0))],
        out_specs=pl.BlockSpec((None, blk, width), lambda bi, g, i: (bi, i, g)),
        out_shape=jax.ShapeDtypeStruct((b, s, D_ATTN), BF16),
        scratch_shapes=[pltpu.VMEM((PAIRS_PER_STEP, s, 2 * LANES), BF16),
                        pltpu.VMEM((PAIRS_PER_STEP, s, 2 * LANES), BF16),
                        pltpu.VMEM((nbp, width), F32),
                        pltpu.VMEM((hps, blk, 2 * LANES), BF16),
                        pltpu.VMEM((hps, blk, 2 * LANES), BF16),
                        pltpu.VMEM((2, hps, blk, FAR_BLOCKS * blk), F32),
                        pltpu.VMEM((hps, blk, LANES), F32),
                        pltpu.VMEM((hps, blk, LANES), F32),
                        pltpu.VMEM((hps, blk, LANES), F32)],
        compiler_params=pltpu.CompilerParams(
            dimension_semantics=("parallel", "parallel", "arbitrary"), vmem_limit_bytes=VMEM_LIMIT),
        name="moba_attn",
    )(qkv, qkv, qkv, near_bias)


def _mix_kernel(x_ref, a_ref, u_ref, uh_ref, wpool_ref, pscale_ref, wout_ref, gffn_ref,
                wr_hi_ref, wr_lo_ref, br_ref,
                x2_ref, h2_ref, meta_ref, metat_ref, cnt_ref, ext_s, carry_s, *, tm, seq):
    t = pl.program_id(0)

    @pl.when(t == 0)
    def _init():
        carry_s[...] = jnp.zeros(carry_s.shape, F32)

    pos0 = (t * tm) % seq
    ext_s[0:HALO, :] = jnp.where(pos0 == 0, 0.0, uh_ref[...])
    ext_s[HALO:HALO + tm, :] = u_ref[...]
    pos = pos0 + lax.broadcasted_iota(jnp.int32, (tm, 1), 0)
    parts = []
    for gi, w in enumerate(POOL_WINDOWS):
        c0 = gi * POOL_GROUP_DIM
        cols = slice(c0, c0 + POOL_GROUP_DIM)
        tok = ext_s[HALO:HALO + tm, cols]
        win = tok
        for d in range(1, w):
            win = win + ext_s[HALO - d:HALO - d + tm, cols]
        cnt = jnp.minimum(pos + 1, w).astype(F32)
        pooled = win / cnt - tok
        parts.append(_dot(pooled.astype(BF16), wpool_ref[gi]) * pscale_ref[:, cols])
    b = jnp.concatenate(parts, axis=1).astype(BF16)

    mix = _dot(a_ref[...], wout_ref[:D_ATTN, :]) + _dot(b, wout_ref[D_ATTN:, :])
    x2 = x_ref[...] + mix
    x2_ref[...] = x2
    h2 = _rms(x2, gffn_ref[...])
    _store_token_tiles(h2_ref, h2)

    hi = h2.astype(BF16)
    lo = (h2 - hi.astype(F32)).astype(BF16)
    logits = jnp.transpose(_dot(hi, wr_hi_ref[...]) + _dot(lo, wr_hi_ref[...]) + _dot(hi, wr_lo_ref[...])
                           + br_ref[...])
    row = lax.broadcasted_iota(jnp.int32, (SUBLANES, tm), 0)
    lg = jnp.where(row < N_GROUPS, logits[0:SUBLANES], -jnp.inf)
    top_g = jnp.max(lg, axis=0, keepdims=True)
    g_idx = jnp.min(jnp.where(lg == top_g, row, SUBLANES), axis=0, keepdims=True)
    p_g = 1.0 / jnp.sum(jnp.exp(lg - top_g), axis=0, keepdims=True)
    l2 = logits[SUBLANES:2 * SUBLANES]
    for g in range(1, N_GROUPS):
        l2 = jnp.where(g_idx == g, logits[(g + 1) * SUBLANES:(g + 2) * SUBLANES], l2)
    v1 = jnp.max(l2, axis=0, keepdims=True)
    i1 = jnp.min(jnp.where(l2 == v1, row, SUBLANES), axis=0, keepdims=True)
    l2 = jnp.where(row == i1, -jnp.inf, l2)
    v2 = jnp.max(l2, axis=0, keepdims=True)
    i2 = jnp.min(jnp.where(l2 == v2, row, SUBLANES), axis=0, keepdims=True)
    e1 = g_idx * E_PER_GROUP + i1
    e2 = g_idx * E_PER_GROUP + i2
    t2 = jnp.exp(v2 - v1)
    w1 = 1.0 / (1.0 + t2)
    gate1 = p_g * w1
    gate2 = p_g * (t2 * w1)

    e_row = lax.broadcasted_iota(jnp.int32, (N_EXPERTS, tm), 0)
    oh1 = jnp.where(e_row == e1, 1.0, 0.0)
    oh2 = jnp.where(e_row == e2, 1.0, 0.0)
    oh = oh1 + oh2
    src = lax.broadcasted_iota(jnp.int32, (tm, tm), 0)
    dst = lax.broadcasted_iota(jnp.int32, (tm, tm), 1)
    earlier = jnp.where(src < dst, 1.0, 0.0).astype(BF16)
    seen = _dot(oh.astype(BF16), earlier) + carry_s[:, 0:1]
    r1 = jnp.sum(oh1 * seen, axis=0, keepdims=True)
    r2 = jnp.sum(oh2 * seen, axis=0, keepdims=True)
    carry_s[...] = carry_s[...] + jnp.sum(oh, axis=1, keepdims=True)
    cnt_ref[...] = carry_s[...]

    rows = (e1.astype(F32), e2.astype(F32), r1, r2, gate1, gate2)
    meta_t = jnp.zeros((SUBLANES, tm), F32)
    for k, val in enumerate(rows):
        meta_t = jnp.where(row == k, val, meta_t)
    metat_ref[...] = meta_t.astype(jnp.int32)
    meta_ref[...] = jnp.transpose(jnp.concatenate([meta_t, jnp.zeros((LANES - SUBLANES, tm), F32)], axis=0))


def _mix_route(x2d, a2d, u2d, w_pool, pool_scale, w_out, g_ffn, wr, br, seq):
    t, d = x2d.shape
    tm = TM
    assert seq % tm == 0 and tm % HALO == 0
    hb = tm // HALO
    kern = functools.partial(_mix_kernel, tm=tm, seq=seq)
    tok = lambda w: pl.BlockSpec((tm, w), lambda i: (i, 0))
    full = lambda *shape: pl.BlockSpec(shape, lambda i: (0,) * len(shape))
    wr_hi = wr.astype(BF16)
    wr_lo = (wr - wr_hi.astype(F32)).astype(BF16)
    return pl.pallas_call(
        kern,
        grid=(t // tm,),
        in_specs=[tok(d), tok(D_ATTN), tok(D_POOL),
                  pl.BlockSpec((HALO, D_POOL), lambda i: (jnp.maximum(i * hb - 1, 0), 0)),
                  full(len(POOL_WINDOWS), POOL_GROUP_DIM, POOL_GROUP_DIM), full(1, D_POOL),
                  full(D_ATTN + D_POOL, d), full(1, d), full(d, LANES), full(d, LANES), full(1, LANES)],
        out_specs=[tok(d), pl.BlockSpec((tm * SUBLANES, LANES), lambda i: (i, 0)), tok(LANES),
                   pl.BlockSpec((SUBLANES, tm), lambda i: (0, i)),
                   full(N_EXPERTS, LANES)],
        out_shape=[jax.ShapeDtypeStruct((t, d), F32), jax.ShapeDtypeStruct((t * SUBLANES, LANES), F32),
                   jax.ShapeDtypeStruct((t, LANES), F32),
                   jax.ShapeDtypeStruct((SUBLANES, t), jnp.int32),
                   jax.ShapeDtypeStruct((N_EXPERTS, LANES), F32)],
        scratch_shapes=[pltpu.VMEM((HALO + tm, D_POOL), F32), pltpu.VMEM((N_EXPERTS, LANES), F32)],
        compiler_params=pltpu.CompilerParams(dimension_semantics=("arbitrary",), vmem_limit_bytes=VMEM_LIMIT),
        name="mix_route",
    )(x2d, a2d, u2d, u2d, w_pool.astype(BF16), pool_scale.reshape(1, D_POOL).astype(F32),
      w_out.astype(BF16), g_ffn.reshape(1, d).astype(F32), wr_hi, wr_lo, br)


ISSUE_UNROLL = 8


def _pos_kernel(offs_ref, meta_ref, pos_ref):
    meta = meta_ref[...]
    start = jnp.zeros(meta.shape, jnp.int32)
    for e in range(N_EXPERTS):
        start = jnp.where(meta == e, offs_ref[e], start)
    pos_ref[...] = start[0:2, :] + meta[2:4, :]


def _moe_positions(offs, metat, tm):
    t = metat.shape[1]
    pos = pl.pallas_call(
        _pos_kernel,
        in_specs=[pl.BlockSpec(memory_space=pltpu.SMEM), pl.BlockSpec(memory_space=pltpu.VMEM)],
        out_specs=pl.BlockSpec(memory_space=pltpu.VMEM),
        out_shape=jax.ShapeDtypeStruct((2, t), jnp.int32),
        name="moe_pos",
    )(offs, metat)
    return pos.reshape(2, t // tm, tm).transpose(1, 0, 2).reshape(2 * t)


def _scatter_kernel(pos_ref, h2_ref, xs_ref, sem, *, tm):
    def issue(r, carry):
        for k in range(2):
            _token_copy(h2_ref, r, xs_ref, pos_ref[k * tm + r], sem).start(priority=k)
        return carry

    lax.fori_loop(0, tm, issue, 0, unroll=ISSUE_UNROLL)

    def drain(r, carry):
        for k in range(2):
            _token_copy(h2_ref, 0, xs_ref, 0, sem).wait()
        return carry

    lax.fori_loop(0, tm, drain, 0, unroll=ISSUE_UNROLL)


def _moe_scatter(pos, h2t):
    rows = h2t.shape[0]
    tm = TM
    kern = functools.partial(_scatter_kernel, tm=tm)
    return pl.pallas_call(
        kern,
        grid=(rows // (tm * SUBLANES),),
        in_specs=[pl.BlockSpec((2 * tm,), lambda i: (i,), memory_space=pltpu.SMEM),
                  pl.BlockSpec((tm * SUBLANES, LANES), lambda i: (i, 0))],
        out_specs=pl.BlockSpec(memory_space=pl.ANY),
        out_shape=jax.ShapeDtypeStruct((2 * rows, LANES), F32),
        scratch_shapes=[pltpu.SemaphoreType.DMA(())],
        compiler_params=pltpu.CompilerParams(dimension_semantics=("arbitrary",), vmem_limit_bytes=VMEM_LIMIT),
        name="moe_scatter",
    )(pos, h2t)


def _expert_kernel(tile_ref, exp_ref, lo_ref, xs_ref, wg_ref, wu_ref, wd_ref, ys_ref,
                   wg_s, wu_s, wd_s, *, tm):
    s = pl.program_id(0)
    lo = lo_ref[s]
    new_expert = jnp.logical_or(s == 0, exp_ref[s] != exp_ref[jnp.maximum(s - 1, 0)])

    @pl.when(jnp.logical_and(lo < tm, new_expert))
    def _round_weights():
        wg_s[...] = wg_ref[...].astype(BF16)
        wu_s[...] = wu_ref[...].astype(BF16)
        wd_s[...] = wd_ref[...].astype(BF16)

    def expert_rows():
        x = _load_token_tiles(xs_ref, tm).astype(BF16)
        a = _dot(x, wg_s[...])
        u = _dot(x, wu_s[...])
        hid = (a * jax.nn.sigmoid(a) * u).astype(BF16)
        return _dot(hid, wd_s[...])

    @pl.when(lo == 0)
    def _first_visit():
        _store_token_tiles(ys_ref, expert_rows())

    @pl.when(jnp.logical_and(lo > 0, lo < tm))
    def _later_visit():
        row = lax.broadcasted_iota(jnp.int32, (tm, 1), 0)
        _store_token_tiles(ys_ref, jnp.where(row >= lo, expert_rows(), _load_token_tiles(ys_ref, tm)))


def _moe_experts(step_tile, step_exp, step_lo, xs, w_gate, w_up, w_down):
    n_rows = xs.shape[0]
    tm = TM_EXPERT
    n_e, d, f = w_gate.shape
    n_steps = step_tile.shape[0]
    kern = functools.partial(_expert_kernel, tm=tm)
    grid_spec = pltpu.PrefetchScalarGridSpec(
        num_scalar_prefetch=3,
        grid=(n_steps,),
        in_specs=[pl.BlockSpec((tm * SUBLANES, LANES), lambda s, tl, ex, lo: (tl[s], 0)),
                  pl.BlockSpec((None, d, f), lambda s, tl, ex, lo: (ex[s], 0, 0)),
                  pl.BlockSpec((None, d, f), lambda s, tl, ex, lo: (ex[s], 0, 0)),
                  pl.BlockSpec((None, f, d), lambda s, tl, ex, lo: (ex[s], 0, 0))],
        out_specs=pl.BlockSpec((tm * SUBLANES, LANES), lambda s, tl, ex, lo: (tl[s], 0)),
        scratch_shapes=[pltpu.VMEM((d, f), BF16), pltpu.VMEM((d, f), BF16), pltpu.VMEM((f, d), BF16)],
    )
    return pl.pallas_call(
        kern,
        grid_spec=grid_spec,
        out_shape=jax.ShapeDtypeStruct((n_rows, LANES), F32),
        compiler_params=pltpu.CompilerParams(dimension_semantics=("arbitrary",), vmem_limit_bytes=VMEM_LIMIT),
        name="moe_experts",
    )(step_tile, step_exp, step_lo, xs, w_gate, w_up, w_down)


def _expert_steps(counts, n_rows, tm):
    n_e = counts.shape[0]
    n_tiles = n_rows // tm
    n_steps = n_tiles + n_e
    ends = jnp.cumsum(counts)
    starts = ends - counts
    first_tile = starts // tm
    last_tile = jnp.maximum(ends - 1, 0) // tm
    tiles_e = jnp.where(counts > 0, last_tile - first_tile + 1, 0)
    step_end = jnp.cumsum(tiles_e)
    step_start = step_end - tiles_e
    total = step_end[-1]
    s = jnp.arange(n_steps, dtype=jnp.int32)
    s_eff = jnp.minimum(s, total - 1)
    mine = ((step_start[None, :] <= s_eff[:, None]) & (s_eff[:, None] < step_end[None, :])).astype(jnp.int32)
    pick = lambda table: jnp.sum(mine * table[None, :], axis=1)
    exp = pick(jnp.arange(n_e, dtype=jnp.int32))
    tile = pick(first_tile) + (s_eff - pick(step_start))
    lo = jnp.maximum(pick(starts) - tile * tm, 0)
    lo = jnp.where(s < total, lo, tm)
    return tile.astype(jnp.int32), exp, lo.astype(jnp.int32), starts.astype(jnp.int32)


def _combine_kernel(pos_ref, pos_next_ref, x2_ref, gates_ref, p_ref, wproj_ref, wgate_ref, gple_ref, gfin_ref,
                    ys_ref, out_ref, rows_s, sem, *, tm):
    i = pl.program_id(0)
    slot = i % 2

    def gather(p_ref, sl):
        def issue(r, carry):
            for k in range(2):
                _token_copy(ys_ref, p_ref[k * tm + r], rows_s.at[sl, k], r, sem.at[sl]).start(priority=k)
            return carry

        lax.fori_loop(0, tm, issue, 0, unroll=ISSUE_UNROLL)

    @pl.when(i == 0)
    def _first_tile():
        gather(pos_ref, 0)

    @pl.when(i + 1 < pl.num_programs(0))
    def _next_tile():
        gather(pos_next_ref, 1 - slot)

    def drain(r, carry):
        for k in range(2):
            _token_copy(ys_ref, 0, rows_s.at[slot, k], 0, sem.at[slot]).wait()
        return carry

    lax.fori_loop(0, tm, drain, 0, unroll=ISSUE_UNROLL)

    gates = gates_ref[...]
    y = (gates[:, 4:5] * _load_token_tiles(rows_s.at[slot, 0], tm)
         + gates[:, 5:6] * _load_token_tiles(rows_s.at[slot, 1], tm))
    x3 = x2_ref[...] + y
    h3 = _rms(x3, gple_ref[...]).astype(BF16)
    gate = jax.nn.sigmoid(_dot(h3, wgate_ref[...]))
    pe = _dot(p_ref[...].astype(BF16), wproj_ref[...])
    x4 = x3 + pe * gate
    out_ref[...] = _rms(x4, gfin_ref[...])


def _combine_ple(pos, x2, meta, p2d, w_ple_proj, w_ple_gate, g_ple, g_final, ys):
    t, d = x2.shape
    d_ple = p2d.shape[1]
    tm = TM
    n_tiles = t // tm
    kern = functools.partial(_combine_kernel, tm=tm)
    tok = lambda w: pl.BlockSpec((tm, w), lambda i: (i, 0))
    full = lambda *shape: pl.BlockSpec(shape, lambda i: (0,) * len(shape))
    return pl.pallas_call(
        kern,
        grid=(t // tm,),
        in_specs=[pl.BlockSpec((2 * tm,), lambda i: (i,), memory_space=pltpu.SMEM),
                  pl.BlockSpec((2 * tm,), lambda i: (jnp.minimum(i + 1, n_tiles - 1),), memory_space=pltpu.SMEM),
                  tok(d), tok(LANES), tok(d_ple), full(d_ple, d), full(d, d), full(1, d), full(1, d),
                  pl.BlockSpec(memory_space=pl.ANY)],
        out_specs=tok(d),
        out_shape=jax.ShapeDtypeStruct((t, d), F32),
        scratch_shapes=[pltpu.VMEM((2, 2, tm * SUBLANES, LANES), F32), pltpu.SemaphoreType.DMA((2,))],
        compiler_params=pltpu.CompilerParams(dimension_semantics=("arbitrary",), vmem_limit_bytes=VMEM_LIMIT),
        name="combine_ple",
    )(pos, pos, x2, meta, p2d, w_ple_proj.astype(BF16), w_ple_gate.astype(BF16),
      g_ple.reshape(1, d).astype(F32), g_final.reshape(1, d).astype(F32), ys)


def _layer(x2d, p2d, seq, g_mix, w_in, w_pool, pool_scale, w_out, near_bias, g_ffn,
           w_r1, b_r1, w_r2, b_r2, w_gate, w_up, w_down, g_ple, w_ple_proj, w_ple_gate, g_final):
    t, d = x2d.shape
    batch = t // seq
    qkv, u = _in_proj(x2d, g_mix, w_in)
    a = _moba_attention(qkv.reshape(batch, seq, 3 * D_ATTN), near_bias).reshape(t, D_ATTN)

    assert N_GROUPS <= SUBLANES and E_PER_GROUP == SUBLANES
    pad_g = SUBLANES - N_GROUPS
    pad_e = LANES - SUBLANES - N_EXPERTS
    wr = jnp.concatenate([w_r1, jnp.zeros((d, pad_g), F32),
                          jnp.transpose(w_r2, (1, 0, 2)).reshape(d, N_EXPERTS),
                          jnp.zeros((d, pad_e), F32)], axis=1).astype(F32)
    br = jnp.concatenate([b_r1, jnp.zeros((pad_g,), F32), b_r2.reshape(N_EXPERTS), jnp.zeros((pad_e,), F32)])
    br = br.astype(F32).reshape(1, LANES)
    assert d == SUBLANES * LANES, "token-tiled rows assume one (8,128) tile per token"
    x2, h2t, meta, metat, cnt = _mix_route(x2d, a, u, w_pool, pool_scale, w_out, g_ffn, wr, br, seq)

    counts = cnt[:, 0].astype(jnp.int32)
    step_tile, step_exp, step_lo, offs = _expert_steps(counts, 2 * t, TM_EXPERT)
    pos = _moe_positions(offs, metat, TM)
    xs = _moe_scatter(pos, h2t)
    f = w_gate.shape[-1]
    ys = _moe_experts(step_tile, step_exp, step_lo, xs,
                      w_gate.reshape(N_EXPERTS, d, f), w_up.reshape(N_EXPERTS, d, f),
                      w_down.reshape(N_EXPERTS, f, d))
    return _combine_ple(pos, x2, meta, p2d, w_ple_proj, w_ple_gate, g_ple, g_final, ys)


def kernel(x, p, g_mix, w_in, w_pool, pool_scale, w_out, rel_bias, g_ffn, w_r1, b_r1, w_r2, b_r2,
           w_gate, w_up, w_down, g_ple, w_ple_proj, w_ple_gate, g_final):
    batch, seq, d = x.shape
    depth = p.shape[0]
    assert depth == 1, "the final norm is fused into the last stage of a single layer"
    near_bias = _bias_tables(rel_bias, seq)
    out = _layer(x.reshape(batch * seq, d), p[0].reshape(batch * seq, -1), seq,
                 g_mix[0], w_in[0], w_pool[0], pool_scale[0], w_out[0], near_bias, g_ffn[0],
                 w_r1[0], b_r1[0], w_r2[0], b_r2[0], w_gate[0], w_up[0], w_down[0],
                 g_ple[0], w_ple_proj[0], w_ple_gate[0], g_final)
    return out.reshape(batch, seq, d)
```

```python
import functools
import math

import numpy as np
import jax
import jax.numpy as jnp
from jax import lax
from jax.experimental import pallas as pl
from jax.experimental.pallas import tpu as pltpu

F32 = jnp.float32
BF16 = jnp.bfloat16

HEAD_DIM = 64
N_HEADS = 8
D_ATTN = N_HEADS * HEAD_DIM
POOL_WINDOWS = (2, 4, 8, 16)
POOL_GROUP_DIM = 128
D_POOL = POOL_GROUP_DIM * len(POOL_WINDOWS)
MOBA_BLOCK = 256
MOBA_TOPK = 3
REL_BUCKETS = 32
REL_MAX_DIST = 128
N_GROUPS = 4
E_PER_GROUP = 8
N_EXPERTS = N_GROUPS * E_PER_GROUP
RMS_EPS = 1e-6

LANES = 128
SUBLANES = 8
VMEM_LIMIT = 56 * 1024 * 1024

MASK_VALUE = -1e30
LOG2E = math.log2(math.e)
HALO = max(POOL_WINDOWS)
TM_PROJ = 512
TM = 256
TM_EXPERT = 512
PAIR = 2 * HEAD_DIM


def _rms(x, g):
    return x * lax.rsqrt(jnp.mean(x * x, axis=-1, keepdims=True) + RMS_EPS) * g


def _dot(a, b):
    return jnp.dot(a, b, preferred_element_type=F32)


def _dot_nt(a, b):
    return lax.dot_general(a, b, (((1,), (1,)), ((), ())), preferred_element_type=F32)


def _store_token_tiles(ref, x):
    n = x.shape[0]
    for c in range(x.shape[1] // LANES):
        ref[pl.ds(c, n, stride=SUBLANES), :] = x[:, c * LANES:(c + 1) * LANES]


def _load_token_tiles(ref, n):
    return jnp.concatenate([ref[pl.ds(c, n, stride=SUBLANES), :] for c in range(SUBLANES)], axis=1)


def _token_copy(src_ref, src_tok, dst_ref, dst_tok, sem):
    src = src_ref.at[pl.ds(pl.multiple_of(src_tok * SUBLANES, SUBLANES), SUBLANES)]
    dst = dst_ref.at[pl.ds(pl.multiple_of(dst_tok * SUBLANES, SUBLANES), SUBLANES)]
    return pltpu.make_async_copy(src, dst, sem)


def _rel_bucket_np(n):
    n = np.maximum(n, 0)
    max_exact = REL_BUCKETS // 2
    nf = np.maximum(n, 1).astype(np.float32)
    large = max_exact + (np.log(nf / np.float32(max_exact)) / np.float32(math.log(REL_MAX_DIST / max_exact))
                         * np.float32(REL_BUCKETS - max_exact)).astype(np.int32)
    large = np.minimum(large, REL_BUCKETS - 1)
    return np.where(n < max_exact, n, large).astype(np.int32)


def _bucket_tiles(seq):
    r = np.arange(MOBA_BLOCK)
    d_own = r[:, None] - r[None, :]
    own = np.where(d_own >= 0, _rel_bucket_np(d_own), -1).astype(np.int32)
    prev = _rel_bucket_np(d_own + MOBA_BLOCK)
    far = _rel_bucket_np(np.arange(MOBA_BLOCK + 1, max(seq, MOBA_BLOCK + 2)))
    assert np.all(far == REL_BUCKETS - 1)
    first = np.concatenate([own, np.full_like(own, -1)], axis=1)
    later = np.concatenate([prev, own], axis=1)
    return np.stack([first.T, later.T])


def _bias_kernel(rb_ref, bucket_ref, near_ref):
    h = pl.program_id(0)
    far = rb_ref[REL_BUCKETS - 1, h]
    bucket = bucket_ref[...]
    tile = jnp.where(bucket < 0, MASK_VALUE, 0.0).astype(F32)
    for b in range(REL_BUCKETS):
        tile = jnp.where(bucket == b, (rb_ref[b, h] - far) * LOG2E, tile)
    near_ref[...] = tile


def _bias_tables(rel_bias, seq):
    buckets = _bucket_tiles(seq)
    blk = MOBA_BLOCK
    return pl.pallas_call(
        _bias_kernel,
        grid=(N_HEADS, 2),
        in_specs=[pl.BlockSpec(memory_space=pltpu.SMEM),
                  pl.BlockSpec((None, 2 * blk, blk), lambda h, v: (v, 0, 0))],
        out_specs=pl.BlockSpec((None, None, 2 * blk, blk), lambda h, v: (h, v, 0, 0)),
        out_shape=jax.ShapeDtypeStruct((N_HEADS, 2, 2 * blk, blk), F32),
        name="bias_tables",
    )(rel_bias.astype(F32), jnp.asarray(buckets))


def _inproj_kernel(x_ref, g_ref, w_ref, qkv_ref, u_ref):
    hb = _rms(x_ref[...], g_ref[...]).astype(BF16)
    scale = HEAD_DIM ** -0.5 * LOG2E
    qkv_ref[:, :D_ATTN] = (_dot(hb, w_ref[:, :D_ATTN]) * scale).astype(BF16)
    qkv_ref[:, D_ATTN:2 * D_ATTN] = _dot(hb, w_ref[:, D_ATTN:2 * D_ATTN]).astype(BF16)
    qkv_ref[:, 2 * D_ATTN:] = _dot(hb, w_ref[:, 2 * D_ATTN:3 * D_ATTN]).astype(BF16)
    u_ref[...] = _dot(hb, w_ref[:, 3 * D_ATTN:])


def _in_proj(x2d, g_mix, w_in):
    t, d = x2d.shape
    n_in = w_in.shape[1]
    tm = min(TM_PROJ, t)
    return pl.pallas_call(
        _inproj_kernel,
        grid=(t // tm,),
        in_specs=[pl.BlockSpec((tm, d), lambda i: (i, 0)),
                  pl.BlockSpec((1, d), lambda i: (0, 0)),
                  pl.BlockSpec((d, n_in), lambda i: (0, 0))],
        out_specs=[pl.BlockSpec((tm, 3 * D_ATTN), lambda i: (i, 0)),
                   pl.BlockSpec((tm, D_POOL), lambda i: (i, 0))],
        out_shape=[jax.ShapeDtypeStruct((t, 3 * D_ATTN), BF16),
                   jax.ShapeDtypeStruct((t, D_POOL), F32)],
        compiler_params=pltpu.CompilerParams(dimension_semantics=("parallel",), vmem_limit_bytes=VMEM_LIMIT),
        name="in_proj",
    )(x2d, g_mix.reshape(1, d).astype(F32), w_in.astype(BF16))


FAR_BLOCKS = 2
PAIRS_PER_STEP = 2
HEADS_PER_STEP = 2 * PAIRS_PER_STEP
ONES_ROWS = 16
VT_ROWS = HEAD_DIM + ONES_ROWS


def _attn_kernel(q_ref, k_ref, v_ref, nearb_ref, o_ref,
                 kaug, vt, kmean, qnear, qfar, sbuf, m_s, acc_s, *, nb):
    blk = MOBA_BLOCK
    i = pl.program_id(2)
    pair_lanes = [slice(pp * LANES, (pp + 1) * LANES) for pp in range(PAIRS_PER_STEP)]

    @pl.when(i == 0)
    def _build_keys():
        kmean[...] = jnp.zeros(kmean.shape, F32)
        lane_b = lax.broadcasted_iota(jnp.int32, (blk, LANES), 1)

        def body(j, carry):
            rows = pl.ds(pl.multiple_of(j * blk, blk), blk)
            kmean[pl.ds(j, 1), :] = jnp.mean(k_ref[rows, :].astype(F32), axis=0, keepdims=True)
            for pp in range(PAIRS_PER_STEP):
                kaug[pp, rows, :LANES] = k_ref[rows, pair_lanes[pp]]
                kaug[pp, rows, LANES:] = jnp.where(lane_b == j, 1.0, 0.0).astype(BF16)
                v_t = jnp.transpose(v_ref[rows, pair_lanes[pp]].astype(F32))
                for half in range(2):
                    hh = 2 * pp + half
                    vt[hh, j, :HEAD_DIM, :] = v_t[half * HEAD_DIM:(half + 1) * HEAD_DIM, :].astype(BF16)
                    vt[hh, j, HEAD_DIM:, :] = jnp.ones((ONES_ROWS, blk), BF16)
            return carry

        lax.fori_loop(0, nb, body, 0)

    lane = lax.broadcasted_iota(jnp.int32, (blk, LANES), 1)
    nbp = kmean.shape[0]
    blk_id = lax.broadcasted_iota(jnp.int32, (nbp, blk), 0)

    for hh in range(HEADS_PER_STEP):
        pp, half = divmod(hh, 2)
        q = q_ref[:, pair_lanes[pp]]
        in_head = (lane < HEAD_DIM) if half == 0 else (lane >= HEAD_DIM)
        qm = jnp.where(in_head, q, jnp.zeros_like(q))
        kmean_b = kmean[:, pair_lanes[pp]].astype(BF16)
        gate = jnp.where(blk_id < i, _dot_nt(kmean_b, qm), -jnp.inf)
        chosen_t = jnp.zeros((nbp, blk), F32)
        for _ in range(MOBA_TOPK):
            top = jnp.max(gate, axis=0, keepdims=True)
            idx = jnp.min(jnp.where(gate == top, blk_id, nbp), axis=0, keepdims=True)
            hit = blk_id == idx
            chosen_t = jnp.where(hit, 1.0, chosen_t)
            gate = jnp.where(hit, -jnp.inf, gate)
        if nbp < LANES:
            chosen_t = jnp.concatenate([chosen_t, jnp.zeros((LANES - nbp, blk), F32)], axis=0)
        penalty = jnp.where(jnp.transpose(chosen_t) > 0.0, 0.0, MASK_VALUE)
        qnear[hh, :, :LANES] = qm
        qnear[hh, :, LANES:] = jnp.where(lane < i, penalty, 0.0).astype(BF16)
        qfar[hh, :, :LANES] = qm
        qfar[hh, :, LANES:] = jnp.where(lane < i - 1, penalty, MASK_VALUE).astype(BF16)

    def values_t(hh, first_block):
        return jnp.concatenate([vt[hh, first_block + k] for k in range(FAR_BLOCKS)], axis=1)

    def fold(hh, s_t, v_t, first):
        m_blk = jnp.max(s_t, axis=0, keepdims=True)
        if first:
            m_new = m_blk
        else:
            m_old = m_s[hh][0:1]
            m_new = jnp.maximum(m_old, m_blk)
            alpha = jnp.exp2(m_old - m_new)
        p_t = jnp.exp2(s_t - m_new)
        pv = _dot(v_t, p_t.astype(BF16))
        acc_s[hh] = pv if first else alpha * acc_s[hh] + pv
        m_s[hh] = jnp.broadcast_to(m_new, m_s.shape[1:])

    near_block = jnp.maximum(i - 1, 0)
    near = pl.ds(pl.multiple_of(near_block * blk, blk), 2 * blk)
    near_scores = [_dot_nt(kaug[hh // 2, near, :], qnear[hh]) + nearb_ref[hh] for hh in range(HEADS_PER_STEP)]

    chunk = FAR_BLOCKS * blk
    last_chunk = kaug.shape[1] // chunk - 1
    n_far = (i + FAR_BLOCKS - 2) // FAR_BLOCKS

    def far_scores(c, slot):
        rows = pl.ds(pl.multiple_of(jnp.minimum(c, last_chunk) * chunk, chunk), chunk)
        for hh in range(HEADS_PER_STEP):
            sbuf[slot, hh] = _dot_nt(kaug[hh // 2, rows, :], qfar[hh])

    def far_fold(c, slot):
        first_block = jnp.minimum(c, last_chunk) * FAR_BLOCKS
        for hh in range(HEADS_PER_STEP):
            fold(hh, sbuf[slot, hh], values_t(hh, first_block), False)

    far_scores(0, 0)
    for hh in range(HEADS_PER_STEP):
        fold(hh, near_scores[hh], values_t(hh, near_block), True)

    def far_body(cc, carry):
        c = 2 * cc
        far_scores(c + 1, 1)
        far_fold(c, 0)
        far_scores(c + 2, 0)
        far_fold(c + 1, 1)
        return carry

    lax.fori_loop(0, (n_far + 1) // 2, far_body, 0)

    for pp in range(PAIRS_PER_STEP):
        outs = []
        for hh in (2 * pp, 2 * pp + 1):
            acc = acc_s[hh]
            outs.append(acc[:HEAD_DIM] / acc[HEAD_DIM:HEAD_DIM + 1])
        o_ref[:, pair_lanes[pp]] = jnp.transpose(jnp.concatenate(outs, axis=0)).astype(o_ref.dtype)


def _moba_attention(qkv, near_bias):
    b, s, _ = qkv.shape
    blk = MOBA_BLOCK
    nb = s // blk
    nbp = -(-nb // SUBLANES) * SUBLANES
    width = PAIRS_PER_STEP * PAIR
    n_groups = D_ATTN // width
    assert FAR_BLOCKS == 2, "the near chunk reuses the far chunk's two-block value layout"
    assert s % (FAR_BLOCKS * blk) == 0 and nbp <= LANES and D_ATTN % width == 0
    max_far = (nb - 1 + FAR_BLOCKS - 2) // FAR_BLOCKS
    assert max_far % 2 == 0 or max_far <= nb // FAR_BLOCKS - 1
    kern = functools.partial(_attn_kernel, nb=nb)
    hps = HEADS_PER_STEP
    return pl.pallas_call(
        kern,
        grid=(b, n_groups, nb),
        in_specs=[pl.BlockSpec((None, blk, width), lambda bi, g, i: (bi, i, g)),
                  pl.BlockSpec((None, s, width), lambda bi, g, i: (bi, 0, n_groups + g)),
                  pl.BlockSpec((None, s, width), lambda bi, g, i: (bi, 0, 2 * n_groups + g)),
                  pl.BlockSpec((hps, None, 2 * blk, blk), lambda bi, g, i: (g, jnp.minimum(i, 1), 0, 0))],
        out_specs=pl.BlockSpec((None, blk, width), lambda bi, g, i: (bi, i, g)),
        out_shape=jax.ShapeDtypeStruct((b, s, D_ATTN), BF16),
        scratch_shapes=[pltpu.VMEM((PAIRS_PER_STEP, s, 2 * LANES), BF16),
                        pltpu.VMEM((hps, nb, VT_ROWS, blk), BF16),
                        pltpu.VMEM((nbp, width), F32),
                        pltpu.VMEM((hps, blk, 2 * LANES), BF16),
                        pltpu.VMEM((hps, blk, 2 * LANES), BF16),
                        pltpu.VMEM((2, hps, FAR_BLOCKS * blk, blk), F32),
                        pltpu.VMEM((hps, SUBLANES, blk), F32),
                        pltpu.VMEM((hps, VT_ROWS, blk), F32)],
        compiler_params=pltpu.CompilerParams(
            dimension_semantics=("parallel", "parallel", "arbitrary"), vmem_limit_bytes=VMEM_LIMIT),
        name="moba_attn",
    )(qkv, qkv, qkv, near_bias)


def _mix_kernel(x_ref, a_ref, u_ref, uh_ref, wpool_ref, pscale_ref, wout_ref, gffn_ref,
                wr_hi_ref, wr_lo_ref, br_ref,
                x2_ref, h2_ref, meta_ref, metat_ref, cnt_ref, ext_s, carry_s, *, tm, seq):
    t = pl.program_id(0)

    @pl.when(t == 0)
    def _init():
        carry_s[...] = jnp.zeros(carry_s.shape, F32)

    pos0 = (t * tm) % seq
    ext_s[0:HALO, :] = jnp.where(pos0 == 0, 0.0, uh_ref[...])
    ext_s[HALO:HALO + tm, :] = u_ref[...]
    pos = pos0 + lax.broadcasted_iota(jnp.int32, (tm, 1), 0)
    parts = []
    for gi, w in enumerate(POOL_WINDOWS):
        c0 = gi * POOL_GROUP_DIM
        cols = slice(c0, c0 + POOL_GROUP_DIM)
        tok = ext_s[HALO:HALO + tm, cols]
        win = tok
        for d in range(1, w):
            win = win + ext_s[HALO - d:HALO - d + tm, cols]
        cnt = jnp.minimum(pos + 1, w).astype(F32)
        pooled = win / cnt - tok
        parts.append(_dot(pooled.astype(BF16), wpool_ref[gi]) * pscale_ref[:, cols])
    b = jnp.concatenate(parts, axis=1).astype(BF16)

    mix = _dot(a_ref[...], wout_ref[:D_ATTN, :]) + _dot(b, wout_ref[D_ATTN:, :])
    x2 = x_ref[...] + mix
    x2_ref[...] = x2
    h2 = _rms(x2, gffn_ref[...])
    _store_token_tiles(h2_ref, h2)

    hi = h2.astype(BF16)
    lo = (h2 - hi.astype(F32)).astype(BF16)
    logits = jnp.transpose(_dot(hi, wr_hi_ref[...]) + _dot(lo, wr_hi_ref[...]) + _dot(hi, wr_lo_ref[...])
                           + br_ref[...])
    row = lax.broadcasted_iota(jnp.int32, (SUBLANES, tm), 0)
    lg = jnp.where(row < N_GROUPS, logits[0:SUBLANES], -jnp.inf)
    top_g = jnp.max(lg, axis=0, keepdims=True)
    g_idx = jnp.min(jnp.where(lg == top_g, row, SUBLANES), axis=0, keepdims=True)
    p_g = 1.0 / jnp.sum(jnp.exp(lg - top_g), axis=0, keepdims=True)
    l2 = logits[SUBLANES:2 * SUBLANES]
    for g in range(1, N_GROUPS):
        l2 = jnp.where(g_idx == g, logits[(g + 1) * SUBLANES:(g + 2) * SUBLANES], l2)
    v1 = jnp.max(l2, axis=0, keepdims=True)
    i1 = jnp.min(jnp.where(l2 == v1, row, SUBLANES), axis=0, keepdims=True)
    l2 = jnp.where(row == i1, -jnp.inf, l2)
    v2 = jnp.max(l2, axis=0, keepdims=True)
    i2 = jnp.min(jnp.where(l2 == v2, row, SUBLANES), axis=0, keepdims=True)
    e1 = g_idx * E_PER_GROUP + i1
    e2 = g_idx * E_PER_GROUP + i2
    t2 = jnp.exp(v2 - v1)
    w1 = 1.0 / (1.0 + t2)
    gate1 = p_g * w1
    gate2 = p_g * (t2 * w1)

    e_row = lax.broadcasted_iota(jnp.int32, (N_EXPERTS, tm), 0)
    oh1 = jnp.where(e_row == e1, 1.0, 0.0)
    oh2 = jnp.where(e_row == e2, 1.0, 0.0)
    oh = oh1 + oh2
    src = lax.broadcasted_iota(jnp.int32, (tm, tm), 0)
    dst = lax.broadcasted_iota(jnp.int32, (tm, tm), 1)
    earlier = jnp.where(src < dst, 1.0, 0.0).astype(BF16)
    seen = _dot(oh.astype(BF16), earlier) + carry_s[:, 0:1]
    r1 = jnp.sum(oh1 * seen, axis=0, keepdims=True)
    r2 = jnp.sum(oh2 * seen, axis=0, keepdims=True)
    carry_s[...] = carry_s[...] + jnp.sum(oh, axis=1, keepdims=True)
    cnt_ref[...] = carry_s[...]

    rows = (e1.astype(F32), e2.astype(F32), r1, r2, gate1, gate2)
    meta_t = jnp.zeros((SUBLANES, tm), F32)
    for k, val in enumerate(rows):
        meta_t = jnp.where(row == k, val, meta_t)
    metat_ref[...] = meta_t.astype(jnp.int32)
    meta_ref[...] = jnp.transpose(jnp.concatenate([meta_t, jnp.zeros((LANES - SUBLANES, tm), F32)], axis=0))


def _mix_route(x2d, a2d, u2d, w_pool, pool_scale, w_out, g_ffn, wr, br, seq):
    t, d = x2d.shape
    tm = TM
    assert seq % tm == 0 and tm % HALO == 0
    hb = tm // HALO
    kern = functools.partial(_mix_kernel, tm=tm, seq=seq)
    tok = lambda w: pl.BlockSpec((tm, w), lambda i: (i, 0))
    full = lambda *shape: pl.BlockSpec(shape, lambda i: (0,) * len(shape))
    wr_hi = wr.astype(BF16)
    wr_lo = (wr - wr_hi.astype(F32)).astype(BF16)
    return pl.pallas_call(
        kern,
        grid=(t // tm,),
        in_specs=[tok(d), tok(D_ATTN), tok(D_POOL),
                  pl.BlockSpec((HALO, D_POOL), lambda i: (jnp.maximum(i * hb - 1, 0), 0)),
                  full(len(POOL_WINDOWS), POOL_GROUP_DIM, POOL_GROUP_DIM), full(1, D_POOL),
                  full(D_ATTN + D_POOL, d), full(1, d), full(d, LANES), full(d, LANES), full(1, LANES)],
        out_specs=[tok(d), pl.BlockSpec((tm * SUBLANES, LANES), lambda i: (i, 0)), tok(LANES),
                   pl.BlockSpec((SUBLANES, tm), lambda i: (0, i)),
                   full(N_EXPERTS, LANES)],
        out_shape=[jax.ShapeDtypeStruct((t, d), F32), jax.ShapeDtypeStruct((t * SUBLANES, LANES), F32),
                   jax.ShapeDtypeStruct((t, LANES), F32),
                   jax.ShapeDtypeStruct((SUBLANES, t), jnp.int32),
                   jax.ShapeDtypeStruct((N_EXPERTS, LANES), F32)],
        scratch_shapes=[pltpu.VMEM((HALO + tm, D_POOL), F32), pltpu.VMEM((N_EXPERTS, LANES), F32)],
        compiler_params=pltpu.CompilerParams(dimension_semantics=("arbitrary",), vmem_limit_bytes=VMEM_LIMIT),
        name="mix_route",
    )(x2d, a2d, u2d, u2d, w_pool.astype(BF16), pool_scale.reshape(1, D_POOL).astype(F32),
      w_out.astype(BF16), g_ffn.reshape(1, d).astype(F32), wr_hi, wr_lo, br)


ISSUE_UNROLL = 8


def _pos_kernel(offs_ref, meta_ref, pos_ref):
    meta = meta_ref[...]
    start = jnp.zeros(meta.shape, jnp.int32)
    for e in range(N_EXPERTS):
        start = jnp.where(meta == e, offs_ref[e], start)
    pos_ref[...] = start[0:2, :] + meta[2:4, :]


def _moe_positions(offs, metat, tm):
    t = metat.shape[1]
    pos = pl.pallas_call(
        _pos_kernel,
        in_specs=[pl.BlockSpec(memory_space=pltpu.SMEM), pl.BlockSpec(memory_space=pltpu.VMEM)],
        out_specs=pl.BlockSpec(memory_space=pltpu.VMEM),
        out_shape=jax.ShapeDtypeStruct((2, t), jnp.int32),
        name="moe_pos",
    )(offs, metat)
    return pos.reshape(2, t // tm, tm).transpose(1, 0, 2).reshape(2 * t)


def _scatter_kernel(pos_ref, h2_ref, xs_ref, sem, *, tm):
    def issue(r, carry):
        for k in range(2):
            _token_copy(h2_ref, r, xs_ref, pos_ref[k * tm + r], sem).start(priority=k)
        return carry

    lax.fori_loop(0, tm, issue, 0, unroll=ISSUE_UNROLL)

    def drain(r, carry):
        for k in range(2):
            _token_copy(h2_ref, 0, xs_ref, 0, sem).wait()
        return carry

    lax.fori_loop(0, tm, drain, 0, unroll=ISSUE_UNROLL)


def _moe_scatter(pos, h2t):
    rows = h2t.shape[0]
    tm = TM
    kern = functools.partial(_scatter_kernel, tm=tm)
    return pl.pallas_call(
        kern,
        grid=(rows // (tm * SUBLANES),),
        in_specs=[pl.BlockSpec((2 * tm,), lambda i: (i,), memory_space=pltpu.SMEM),
                  pl.BlockSpec((tm * SUBLANES, LANES), lambda i: (i, 0))],
        out_specs=pl.BlockSpec(memory_space=pl.ANY),
        out_shape=jax.ShapeDtypeStruct((2 * rows, LANES), F32),
        scratch_shapes=[pltpu.SemaphoreType.DMA(())],
        compiler_params=pltpu.CompilerParams(dimension_semantics=("arbitrary",), vmem_limit_bytes=VMEM_LIMIT),
        name="moe_scatter",
    )(pos, h2t)


def _expert_kernel(tile_ref, exp_ref, lo_ref, xs_ref, wg_ref, wu_ref, wd_ref, ys_ref,
                   wg_s, wu_s, wd_s, *, tm):
    s = pl.program_id(0)
    lo = lo_ref[s]
    new_expert = jnp.logical_or(s == 0, exp_ref[s] != exp_ref[jnp.maximum(s - 1, 0)])

    @pl.when(jnp.logical_and(lo < tm, new_expert))
    def _round_weights():
        wg_s[...] = wg_ref[...].astype(BF16)
        wu_s[...] = wu_ref[...].astype(BF16)
        wd_s[...] = wd_ref[...].astype(BF16)

    def expert_rows():
        x = _load_token_tiles(xs_ref, tm).astype(BF16)
        a = _dot(x, wg_s[...])
        u = _dot(x, wu_s[...])
        hid = (a * jax.nn.sigmoid(a) * u).astype(BF16)
        return _dot(hid, wd_s[...])

    @pl.when(lo == 0)
    def _first_visit():
        _store_token_tiles(ys_ref, expert_rows())

    @pl.when(jnp.logical_and(lo > 0, lo < tm))
    def _later_visit():
        row = lax.broadcasted_iota(jnp.int32, (tm, 1), 0)
        _store_token_tiles(ys_ref, jnp.where(row >= lo, expert_rows(), _load_token_tiles(ys_ref, tm)))


def _moe_experts(step_tile, step_exp, step_lo, xs, w_gate, w_up, w_down):
    n_rows = xs.shape[0]
    tm = TM_EXPERT
    n_e, d, f = w_gate.shape
    n_steps = step_tile.shape[0]
    kern = functools.partial(_expert_kernel, tm=tm)
    grid_spec = pltpu.PrefetchScalarGridSpec(
        num_scalar_prefetch=3,
        grid=(n_steps,),
        in_specs=[pl.BlockSpec((tm * SUBLANES, LANES), lambda s, tl, ex, lo: (tl[s], 0)),
                  pl.BlockSpec((None, d, f), lambda s, tl, ex, lo: (ex[s], 0, 0)),
                  pl.BlockSpec((None, d, f), lambda s, tl, ex, lo: (ex[s], 0, 0)),
                  pl.BlockSpec((None, f, d), lambda s, tl, ex, lo: (ex[s], 0, 0))],
        out_specs=pl.BlockSpec((tm * SUBLANES, LANES), lambda s, tl, ex, lo: (tl[s], 0)),
        scratch_shapes=[pltpu.VMEM((d, f), BF16), pltpu.VMEM((d, f), BF16), pltpu.VMEM((f, d), BF16)],
    )
    return pl.pallas_call(
        kern,
        grid_spec=grid_spec,
        out_shape=jax.ShapeDtypeStruct((n_rows, LANES), F32),
        compiler_params=pltpu.CompilerParams(dimension_semantics=("arbitrary",), vmem_limit_bytes=VMEM_LIMIT),
        name="moe_experts",
    )(step_tile, step_exp, step_lo, xs, w_gate, w_up, w_down)


def _expert_steps(counts, n_rows, tm):
    n_e = counts.shape[0]
    n_tiles = n_rows // tm
    n_steps = n_tiles + n_e
    ends = jnp.cumsum(counts)
    starts = ends - counts
    first_tile = starts // tm
    last_tile = jnp.maximum(ends - 1, 0) // tm
    tiles_e = jnp.where(counts > 0, last_tile - first_tile + 1, 0)
    step_end = jnp.cumsum(tiles_e)
    step_start = step_end - tiles_e
    total = step_end[-1]
    s = jnp.arange(n_steps, dtype=jnp.int32)
    s_eff = jnp.minimum(s, total - 1)
    mine = ((step_start[None, :] <= s_eff[:, None]) & (s_eff[:, None] < step_end[None, :])).astype(jnp.int32)
    pick = lambda table: jnp.sum(mine * table[None, :], axis=1)
    exp = pick(jnp.arange(n_e, dtype=jnp.int32))
    tile = pick(first_tile) + (s_eff - pick(step_start))
    lo = jnp.maximum(pick(starts) - tile * tm, 0)
    lo = jnp.where(s < total, lo, tm)
    return tile.astype(jnp.int32), exp, lo.astype(jnp.int32), starts.astype(jnp.int32)


def _combine_kernel(pos_ref, pos_next_ref, x2_ref, gates_ref, p_ref, wproj_ref, wgate_ref, gple_ref, gfin_ref,
                    ys_ref, out_ref, rows_s, sem, *, tm):
    i = pl.program_id(0)
    slot = i % 2

    def gather(p_ref, sl):
        def issue(r, carry):
            for k in range(2):
                _token_copy(ys_ref, p_ref[k * tm + r], rows_s.at[sl, k], r, sem.at[sl]).start(priority=k)
            return carry

        lax.fori_loop(0, tm, issue, 0, unroll=ISSUE_UNROLL)

    @pl.when(i == 0)
    def _first_tile():
        gather(pos_ref, 0)

    @pl.when(i + 1 < pl.num_programs(0))
    def _next_tile():
        gather(pos_next_ref, 1 - slot)

    def drain(r, carry):
        for k in range(2):
            _token_copy(ys_ref, 0, rows_s.at[slot, k], 0, sem.at[slot]).wait()
        return carry

    lax.fori_loop(0, tm, drain, 0, unroll=ISSUE_UNROLL)

    gates = gates_ref[...]
    y = (gates[:, 4:5] * _load_token_tiles(rows_s.at[slot, 0], tm)
         + gates[:, 5:6] * _load_token_tiles(rows_s.at[slot, 1], tm))
    x3 = x2_ref[...] + y
    h3 = _rms(x3, gple_ref[...]).astype(BF16)
    gate = jax.nn.sigmoid(_dot(h3, wgate_ref[...]))
    pe = _dot(p_ref[...].astype(BF16), wproj_ref[...])
    x4 = x3 + pe * gate
    out_ref[...] = _rms(x4, gfin_ref[...])


def _combine_ple(pos, x2, meta, p2d, w_ple_proj, w_ple_gate, g_ple, g_final, ys):
    t, d = x2.shape
    d_ple = p2d.shape[1]
    tm = TM
    n_tiles = t // tm
    kern = functools.partial(_combine_kernel, tm=tm)
    tok = lambda w: pl.BlockSpec((tm, w), lambda i: (i, 0))
    full = lambda *shape: pl.BlockSpec(shape, lambda i: (0,) * len(shape))
    return pl.pallas_call(
        kern,
        grid=(t // tm,),
        in_specs=[pl.BlockSpec((2 * tm,), lambda i: (i,), memory_space=pltpu.SMEM),
                  pl.BlockSpec((2 * tm,), lambda i: (jnp.minimum(i + 1, n_tiles - 1),), memory_space=pltpu.SMEM),
                  tok(d), tok(LANES), tok(d_ple), full(d_ple, d), full(d, d), full(1, d), full(1, d),
                  pl.BlockSpec(memory_space=pl.ANY)],
        out_specs=tok(d),
        out_shape=jax.ShapeDtypeStruct((t, d), F32),
        scratch_shapes=[pltpu.VMEM((2, 2, tm * SUBLANES, LANES), F32), pltpu.SemaphoreType.DMA((2,))],
        compiler_params=pltpu.CompilerParams(dimension_semantics=("arbitrary",), vmem_limit_bytes=VMEM_LIMIT),
        name="combine_ple",
    )(pos, pos, x2, meta, p2d, w_ple_proj.astype(BF16), w_ple_gate.astype(BF16),
      g_ple.reshape(1, d).astype(F32), g_final.reshape(1, d).astype(F32), ys)


def _layer(x2d, p2d, seq, g_mix, w_in, w_pool, pool_scale, w_out, near_bias, g_ffn,
           w_r1, b_r1, w_r2, b_r2, w_gate, w_up, w_down, g_ple, w_ple_proj, w_ple_gate, g_final):
    t, d = x2d.shape
    batch = t // seq
    qkv, u = _in_proj(x2d, g_mix, w_in)
    a = _moba_attention(qkv.reshape(batch, seq, 3 * D_ATTN), near_bias).reshape(t, D_ATTN)

    assert N_GROUPS <= SUBLANES and E_PER_GROUP == SUBLANES
    pad_g = SUBLANES - N_GROUPS
    pad_e = LANES - SUBLANES - N_EXPERTS
    wr = jnp.concatenate([w_r1, jnp.zeros((d, pad_g), F32),
                          jnp.transpose(w_r2, (1, 0, 2)).reshape(d, N_EXPERTS),
                          jnp.zeros((d, pad_e), F32)], axis=1).astype(F32)
    br = jnp.concatenate([b_r1, jnp.zeros((pad_g,), F32), b_r2.reshape(N_EXPERTS), jnp.zeros((pad_e,), F32)])
    br = br.astype(F32).reshape(1, LANES)
    assert d == SUBLANES * LANES, "token-tiled rows assume one (8,128) tile per token"
    x2, h2t, meta, metat, cnt = _mix_route(x2d, a, u, w_pool, pool_scale, w_out, g_ffn, wr, br, seq)

    counts = cnt[:, 0].astype(jnp.int32)
    step_tile, step_exp, step_lo, offs = _expert_steps(counts, 2 * t, TM_EXPERT)
    pos = _moe_positions(offs, metat, TM)
    xs = _moe_scatter(pos, h2t)
    f = w_gate.shape[-1]
    ys = _moe_experts(step_tile, step_exp, step_lo, xs,
                      w_gate.reshape(N_EXPERTS, d, f), w_up.reshape(N_EXPERTS, d, f),
                      w_down.reshape(N_EXPERTS, f, d))
    return _combine_ple(pos, x2, meta, p2d, w_ple_proj, w_ple_gate, g_ple, g_final, ys)


def kernel(x, p, g_mix, w_in, w_pool, pool_scale, w_out, rel_bias, g_ffn, w_r1, b_r1, w_r2, b_r2,
           w_gate, w_up, w_down, g_ple, w_ple_proj, w_ple_gate, g_final):
    batch, seq, d = x.shape
    depth = p.shape[0]
    assert depth == 1, "the final norm is fused into the last stage of a single layer"
    near_bias = _bias_tables(rel_bias, seq)
    out = _layer(x.reshape(batch * seq, d), p[0].reshape(batch * seq, -1), seq,
                 g_mix[0], w_in[0], w_pool[0], pool_scale[0], w_out[0], near_bias, g_ffn[0],
                 w_r1[0], b_r1[0], w_r2[0], b_r2[0], w_gate[0], w_up[0], w_down[0],
                 g_ple[0], w_ple_proj[0], w_ple_gate[0], g_final)
    return out.reshape(batch, seq, d)
```

```python
import functools
import math

import numpy as np
import jax
import jax.numpy as jnp
from jax import lax
from jax.experimental import pallas as pl
from jax.experimental.pallas import tpu as pltpu

F32 = jnp.float32
BF16 = jnp.bfloat16

HEAD_DIM = 64
N_HEADS = 8
D_ATTN = N_HEADS * HEAD_DIM
POOL_WINDOWS = (2, 4, 8, 16)
POOL_GROUP_DIM = 128
D_POOL = POOL_GROUP_DIM * len(POOL_WINDOWS)
MOBA_BLOCK = 256
MOBA_TOPK = 3
REL_BUCKETS = 32
REL_MAX_DIST = 128
N_GROUPS = 4
E_PER_GROUP = 8
N_EXPERTS = N_GROUPS * E_PER_GROUP
RMS_EPS = 1e-6

LANES = 128
SUBLANES = 8
VMEM_LIMIT = 56 * 1024 * 1024

MASK_VALUE = -1e30
LOG2E = math.log2(math.e)
HALO = max(POOL_WINDOWS)
TM_PROJ = 512
TM = 512
TM_EXPERT = 512
PAIR = 2 * HEAD_DIM


def _rms(x, g):
    return x * lax.rsqrt(jnp.mean(x * x, axis=-1, keepdims=True) + RMS_EPS) * g


def _dot(a, b):
    return jnp.dot(a, b, preferred_element_type=F32)


def _dot_nt(a, b):
    return lax.dot_general(a, b, (((1,), (1,)), ((), ())), preferred_element_type=F32)


def _store_token_tiles(ref, x):
    n = x.shape[0]
    for c in range(x.shape[1] // LANES):
        ref[pl.ds(c, n, stride=SUBLANES), :] = x[:, c * LANES:(c + 1) * LANES]


def _load_token_tiles(ref, n, first=0):
    return jnp.concatenate([ref[pl.ds(first * SUBLANES + c, n, stride=SUBLANES), :] for c in range(SUBLANES)],
                           axis=1)


def _token_copy(src_ref, src_tok, dst_ref, dst_tok, sem):
    src = src_ref.at[pl.ds(pl.multiple_of(src_tok * SUBLANES, SUBLANES), SUBLANES)]
    dst = dst_ref.at[pl.ds(pl.multiple_of(dst_tok * SUBLANES, SUBLANES), SUBLANES)]
    return pltpu.make_async_copy(src, dst, sem)


def _rel_bucket_np(n):
    n = np.maximum(n, 0)
    max_exact = REL_BUCKETS // 2
    nf = np.maximum(n, 1).astype(np.float32)
    large = max_exact + (np.log(nf / np.float32(max_exact)) / np.float32(math.log(REL_MAX_DIST / max_exact))
                         * np.float32(REL_BUCKETS - max_exact)).astype(np.int32)
    large = np.minimum(large, REL_BUCKETS - 1)
    return np.where(n < max_exact, n, large).astype(np.int32)


def _bucket_tiles(seq):
    r = np.arange(MOBA_BLOCK)
    d_own = r[:, None] - r[None, :]
    own = np.where(d_own >= 0, _rel_bucket_np(d_own), -1).astype(np.int32)
    prev = _rel_bucket_np(d_own + MOBA_BLOCK)
    far = _rel_bucket_np(np.arange(MOBA_BLOCK + 1, max(seq, MOBA_BLOCK + 2)))
    assert np.all(far == REL_BUCKETS - 1)
    first = np.concatenate([own, np.full_like(own, -1)], axis=1)
    later = np.concatenate([prev, own], axis=1)
    return np.stack([first.T, later.T])


def _bias_kernel(rb_ref, bucket_ref, near_ref):
    h = pl.program_id(0)
    far = rb_ref[REL_BUCKETS - 1, h]
    bucket = bucket_ref[...]
    tile = jnp.where(bucket < 0, MASK_VALUE, 0.0).astype(F32)
    for b in range(REL_BUCKETS):
        tile = jnp.where(bucket == b, (rb_ref[b, h] - far) * LOG2E, tile)
    near_ref[...] = tile


def _bias_tables(rel_bias, seq):
    buckets = _bucket_tiles(seq)
    blk = MOBA_BLOCK
    return pl.pallas_call(
        _bias_kernel,
        grid=(N_HEADS, 2),
        in_specs=[pl.BlockSpec(memory_space=pltpu.SMEM),
                  pl.BlockSpec((None, 2 * blk, blk), lambda h, v: (v, 0, 0))],
        out_specs=pl.BlockSpec((None, None, 2 * blk, blk), lambda h, v: (h, v, 0, 0)),
        out_shape=jax.ShapeDtypeStruct((N_HEADS, 2, 2 * blk, blk), F32),
        name="bias_tables",
    )(rel_bias.astype(F32), jnp.asarray(buckets))


def _inproj_kernel(x_ref, g_ref, w_ref, qkv_ref, u_ref):
    hb = _rms(x_ref[...], g_ref[...]).astype(BF16)
    scale = HEAD_DIM ** -0.5 * LOG2E
    qkv_ref[:, :D_ATTN] = (_dot(hb, w_ref[:, :D_ATTN]) * scale).astype(BF16)
    qkv_ref[:, D_ATTN:2 * D_ATTN] = _dot(hb, w_ref[:, D_ATTN:2 * D_ATTN]).astype(BF16)
    qkv_ref[:, 2 * D_ATTN:] = _dot(hb, w_ref[:, 2 * D_ATTN:3 * D_ATTN]).astype(BF16)
    u_ref[...] = _dot(hb, w_ref[:, 3 * D_ATTN:])


def _in_proj(x2d, g_mix, w_in):
    t, d = x2d.shape
    n_in = w_in.shape[1]
    tm = min(TM_PROJ, t)
    return pl.pallas_call(
        _inproj_kernel,
        grid=(t // tm,),
        in_specs=[pl.BlockSpec((tm, d), lambda i: (i, 0)),
                  pl.BlockSpec((1, d), lambda i: (0, 0)),
                  pl.BlockSpec((d, n_in), lambda i: (0, 0))],
        out_specs=[pl.BlockSpec((tm, 3 * D_ATTN), lambda i: (i, 0)),
                   pl.BlockSpec((tm, D_POOL), lambda i: (i, 0))],
        out_shape=[jax.ShapeDtypeStruct((t, 3 * D_ATTN), BF16),
                   jax.ShapeDtypeStruct((t, D_POOL), F32)],
        compiler_params=pltpu.CompilerParams(dimension_semantics=("parallel",), vmem_limit_bytes=VMEM_LIMIT),
        name="in_proj",
    )(x2d, g_mix.reshape(1, d).astype(F32), w_in.astype(BF16))


FAR_BLOCKS = 2
PAIRS_PER_STEP = 2
HEADS_PER_STEP = 2 * PAIRS_PER_STEP
ONES_ROWS = 16
VT_ROWS = HEAD_DIM + ONES_ROWS


def _attn_kernel(q_ref, k_ref, v_ref, nearb_ref, o_ref,
                 kaug, vt, kmean, qnear, qfar, sbuf, m_s, acc_s, *, nb):
    blk = MOBA_BLOCK
    i = pl.program_id(2)
    pair_lanes = [slice(pp * LANES, (pp + 1) * LANES) for pp in range(PAIRS_PER_STEP)]

    @pl.when(i == 0)
    def _build_keys():
        kmean[...] = jnp.zeros(kmean.shape, F32)
        lane_b = lax.broadcasted_iota(jnp.int32, (blk, LANES), 1)

        def body(j, carry):
            rows = pl.ds(pl.multiple_of(j * blk, blk), blk)
            kmean[pl.ds(j, 1), :] = jnp.mean(k_ref[rows, :].astype(F32), axis=0, keepdims=True)
            for pp in range(PAIRS_PER_STEP):
                kaug[pp, rows, :LANES] = k_ref[rows, pair_lanes[pp]]
                kaug[pp, rows, LANES:] = jnp.where(lane_b == j, 1.0, 0.0).astype(BF16)
                v_t = jnp.transpose(v_ref[rows, pair_lanes[pp]].astype(F32))
                for half in range(2):
                    hh = 2 * pp + half
                    vt[hh, j, :HEAD_DIM, :] = v_t[half * HEAD_DIM:(half + 1) * HEAD_DIM, :].astype(BF16)
                    vt[hh, j, HEAD_DIM:, :] = jnp.ones((ONES_ROWS, blk), BF16)
            return carry

        lax.fori_loop(0, nb, body, 0)

    lane = lax.broadcasted_iota(jnp.int32, (blk, LANES), 1)
    nbp = kmean.shape[0]
    blk_id = lax.broadcasted_iota(jnp.int32, (nbp, blk), 0)

    for hh in range(HEADS_PER_STEP):
        pp, half = divmod(hh, 2)
        q = q_ref[:, pair_lanes[pp]]
        in_head = (lane < HEAD_DIM) if half == 0 else (lane >= HEAD_DIM)
        qm = jnp.where(in_head, q, jnp.zeros_like(q))
        kmean_b = kmean[:, pair_lanes[pp]].astype(BF16)
        gate = jnp.where(blk_id < i, _dot_nt(kmean_b, qm), -jnp.inf)
        chosen_t = jnp.zeros((nbp, blk), F32)
        for _ in range(MOBA_TOPK):
            top = jnp.max(gate, axis=0, keepdims=True)
            idx = jnp.min(jnp.where(gate == top, blk_id, nbp), axis=0, keepdims=True)
            hit = blk_id == idx
            chosen_t = jnp.where(hit, 1.0, chosen_t)
            gate = jnp.where(hit, -jnp.inf, gate)
        if nbp < LANES:
            chosen_t = jnp.concatenate([chosen_t, jnp.zeros((LANES - nbp, blk), F32)], axis=0)
        penalty = jnp.where(jnp.transpose(chosen_t) > 0.0, 0.0, MASK_VALUE)
        qnear[hh, :, :LANES] = qm
        qnear[hh, :, LANES:] = jnp.where(lane < i, penalty, 0.0).astype(BF16)
        qfar[hh, :, :LANES] = qm
        qfar[hh, :, LANES:] = jnp.where(lane < i - 1, penalty, MASK_VALUE).astype(BF16)

    def values_t(hh, first_block):
        return jnp.concatenate([vt[hh, first_block + k] for k in range(FAR_BLOCKS)], axis=1)

    def fold(hh, s_t, v_t, first):
        m_blk = jnp.max(s_t, axis=0, keepdims=True)
        if first:
            m_new = m_blk
        else:
            m_old = m_s[hh][0:1]
            m_new = jnp.maximum(m_old, m_blk)
            alpha = jnp.exp2(m_old - m_new)
        p_t = jnp.exp2(s_t - m_new)
        pv = _dot(v_t, p_t.astype(BF16))
        acc_s[hh] = pv if first else alpha * acc_s[hh] + pv
        m_s[hh] = jnp.broadcast_to(m_new, m_s.shape[1:])

    near_block = jnp.maximum(i - 1, 0)
    near = pl.ds(pl.multiple_of(near_block * blk, blk), 2 * blk)
    near_scores = [_dot_nt(kaug[hh // 2, near, :], qnear[hh]) + nearb_ref[hh] for hh in range(HEADS_PER_STEP)]

    chunk = FAR_BLOCKS * blk
    last_chunk = kaug.shape[1] // chunk - 1
    n_far = (i + FAR_BLOCKS - 2) // FAR_BLOCKS

    def far_scores(c, slot):
        rows = pl.ds(pl.multiple_of(jnp.minimum(c, last_chunk) * chunk, chunk), chunk)
        for hh in range(HEADS_PER_STEP):
            sbuf[slot, hh] = _dot_nt(kaug[hh // 2, rows, :], qfar[hh])

    def far_fold(c, slot):
        first_block = jnp.minimum(c, last_chunk) * FAR_BLOCKS
        for hh in range(HEADS_PER_STEP):
            fold(hh, sbuf[slot, hh], values_t(hh, first_block), False)

    far_scores(0, 0)
    for hh in range(HEADS_PER_STEP):
        fold(hh, near_scores[hh], values_t(hh, near_block), True)

    def far_body(cc, carry):
        c = 2 * cc
        far_scores(c + 1, 1)
        far_fold(c, 0)
        far_scores(c + 2, 0)
        far_fold(c + 1, 1)
        return carry

    lax.fori_loop(0, (n_far + 1) // 2, far_body, 0)

    for pp in range(PAIRS_PER_STEP):
        outs = []
        for hh in (2 * pp, 2 * pp + 1):
            acc = acc_s[hh]
            outs.append(acc[:HEAD_DIM] / acc[HEAD_DIM:HEAD_DIM + 1])
        o_ref[:, pair_lanes[pp]] = jnp.transpose(jnp.concatenate(outs, axis=0)).astype(o_ref.dtype)


def _moba_attention(qkv, near_bias):
    b, s, _ = qkv.shape
    blk = MOBA_BLOCK
    nb = s // blk
    nbp = -(-nb // SUBLANES) * SUBLANES
    width = PAIRS_PER_STEP * PAIR
    n_groups = D_ATTN // width
    assert FAR_BLOCKS == 2, "the near chunk reuses the far chunk's two-block value layout"
    assert s % (FAR_BLOCKS * blk) == 0 and nbp <= LANES and D_ATTN % width == 0
    max_far = (nb - 1 + FAR_BLOCKS - 2) // FAR_BLOCKS
    assert max_far % 2 == 0 or max_far <= nb // FAR_BLOCKS - 1
    kern = functools.partial(_attn_kernel, nb=nb)
    hps = HEADS_PER_STEP
    return pl.pallas_call(
        kern,
        grid=(b, n_groups, nb),
        in_specs=[pl.BlockSpec((None, blk, width), lambda bi, g, i: (bi, i, g)),
                  pl.BlockSpec((None, s, width), lambda bi, g, i: (bi, 0, n_groups + g)),
                  pl.BlockSpec((None, s, width), lambda bi, g, i: (bi, 0, 2 * n_groups + g)),
                  pl.BlockSpec((hps, None, 2 * blk, blk), lambda bi, g, i: (g, jnp.minimum(i, 1), 0, 0))],
        out_specs=pl.BlockSpec((None, blk, width), lambda bi, g, i: (bi, i, g)),
        out_shape=jax.ShapeDtypeStruct((b, s, D_ATTN), BF16),
        scratch_shapes=[pltpu.VMEM((PAIRS_PER_STEP, s, 2 * LANES), BF16),
                        pltpu.VMEM((hps, nb, VT_ROWS, blk), BF16),
                        pltpu.VMEM((nbp, width), F32),
                        pltpu.VMEM((hps, blk, 2 * LANES), BF16),
                        pltpu.VMEM((hps, blk, 2 * LANES), BF16),
                        pltpu.VMEM((2, hps, FAR_BLOCKS * blk, blk), F32),
                        pltpu.VMEM((hps, SUBLANES, blk), F32),
                        pltpu.VMEM((hps, VT_ROWS, blk), F32)],
        compiler_params=pltpu.CompilerParams(
            dimension_semantics=("parallel", "parallel", "arbitrary"), vmem_limit_bytes=VMEM_LIMIT),
        name="moba_attn",
    )(qkv, qkv, qkv, near_bias)


def _mix_kernel(x_ref, a_ref, u_ref, uh_ref, wpool_ref, pscale_ref, wout_ref, gffn_ref,
                wr_hi_ref, wr_lo_ref, br_ref,
                x2_ref, h2_ref, meta_ref, metat_ref, cnt_ref, ext_s, carry_s, *, tm, seq):
    t = pl.program_id(0)

    @pl.when(t == 0)
    def _init():
        carry_s[...] = jnp.zeros(carry_s.shape, F32)

    pos0 = (t * tm) % seq
    ext_s[0:HALO, :] = jnp.where(pos0 == 0, 0.0, uh_ref[...])
    ext_s[HALO:HALO + tm, :] = u_ref[...]
    pos = pos0 + lax.broadcasted_iota(jnp.int32, (tm, 1), 0)
    parts = []
    for gi, w in enumerate(POOL_WINDOWS):
        c0 = gi * POOL_GROUP_DIM
        cols = slice(c0, c0 + POOL_GROUP_DIM)
        tok = ext_s[HALO:HALO + tm, cols]
        win = tok
        for d in range(1, w):
            win = win + ext_s[HALO - d:HALO - d + tm, cols]
        cnt = jnp.minimum(pos + 1, w).astype(F32)
        pooled = win / cnt - tok
        parts.append(_dot(pooled.astype(BF16), wpool_ref[gi]) * pscale_ref[:, cols])
    b = jnp.concatenate(parts, axis=1).astype(BF16)

    mix = _dot(a_ref[...], wout_ref[:D_ATTN, :]) + _dot(b, wout_ref[D_ATTN:, :])
    x2 = x_ref[...] + mix
    x2_ref[...] = x2
    h2 = _rms(x2, gffn_ref[...])
    _store_token_tiles(h2_ref, h2)

    hi = h2.astype(BF16)
    lo = (h2 - hi.astype(F32)).astype(BF16)
    logits = jnp.transpose(_dot(hi, wr_hi_ref[...]) + _dot(lo, wr_hi_ref[...]) + _dot(hi, wr_lo_ref[...])
                           + br_ref[...])
    row = lax.broadcasted_iota(jnp.int32, (SUBLANES, tm), 0)
    lg = jnp.where(row < N_GROUPS, logits[0:SUBLANES], -jnp.inf)
    top_g = jnp.max(lg, axis=0, keepdims=True)
    g_idx = jnp.min(jnp.where(lg == top_g, row, SUBLANES), axis=0, keepdims=True)
    p_g = 1.0 / jnp.sum(jnp.exp(lg - top_g), axis=0, keepdims=True)
    l2 = logits[SUBLANES:2 * SUBLANES]
    for g in range(1, N_GROUPS):
        l2 = jnp.where(g_idx == g, logits[(g + 1) * SUBLANES:(g + 2) * SUBLANES], l2)
    v1 = jnp.max(l2, axis=0, keepdims=True)
    i1 = jnp.min(jnp.where(l2 == v1, row, SUBLANES), axis=0, keepdims=True)
    l2 = jnp.where(row == i1, -jnp.inf, l2)
    v2 = jnp.max(l2, axis=0, keepdims=True)
    i2 = jnp.min(jnp.where(l2 == v2, row, SUBLANES), axis=0, keepdims=True)
    e1 = g_idx * E_PER_GROUP + i1
    e2 = g_idx * E_PER_GROUP + i2
    t2 = jnp.exp(v2 - v1)
    w1 = 1.0 / (1.0 + t2)
    gate1 = p_g * w1
    gate2 = p_g * (t2 * w1)

    e_row = lax.broadcasted_iota(jnp.int32, (N_EXPERTS, tm), 0)
    oh1 = jnp.where(e_row == e1, 1.0, 0.0)
    oh2 = jnp.where(e_row == e2, 1.0, 0.0)
    oh = oh1 + oh2
    src = lax.broadcasted_iota(jnp.int32, (tm, tm), 0)
    dst = lax.broadcasted_iota(jnp.int32, (tm, tm), 1)
    earlier = jnp.where(src < dst, 1.0, 0.0).astype(BF16)
    seen = _dot(oh.astype(BF16), earlier) + carry_s[:, 0:1]
    r1 = jnp.sum(oh1 * seen, axis=0, keepdims=True)
    r2 = jnp.sum(oh2 * seen, axis=0, keepdims=True)
    carry_s[...] = carry_s[...] + jnp.sum(oh, axis=1, keepdims=True)
    cnt_ref[...] = carry_s[...]

    rows = (e1.astype(F32), e2.astype(F32), r1, r2, gate1, gate2)
    meta_t = jnp.zeros((SUBLANES, tm), F32)
    for k, val in enumerate(rows):
        meta_t = jnp.where(row == k, val, meta_t)
    metat_ref[...] = meta_t.astype(jnp.int32)
    meta_ref[...] = jnp.transpose(jnp.concatenate([meta_t, jnp.zeros((LANES - SUBLANES, tm), F32)], axis=0))


def _mix_route(x2d, a2d, u2d, w_pool, pool_scale, w_out, g_ffn, wr, br, seq):
    t, d = x2d.shape
    tm = TM
    assert seq % tm == 0 and tm % HALO == 0
    hb = tm // HALO
    kern = functools.partial(_mix_kernel, tm=tm, seq=seq)
    tok = lambda w: pl.BlockSpec((tm, w), lambda i: (i, 0))
    full = lambda *shape: pl.BlockSpec(shape, lambda i: (0,) * len(shape))
    wr_hi = wr.astype(BF16)
    wr_lo = (wr - wr_hi.astype(F32)).astype(BF16)
    return pl.pallas_call(
        kern,
        grid=(t // tm,),
        in_specs=[tok(d), tok(D_ATTN), tok(D_POOL),
                  pl.BlockSpec((HALO, D_POOL), lambda i: (jnp.maximum(i * hb - 1, 0), 0)),
                  full(len(POOL_WINDOWS), POOL_GROUP_DIM, POOL_GROUP_DIM), full(1, D_POOL),
                  full(D_ATTN + D_POOL, d), full(1, d), full(d, LANES), full(d, LANES), full(1, LANES)],
        out_specs=[tok(d), pl.BlockSpec((tm * SUBLANES, LANES), lambda i: (i, 0)), tok(LANES),
                   pl.BlockSpec((SUBLANES, tm), lambda i: (0, i)),
                   full(N_EXPERTS, LANES)],
        out_shape=[jax.ShapeDtypeStruct((t, d), F32), jax.ShapeDtypeStruct((t * SUBLANES, LANES), F32),
                   jax.ShapeDtypeStruct((t, LANES), F32),
                   jax.ShapeDtypeStruct((SUBLANES, t), jnp.int32),
                   jax.ShapeDtypeStruct((N_EXPERTS, LANES), F32)],
        scratch_shapes=[pltpu.VMEM((HALO + tm, D_POOL), F32), pltpu.VMEM((N_EXPERTS, LANES), F32)],
        compiler_params=pltpu.CompilerParams(dimension_semantics=("arbitrary",), vmem_limit_bytes=VMEM_LIMIT),
        name="mix_route",
    )(x2d, a2d, u2d, u2d, w_pool.astype(BF16), pool_scale.reshape(1, D_POOL).astype(F32),
      w_out.astype(BF16), g_ffn.reshape(1, d).astype(F32), wr_hi, wr_lo, br)


ISSUE_UNROLL = 8
COMBINE_CHUNKS = 2


def _pos_kernel(offs_ref, meta_ref, pos_ref):
    meta = meta_ref[...]
    start = jnp.zeros(meta.shape, jnp.int32)
    for e in range(N_EXPERTS):
        start = jnp.where(meta == e, offs_ref[e], start)
    pos_ref[...] = start[0:2, :] + meta[2:4, :]


def _moe_positions(offs, metat, tm):
    t = metat.shape[1]
    pos = pl.pallas_call(
        _pos_kernel,
        in_specs=[pl.BlockSpec(memory_space=pltpu.SMEM), pl.BlockSpec(memory_space=pltpu.VMEM)],
        out_specs=pl.BlockSpec(memory_space=pltpu.VMEM),
        out_shape=jax.ShapeDtypeStruct((2, t), jnp.int32),
        name="moe_pos",
    )(offs, metat)
    return pos.reshape(2, t // tm, tm).transpose(1, 0, 2).reshape(2 * t)


def _scatter_kernel(pos_ref, h2_ref, xs_ref, sem, *, tm):
    def issue(r, carry):
        for k in range(2):
            _token_copy(h2_ref, r, xs_ref, pos_ref[k * tm + r], sem).start(priority=k)
        return carry

    lax.fori_loop(0, tm, issue, 0, unroll=ISSUE_UNROLL)

    def drain(r, carry):
        for k in range(2):
            _token_copy(h2_ref, 0, xs_ref, 0, sem).wait()
        return carry

    lax.fori_loop(0, tm, drain, 0, unroll=ISSUE_UNROLL)


def _moe_scatter(pos, h2t):
    rows = h2t.shape[0]
    tm = TM
    kern = functools.partial(_scatter_kernel, tm=tm)
    return pl.pallas_call(
        kern,
        grid=(rows // (tm * SUBLANES),),
        in_specs=[pl.BlockSpec((2 * tm,), lambda i: (i,), memory_space=pltpu.SMEM),
                  pl.BlockSpec((tm * SUBLANES, LANES), lambda i: (i, 0))],
        out_specs=pl.BlockSpec(memory_space=pl.ANY),
        out_shape=jax.ShapeDtypeStruct((2 * rows, LANES), F32),
        scratch_shapes=[pltpu.SemaphoreType.DMA(())],
        compiler_params=pltpu.CompilerParams(dimension_semantics=("arbitrary",), vmem_limit_bytes=VMEM_LIMIT),
        name="moe_scatter",
    )(pos, h2t)


def _expert_kernel(tile_ref, exp_ref, lo_ref, xs_ref, wg_ref, wu_ref, wd_ref, ys_ref,
                   wg_s, wu_s, wd_s, *, tm):
    s = pl.program_id(0)
    lo = lo_ref[s]
    new_expert = jnp.logical_or(s == 0, exp_ref[s] != exp_ref[jnp.maximum(s - 1, 0)])

    @pl.when(jnp.logical_and(lo < tm, new_expert))
    def _round_weights():
        wg_s[...] = wg_ref[...].astype(BF16)
        wu_s[...] = wu_ref[...].astype(BF16)
        wd_s[...] = wd_ref[...].astype(BF16)

    def expert_rows():
        x = _load_token_tiles(xs_ref, tm).astype(BF16)
        a = _dot(x, wg_s[...])
        u = _dot(x, wu_s[...])
        hid = (a * jax.nn.sigmoid(a) * u).astype(BF16)
        return _dot(hid, wd_s[...])

    @pl.when(lo == 0)
    def _first_visit():
        _store_token_tiles(ys_ref, expert_rows())

    @pl.when(jnp.logical_and(lo > 0, lo < tm))
    def _later_visit():
        row = lax.broadcasted_iota(jnp.int32, (tm, 1), 0)
        _store_token_tiles(ys_ref, jnp.where(row >= lo, expert_rows(), _load_token_tiles(ys_ref, tm)))


def _moe_experts(step_tile, step_exp, step_lo, xs, w_gate, w_up, w_down):
    n_rows = xs.shape[0]
    tm = TM_EXPERT
    n_e, d, f = w_gate.shape
    n_steps = step_tile.shape[0]
    kern = functools.partial(_expert_kernel, tm=tm)
    grid_spec = pltpu.PrefetchScalarGridSpec(
        num_scalar_prefetch=3,
        grid=(n_steps,),
        in_specs=[pl.BlockSpec((tm * SUBLANES, LANES), lambda s, tl, ex, lo: (tl[s], 0)),
                  pl.BlockSpec((None, d, f), lambda s, tl, ex, lo: (ex[s], 0, 0)),
                  pl.BlockSpec((None, d, f), lambda s, tl, ex, lo: (ex[s], 0, 0)),
                  pl.BlockSpec((None, f, d), lambda s, tl, ex, lo: (ex[s], 0, 0))],
        out_specs=pl.BlockSpec((tm * SUBLANES, LANES), lambda s, tl, ex, lo: (tl[s], 0)),
        scratch_shapes=[pltpu.VMEM((d, f), BF16), pltpu.VMEM((d, f), BF16), pltpu.VMEM((f, d), BF16)],
    )
    return pl.pallas_call(
        kern,
        grid_spec=grid_spec,
        out_shape=jax.ShapeDtypeStruct((n_rows, LANES), F32),
        compiler_params=pltpu.CompilerParams(dimension_semantics=("arbitrary",), vmem_limit_bytes=VMEM_LIMIT),
        name="moe_experts",
    )(step_tile, step_exp, step_lo, xs, w_gate, w_up, w_down)


def _expert_steps(counts, n_rows, tm):
    n_e = counts.shape[0]
    n_tiles = n_rows // tm
    n_steps = n_tiles + n_e
    ends = jnp.cumsum(counts)
    starts = ends - counts
    first_tile = starts // tm
    last_tile = jnp.maximum(ends - 1, 0) // tm
    tiles_e = jnp.where(counts > 0, last_tile - first_tile + 1, 0)
    step_end = jnp.cumsum(tiles_e)
    step_start = step_end - tiles_e
    total = step_end[-1]
    s = jnp.arange(n_steps, dtype=jnp.int32)
    s_eff = jnp.minimum(s, total - 1)
    mine = ((step_start[None, :] <= s_eff[:, None]) & (s_eff[:, None] < step_end[None, :])).astype(jnp.int32)
    pick = lambda table: jnp.sum(mine * table[None, :], axis=1)
    exp = pick(jnp.arange(n_e, dtype=jnp.int32))
    tile = pick(first_tile) + (s_eff - pick(step_start))
    lo = jnp.maximum(pick(starts) - tile * tm, 0)
    lo = jnp.where(s < total, lo, tm)
    return tile.astype(jnp.int32), exp, lo.astype(jnp.int32), starts.astype(jnp.int32)


def _combine_kernel(pos_ref, pos_next_ref, x2_ref, gates_ref, p_ref, wproj_ref, wgate_ref, gple_ref, gfin_ref,
                    ys_ref, out_ref, rows_s, sem, *, tm):
    i = pl.program_id(0)
    slot = i % 2

    def gather(p_ref, sl):
        def issue(r, carry):
            for k in range(2):
                _token_copy(ys_ref, p_ref[k * tm + r], rows_s.at[sl, k], r, sem.at[sl]).start(priority=k)
            return carry

        lax.fori_loop(0, tm, issue, 0, unroll=ISSUE_UNROLL)

    @pl.when(i == 0)
    def _first_tile():
        gather(pos_ref, 0)

    @pl.when(i + 1 < pl.num_programs(0))
    def _next_tile():
        gather(pos_next_ref, 1 - slot)

    def drain(r, carry):
        for k in range(2):
            _token_copy(ys_ref, 0, rows_s.at[slot, k], 0, sem.at[slot]).wait()
        return carry

    lax.fori_loop(0, tm, drain, 0, unroll=ISSUE_UNROLL)

    n = tm // COMBINE_CHUNKS
    for ch in range(COMBINE_CHUNKS):
        rows = slice(ch * n, (ch + 1) * n)
        gates = gates_ref[rows, :]
        y = (gates[:, 4:5] * _load_token_tiles(rows_s.at[slot, 0], n, ch * n)
             + gates[:, 5:6] * _load_token_tiles(rows_s.at[slot, 1], n, ch * n))
        x3 = x2_ref[rows, :] + y
        h3 = _rms(x3, gple_ref[...]).astype(BF16)
        gate = jax.nn.sigmoid(_dot(h3, wgate_ref[...]))
        pe = _dot(p_ref[rows, :].astype(BF16), wproj_ref[...])
        x4 = x3 + pe * gate
        out_ref[rows, :] = _rms(x4, gfin_ref[...])


def _combine_ple(pos, x2, meta, p2d, w_ple_proj, w_ple_gate, g_ple, g_final, ys):
    t, d = x2.shape
    d_ple = p2d.shape[1]
    tm = TM
    n_tiles = t // tm
    kern = functools.partial(_combine_kernel, tm=tm)
    tok = lambda w: pl.BlockSpec((tm, w), lambda i: (i, 0))
    full = lambda *shape: pl.BlockSpec(shape, lambda i: (0,) * len(shape))
    return pl.pallas_call(
        kern,
        grid=(t // tm,),
        in_specs=[pl.BlockSpec((2 * tm,), lambda i: (i,), memory_space=pltpu.SMEM),
                  pl.BlockSpec((2 * tm,), lambda i: (jnp.minimum(i + 1, n_tiles - 1),), memory_space=pltpu.SMEM),
                  tok(d), tok(LANES), tok(d_ple), full(d_ple, d), full(d, d), full(1, d), full(1, d),
                  pl.BlockSpec(memory_space=pl.ANY)],
        out_specs=tok(d),
        out_shape=jax.ShapeDtypeStruct((t, d), F32),
        scratch_shapes=[pltpu.VMEM((2, 2, tm * SUBLANES, LANES), F32), pltpu.SemaphoreType.DMA((2,))],
        compiler_params=pltpu.CompilerParams(dimension_semantics=("arbitrary",), vmem_limit_bytes=VMEM_LIMIT),
        name="combine_ple",
    )(pos, pos, x2, meta, p2d, w_ple_proj.astype(BF16), w_ple_gate.astype(BF16),
      g_ple.reshape(1, d).astype(F32), g_final.reshape(1, d).astype(F32), ys)


def _layer(x2d, p2d, seq, g_mix, w_in, w_pool, pool_scale, w_out, near_bias, g_ffn,
           w_r1, b_r1, w_r2, b_r2, w_gate, w_up, w_down, g_ple, w_ple_proj, w_ple_gate, g_final):
    t, d = x2d.shape
    batch = t // seq
    qkv, u = _in_proj(x2d, g_mix, w_in)
    a = _moba_attention(qkv.reshape(batch, seq, 3 * D_ATTN), near_bias).reshape(t, D_ATTN)

    assert N_GROUPS <= SUBLANES and E_PER_GROUP == SUBLANES
    pad_g = SUBLANES - N_GROUPS
    pad_e = LANES - SUBLANES - N_EXPERTS
    wr = jnp.concatenate([w_r1, jnp.zeros((d, pad_g), F32),
                          jnp.transpose(w_r2, (1, 0, 2)).reshape(d, N_EXPERTS),
                          jnp.zeros((d, pad_e), F32)], axis=1).astype(F32)
    br = jnp.concatenate([b_r1, jnp.zeros((pad_g,), F32), b_r2.reshape(N_EXPERTS), jnp.zeros((pad_e,), F32)])
    br = br.astype(F32).reshape(1, LANES)
    assert d == SUBLANES * LANES, "token-tiled rows assume one (8,128) tile per token"
    x2, h2t, meta, metat, cnt = _mix_route(x2d, a, u, w_pool, pool_scale, w_out, g_ffn, wr, br, seq)

    counts = cnt[:, 0].astype(jnp.int32)
    step_tile, step_exp, step_lo, offs = _expert_steps(counts, 2 * t, TM_EXPERT)
    pos = _moe_positions(offs, metat, TM)
    xs = _moe_scatter(pos, h2t)
    f = w_gate.shape[-1]
    ys = _moe_experts(step_tile, step_exp, step_lo, xs,
                      w_gate.reshape(N_EXPERTS, d, f), w_up.reshape(N_EXPERTS, d, f),
                      w_down.reshape(N_EXPERTS, f, d))
    return _combine_ple(pos, x2, meta, p2d, w_ple_proj, w_ple_gate, g_ple, g_final, ys)


def kernel(x, p, g_mix, w_in, w_pool, pool_scale, w_out, rel_bias, g_ffn, w_r1, b_r1, w_r2, b_r2,
           w_gate, w_up, w_down, g_ple, w_ple_proj, w_ple_gate, g_final):
    batch, seq, d = x.shape
    depth = p.shape[0]
    assert depth == 1, "the final norm is fused into the last stage of a single layer"
    near_bias = _bias_tables(rel_bias, seq)
    out = _layer(x.reshape(batch * seq, d), p[0].reshape(batch * seq, -1), seq,
                 g_mix[0], w_in[0], w_pool[0], pool_scale[0], w_out[0], near_bias, g_ffn[0],
                 w_r1[0], b_r1[0], w_r2[0], b_r2[0], w_gate[0], w_up[0], w_down[0],
                 g_ple[0], w_ple_proj[0], w_ple_gate[0], g_final)
    return out.reshape(batch, seq, d)
```

```python
import functools
import math

import numpy as np
import jax
import jax.numpy as jnp
from jax import lax
from jax.experimental import pallas as pl
from jax.experimental.pallas import tpu as pltpu

F32 = jnp.float32
BF16 = jnp.bfloat16

HEAD_DIM = 64
N_HEADS = 8
D_ATTN = N_HEADS * HEAD_DIM
POOL_WINDOWS = (2, 4, 8, 16)
POOL_GROUP_DIM = 128
D_POOL = POOL_GROUP_DIM * len(POOL_WINDOWS)
MOBA_BLOCK = 256
MOBA_TOPK = 3
REL_BUCKETS = 32
REL_MAX_DIST = 128
N_GROUPS = 4
E_PER_GROUP = 8
N_EXPERTS = N_GROUPS * E_PER_GROUP
RMS_EPS = 1e-6

LANES = 128
SUBLANES = 8
VMEM_LIMIT = 56 * 1024 * 1024

MASK_VALUE = -1e30
LOG2E = math.log2(math.e)
HALO = max(POOL_WINDOWS)
TM_PROJ = 512
TM = 512
TM_EXPERT = 512
PAIR = 2 * HEAD_DIM


def _rms(x, g):
    return x * lax.rsqrt(jnp.mean(x * x, axis=-1, keepdims=True) + RMS_EPS) * g


def _dot(a, b):
    return jnp.dot(a, b, preferred_element_type=F32)


def _dot_nt(a, b):
    return lax.dot_general(a, b, (((1,), (1,)), ((), ())), preferred_element_type=F32)


def _store_token_tiles(ref, x):
    n = x.shape[0]
    for c in range(x.shape[1] // LANES):
        ref[pl.ds(c, n, stride=SUBLANES), :] = x[:, c * LANES:(c + 1) * LANES]


def _load_token_tiles(ref, n, first=0):
    return jnp.concatenate([ref[pl.ds(first * SUBLANES + c, n, stride=SUBLANES), :] for c in range(SUBLANES)],
                           axis=1)


def _token_copy(src_ref, src_tok, dst_ref, dst_tok, sem):
    src = src_ref.at[pl.ds(pl.multiple_of(src_tok * SUBLANES, SUBLANES), SUBLANES)]
    dst = dst_ref.at[pl.ds(pl.multiple_of(dst_tok * SUBLANES, SUBLANES), SUBLANES)]
    return pltpu.make_async_copy(src, dst, sem)


def _rel_bucket_np(n):
    n = np.maximum(n, 0)
    max_exact = REL_BUCKETS // 2
    nf = np.maximum(n, 1).astype(np.float32)
    large = max_exact + (np.log(nf / np.float32(max_exact)) / np.float32(math.log(REL_MAX_DIST / max_exact))
                         * np.float32(REL_BUCKETS - max_exact)).astype(np.int32)
    large = np.minimum(large, REL_BUCKETS - 1)
    return np.where(n < max_exact, n, large).astype(np.int32)


def _bucket_tiles(seq):
    r = np.arange(MOBA_BLOCK)
    d_own = r[:, None] - r[None, :]
    own = np.where(d_own >= 0, _rel_bucket_np(d_own), -1).astype(np.int32)
    prev = _rel_bucket_np(d_own + MOBA_BLOCK)
    far = _rel_bucket_np(np.arange(MOBA_BLOCK + 1, max(seq, MOBA_BLOCK + 2)))
    assert np.all(far == REL_BUCKETS - 1)
    first = np.concatenate([own, np.full_like(own, -1)], axis=1)
    later = np.concatenate([prev, own], axis=1)
    return np.stack([first.T, later.T])


def _bias_kernel(rb_ref, bucket_ref, near_ref):
    h = pl.program_id(0)
    far = rb_ref[REL_BUCKETS - 1, h]
    bucket = bucket_ref[...]
    tile = jnp.where(bucket < 0, MASK_VALUE, 0.0).astype(F32)
    for b in range(REL_BUCKETS):
        tile = jnp.where(bucket == b, (rb_ref[b, h] - far) * LOG2E, tile)
    near_ref[...] = tile


def _bias_tables(rel_bias, seq):
    buckets = _bucket_tiles(seq)
    blk = MOBA_BLOCK
    return pl.pallas_call(
        _bias_kernel,
        grid=(N_HEADS, 2),
        in_specs=[pl.BlockSpec(memory_space=pltpu.SMEM),
                  pl.BlockSpec((None, 2 * blk, blk), lambda h, v: (v, 0, 0))],
        out_specs=pl.BlockSpec((None, None, 2 * blk, blk), lambda h, v: (h, v, 0, 0)),
        out_shape=jax.ShapeDtypeStruct((N_HEADS, 2, 2 * blk, blk), F32),
        name="bias_tables",
    )(rel_bias.astype(F32), jnp.asarray(buckets))


def _inproj_kernel(x_ref, g_ref, w_ref, qkv_ref, u_ref):
    hb = _rms(x_ref[...], g_ref[...]).astype(BF16)
    scale = HEAD_DIM ** -0.5 * LOG2E
    qkv_ref[:, :D_ATTN] = (_dot(hb, w_ref[:, :D_ATTN]) * scale).astype(BF16)
    qkv_ref[:, D_ATTN:2 * D_ATTN] = _dot(hb, w_ref[:, D_ATTN:2 * D_ATTN]).astype(BF16)
    qkv_ref[:, 2 * D_ATTN:] = _dot(hb, w_ref[:, 2 * D_ATTN:3 * D_ATTN]).astype(BF16)
    u_ref[...] = _dot(hb, w_ref[:, 3 * D_ATTN:])


def _in_proj(x2d, g_mix, w_in):
    t, d = x2d.shape
    n_in = w_in.shape[1]
    tm = min(TM_PROJ, t)
    return pl.pallas_call(
        _inproj_kernel,
        grid=(t // tm,),
        in_specs=[pl.BlockSpec((tm, d), lambda i: (i, 0)),
                  pl.BlockSpec((1, d), lambda i: (0, 0)),
                  pl.BlockSpec((d, n_in), lambda i: (0, 0))],
        out_specs=[pl.BlockSpec((tm, 3 * D_ATTN), lambda i: (i, 0)),
                   pl.BlockSpec((tm, D_POOL), lambda i: (i, 0))],
        out_shape=[jax.ShapeDtypeStruct((t, 3 * D_ATTN), BF16),
                   jax.ShapeDtypeStruct((t, D_POOL), F32)],
        compiler_params=pltpu.CompilerParams(dimension_semantics=("parallel",), vmem_limit_bytes=VMEM_LIMIT),
        name="in_proj",
    )(x2d, g_mix.reshape(1, d).astype(F32), w_in.astype(BF16))


FAR_BLOCKS = 2
PAIRS_PER_STEP = 2
HEADS_PER_STEP = 2 * PAIRS_PER_STEP
QBLOCKS_PER_STEP = 2
STREAMS = QBLOCKS_PER_STEP * HEADS_PER_STEP
ONES_ROWS = 16
VT_ROWS = HEAD_DIM + ONES_ROWS


def _attn_kernel(q_ref, k_ref, v_ref, nearb0_ref, nearb1_ref, o_ref,
                 kaug, vt, kmean, qnear, qfar, sbuf, m_s, acc_s, *, nb):
    blk = MOBA_BLOCK
    step = pl.program_id(2)
    pair_lanes = [slice(pp * LANES, (pp + 1) * LANES) for pp in range(PAIRS_PER_STEP)]
    q_rows = [slice(qb * blk, (qb + 1) * blk) for qb in range(QBLOCKS_PER_STEP)]
    near_bias = (nearb0_ref, nearb1_ref)

    @pl.when(step == 0)
    def _build_keys():
        kmean[...] = jnp.zeros(kmean.shape, F32)
        lane_b = lax.broadcasted_iota(jnp.int32, (blk, LANES), 1)

        def body(j, carry):
            rows = pl.ds(pl.multiple_of(j * blk, blk), blk)
            kmean[pl.ds(j, 1), :] = jnp.mean(k_ref[rows, :].astype(F32), axis=0, keepdims=True)
            for pp in range(PAIRS_PER_STEP):
                kaug[pp, rows, :LANES] = k_ref[rows, pair_lanes[pp]]
                kaug[pp, rows, LANES:] = jnp.where(lane_b == j, 1.0, 0.0).astype(BF16)
                v_t = jnp.transpose(v_ref[rows, pair_lanes[pp]].astype(F32))
                for half in range(2):
                    hh = 2 * pp + half
                    vt[hh, j, :HEAD_DIM, :] = v_t[half * HEAD_DIM:(half + 1) * HEAD_DIM, :].astype(BF16)
                    vt[hh, j, HEAD_DIM:, :] = jnp.ones((ONES_ROWS, blk), BF16)
            return carry

        lax.fori_loop(0, nb, body, 0)

    lane = lax.broadcasted_iota(jnp.int32, (blk, LANES), 1)
    nbp = kmean.shape[0]
    blk_id = lax.broadcasted_iota(jnp.int32, (nbp, blk), 0)

    for st in range(STREAMS):
        qb, hh = divmod(st, HEADS_PER_STEP)
        pp, half = divmod(hh, 2)
        i = QBLOCKS_PER_STEP * step + qb
        q = q_ref[q_rows[qb], pair_lanes[pp]]
        in_head = (lane < HEAD_DIM) if half == 0 else (lane >= HEAD_DIM)
        qm = jnp.where(in_head, q, jnp.zeros_like(q))
        kmean_b = kmean[:, pair_lanes[pp]].astype(BF16)
        gate = jnp.where(blk_id < i, _dot_nt(kmean_b, qm), -jnp.inf)
        chosen_t = jnp.zeros((nbp, blk), F32)
        for _ in range(MOBA_TOPK):
            top = jnp.max(gate, axis=0, keepdims=True)
            idx = jnp.min(jnp.where(gate == top, blk_id, nbp), axis=0, keepdims=True)
            hit = blk_id == idx
            chosen_t = jnp.where(hit, 1.0, chosen_t)
            gate = jnp.where(hit, -jnp.inf, gate)
        if nbp < LANES:
            chosen_t = jnp.concatenate([chosen_t, jnp.zeros((LANES - nbp, blk), F32)], axis=0)
        penalty = jnp.where(jnp.transpose(chosen_t) > 0.0, 0.0, MASK_VALUE)
        qnear[st, :, :LANES] = qm
        qnear[st, :, LANES:] = jnp.where(lane < i, penalty, 0.0).astype(BF16)
        qfar[st, :, :LANES] = qm
        qfar[st, :, LANES:] = jnp.where(lane < i - 1, penalty, MASK_VALUE).astype(BF16)

    def values_t(hh, first_block):
        return jnp.concatenate([vt[hh, first_block + k] for k in range(FAR_BLOCKS)], axis=1)

    def fold(st, s_t, v_t, first):
        m_blk = jnp.max(s_t, axis=0, keepdims=True)
        if first:
            m_new = m_blk
        else:
            m_old = m_s[st][0:1]
            m_new = jnp.maximum(m_old, m_blk)
            alpha = jnp.exp2(m_old - m_new)
        p_t = jnp.exp2(s_t - m_new)
        pv = _dot(v_t, p_t.astype(BF16))
        acc_s[st] = pv if first else alpha * acc_s[st] + pv
        m_s[st] = jnp.broadcast_to(m_new, m_s.shape[1:])

    near_blocks = [jnp.maximum(QBLOCKS_PER_STEP * step + qb - 1, 0) for qb in range(QBLOCKS_PER_STEP)]
    near_scores = []
    for st in range(STREAMS):
        qb, hh = divmod(st, HEADS_PER_STEP)
        near = pl.ds(pl.multiple_of(near_blocks[qb] * blk, blk), 2 * blk)
        near_scores.append(_dot_nt(kaug[hh // 2, near, :], qnear[st]) + near_bias[qb][hh])

    chunk = FAR_BLOCKS * blk
    last_chunk = kaug.shape[1] // chunk - 1
    n_far = (QBLOCKS_PER_STEP * step + QBLOCKS_PER_STEP - 1 + FAR_BLOCKS - 2) // FAR_BLOCKS

    def far_scores(c, slot):
        rows = pl.ds(pl.multiple_of(jnp.minimum(c, last_chunk) * chunk, chunk), chunk)
        for st in range(STREAMS):
            sbuf[slot, st] = _dot_nt(kaug[(st % HEADS_PER_STEP) // 2, rows, :], qfar[st])

    def far_fold(c, slot):
        first_block = jnp.minimum(c, last_chunk) * FAR_BLOCKS
        for st in range(STREAMS):
            fold(st, sbuf[slot, st], values_t(st % HEADS_PER_STEP, first_block), False)

    far_scores(0, 0)
    for st in range(STREAMS):
        qb, hh = divmod(st, HEADS_PER_STEP)
        fold(st, near_scores[st], values_t(hh, near_blocks[qb]), True)

    def far_body(cc, carry):
        c = 2 * cc
        far_scores(c + 1, 1)
        far_fold(c, 0)
        far_scores(c + 2, 0)
        far_fold(c + 1, 1)
        return carry

    lax.fori_loop(0, (n_far + 1) // 2, far_body, 0)

    for qb in range(QBLOCKS_PER_STEP):
        for pp in range(PAIRS_PER_STEP):
            outs = []
            for hh in (2 * pp, 2 * pp + 1):
                acc = acc_s[qb * HEADS_PER_STEP + hh]
                outs.append(acc[:HEAD_DIM] / acc[HEAD_DIM:HEAD_DIM + 1])
            o_ref[q_rows[qb], pair_lanes[pp]] = jnp.transpose(jnp.concatenate(outs, axis=0)).astype(o_ref.dtype)


def _moba_attention(qkv, near_bias):
    b, s, _ = qkv.shape
    blk = MOBA_BLOCK
    nb = s // blk
    nbp = -(-nb // SUBLANES) * SUBLANES
    width = PAIRS_PER_STEP * PAIR
    n_groups = D_ATTN // width
    qrows = QBLOCKS_PER_STEP * blk
    assert FAR_BLOCKS == 2 and QBLOCKS_PER_STEP == 2, "near / far chunk bookkeeping assumes pairs of blocks"
    assert s % (FAR_BLOCKS * blk) == 0 and s % qrows == 0 and nbp <= LANES and D_ATTN % width == 0
    max_far = (nb - 1 + FAR_BLOCKS - 2) // FAR_BLOCKS
    assert max_far % 2 == 0 or max_far <= nb // FAR_BLOCKS - 1
    kern = functools.partial(_attn_kernel, nb=nb)
    hps = HEADS_PER_STEP
    bias_tile = (hps, None, 2 * blk, blk)
    return pl.pallas_call(
        kern,
        grid=(b, n_groups, nb // QBLOCKS_PER_STEP),
        in_specs=[pl.BlockSpec((None, qrows, width), lambda bi, g, i: (bi, i, g)),
                  pl.BlockSpec((None, s, width), lambda bi, g, i: (bi, 0, n_groups + g)),
                  pl.BlockSpec((None, s, width), lambda bi, g, i: (bi, 0, 2 * n_groups + g)),
                  pl.BlockSpec(bias_tile, lambda bi, g, i: (g, jnp.minimum(i, 1), 0, 0)),
                  pl.BlockSpec(bias_tile, lambda bi, g, i: (g, 1, 0, 0))],
        out_specs=pl.BlockSpec((None, qrows, width), lambda bi, g, i: (bi, i, g)),
        out_shape=jax.ShapeDtypeStruct((b, s, D_ATTN), BF16),
        scratch_shapes=[pltpu.VMEM((PAIRS_PER_STEP, s, 2 * LANES), BF16),
                        pltpu.VMEM((hps, nb, VT_ROWS, blk), BF16),
                        pltpu.VMEM((nbp, width), F32),
                        pltpu.VMEM((STREAMS, blk, 2 * LANES), BF16),
                        pltpu.VMEM((STREAMS, blk, 2 * LANES), BF16),
                        pltpu.VMEM((2, STREAMS, FAR_BLOCKS * blk, blk), F32),
                        pltpu.VMEM((STREAMS, SUBLANES, blk), F32),
                        pltpu.VMEM((STREAMS, VT_ROWS, blk), F32)],
        compiler_params=pltpu.CompilerParams(
            dimension_semantics=("parallel", "parallel", "arbitrary"), vmem_limit_bytes=VMEM_LIMIT),
        name="moba_attn",
    )(qkv, qkv, qkv, near_bias, near_bias)


def _mix_kernel(x_ref, a_ref, u_ref, uh_ref, wpool_ref, pscale_ref, wout_ref, gffn_ref,
                wr_hi_ref, wr_lo_ref, br_ref,
                x2_ref, h2_ref, meta_ref, metat_ref, cnt_ref, ext_s, carry_s, *, tm, seq):
    t = pl.program_id(0)

    @pl.when(t == 0)
    def _init():
        carry_s[...] = jnp.zeros(carry_s.shape, F32)

    pos0 = (t * tm) % seq
    ext_s[0:HALO, :] = jnp.where(pos0 == 0, 0.0, uh_ref[...])
    ext_s[HALO:HALO + tm, :] = u_ref[...]
    pos = pos0 + lax.broadcasted_iota(jnp.int32, (tm, 1), 0)
    parts = []
    for gi, w in enumerate(POOL_WINDOWS):
        c0 = gi * POOL_GROUP_DIM
        cols = slice(c0, c0 + POOL_GROUP_DIM)
        tok = ext_s[HALO:HALO + tm, cols]
        win = tok
        for d in range(1, w):
            win = win + ext_s[HALO - d:HALO - d + tm, cols]
        cnt = jnp.minimum(pos + 1, w).astype(F32)
        pooled = win / cnt - tok
        parts.append(_dot(pooled.astype(BF16), wpool_ref[gi]) * pscale_ref[:, cols])
    b = jnp.concatenate(parts, axis=1).astype(BF16)

    mix = _dot(a_ref[...], wout_ref[:D_ATTN, :]) + _dot(b, wout_ref[D_ATTN:, :])
    x2 = x_ref[...] + mix
    x2_ref[...] = x2
    h2 = _rms(x2, gffn_ref[...])
    _store_token_tiles(h2_ref, h2)

    hi = h2.astype(BF16)
    lo = (h2 - hi.astype(F32)).astype(BF16)
    logits = jnp.transpose(_dot(hi, wr_hi_ref[...]) + _dot(lo, wr_hi_ref[...]) + _dot(hi, wr_lo_ref[...])
                           + br_ref[...])
    row = lax.broadcasted_iota(jnp.int32, (SUBLANES, tm), 0)
    lg = jnp.where(row < N_GROUPS, logits[0:SUBLANES], -jnp.inf)
    top_g = jnp.max(lg, axis=0, keepdims=True)
    g_idx = jnp.min(jnp.where(lg == top_g, row, SUBLANES), axis=0, keepdims=True)
    p_g = 1.0 / jnp.sum(jnp.exp(lg - top_g), axis=0, keepdims=True)
    l2 = logits[SUBLANES:2 * SUBLANES]
    for g in range(1, N_GROUPS):
        l2 = jnp.where(g_idx == g, logits[(g + 1) * SUBLANES:(g + 2) * SUBLANES], l2)
    v1 = jnp.max(l2, axis=0, keepdims=True)
    i1 = jnp.min(jnp.where(l2 == v1, row, SUBLANES), axis=0, keepdims=True)
    l2 = jnp.where(row == i1, -jnp.inf, l2)
    v2 = jnp.max(l2, axis=0, keepdims=True)
    i2 = jnp.min(jnp.where(l2 == v2, row, SUBLANES), axis=0, keepdims=True)
    e1 = g_idx * E_PER_GROUP + i1
    e2 = g_idx * E_PER_GROUP + i2
    t2 = jnp.exp(v2 - v1)
    w1 = 1.0 / (1.0 + t2)
    gate1 = p_g * w1
    gate2 = p_g * (t2 * w1)

    e_row = lax.broadcasted_iota(jnp.int32, (N_EXPERTS, tm), 0)
    oh1 = jnp.where(e_row == e1, 1.0, 0.0)
    oh2 = jnp.where(e_row == e2, 1.0, 0.0)
    oh = oh1 + oh2
    src = lax.broadcasted_iota(jnp.int32, (tm, tm), 0)
    dst = lax.broadcasted_iota(jnp.int32, (tm, tm), 1)
    earlier = jnp.where(src < dst, 1.0, 0.0).astype(BF16)
    seen = _dot(oh.astype(BF16), earlier) + carry_s[:, 0:1]
    r1 = jnp.sum(oh1 * seen, axis=0, keepdims=True)
    r2 = jnp.sum(oh2 * seen, axis=0, keepdims=True)
    carry_s[...] = carry_s[...] + jnp.sum(oh, axis=1, keepdims=True)
    cnt_ref[...] = carry_s[...]

    rows = (e1.astype(F32), e2.astype(F32), r1, r2, gate1, gate2)
    meta_t = jnp.zeros((SUBLANES, tm), F32)
    for k, val in enumerate(rows):
        meta_t = jnp.where(row == k, val, meta_t)
    metat_ref[...] = meta_t.astype(jnp.int32)
    meta_ref[...] = jnp.transpose(jnp.concatenate([meta_t, jnp.zeros((LANES - SUBLANES, tm), F32)], axis=0))


def _mix_route(x2d, a2d, u2d, w_pool, pool_scale, w_out, g_ffn, wr, br, seq):
    t, d = x2d.shape
    tm = TM
    assert seq % tm == 0 and tm % HALO == 0
    hb = tm // HALO
    kern = functools.partial(_mix_kernel, tm=tm, seq=seq)
    tok = lambda w: pl.BlockSpec((tm, w), lambda i: (i, 0))
    full = lambda *shape: pl.BlockSpec(shape, lambda i: (0,) * len(shape))
    wr_hi = wr.astype(BF16)
    wr_lo = (wr - wr_hi.astype(F32)).astype(BF16)
    return pl.pallas_call(
        kern,
        grid=(t // tm,),
        in_specs=[tok(d), tok(D_ATTN), tok(D_POOL),
                  pl.BlockSpec((HALO, D_POOL), lambda i: (jnp.maximum(i * hb - 1, 0), 0)),
                  full(len(POOL_WINDOWS), POOL_GROUP_DIM, POOL_GROUP_DIM), full(1, D_POOL),
                  full(D_ATTN + D_POOL, d), full(1, d), full(d, LANES), full(d, LANES), full(1, LANES)],
        out_specs=[tok(d), pl.BlockSpec((tm * SUBLANES, LANES), lambda i: (i, 0)), tok(LANES),
                   pl.BlockSpec((SUBLANES, tm), lambda i: (0, i)),
                   full(N_EXPERTS, LANES)],
        out_shape=[jax.ShapeDtypeStruct((t, d), F32), jax.ShapeDtypeStruct((t * SUBLANES, LANES), F32),
                   jax.ShapeDtypeStruct((t, LANES), F32),
                   jax.ShapeDtypeStruct((SUBLANES, t), jnp.int32),
                   jax.ShapeDtypeStruct((N_EXPERTS, LANES), F32)],
        scratch_shapes=[pltpu.VMEM((HALO + tm, D_POOL), F32), pltpu.VMEM((N_EXPERTS, LANES), F32)],
        compiler_params=pltpu.CompilerParams(dimension_semantics=("arbitrary",), vmem_limit_bytes=VMEM_LIMIT),
        name="mix_route",
    )(x2d, a2d, u2d, u2d, w_pool.astype(BF16), pool_scale.reshape(1, D_POOL).astype(F32),
      w_out.astype(BF16), g_ffn.reshape(1, d).astype(F32), wr_hi, wr_lo, br)


ISSUE_UNROLL = 8
COMBINE_CHUNKS = 2


def _pos_kernel(offs_ref, meta_ref, pos_ref):
    meta = meta_ref[...]
    start = jnp.zeros(meta.shape, jnp.int32)
    for e in range(N_EXPERTS):
        start = jnp.where(meta == e, offs_ref[e], start)
    pos_ref[...] = start[0:2, :] + meta[2:4, :]


def _moe_positions(offs, metat, tm):
    t = metat.shape[1]
    pos = pl.pallas_call(
        _pos_kernel,
        in_specs=[pl.BlockSpec(memory_space=pltpu.SMEM), pl.BlockSpec(memory_space=pltpu.VMEM)],
        out_specs=pl.BlockSpec(memory_space=pltpu.VMEM),
        out_shape=jax.ShapeDtypeStruct((2, t), jnp.int32),
        name="moe_pos",
    )(offs, metat)
    return pos.reshape(2, t // tm, tm).transpose(1, 0, 2).reshape(2 * t)


def _scatter_kernel(pos_ref, h2_ref, xs_ref, sem, *, tm):
    def issue(r, carry):
        for k in range(2):
            _token_copy(h2_ref, r, xs_ref, pos_ref[k * tm + r], sem).start(priority=k)
        return carry

    lax.fori_loop(0, tm, issue, 0, unroll=ISSUE_UNROLL)

    def drain(r, carry):
        for k in range(2):
            _token_copy(h2_ref, 0, xs_ref, 0, sem).wait()
        return carry

    lax.fori_loop(0, tm, drain, 0, unroll=ISSUE_UNROLL)


def _moe_scatter(pos, h2t):
    rows = h2t.shape[0]
    tm = TM
    kern = functools.partial(_scatter_kernel, tm=tm)
    return pl.pallas_call(
        kern,
        grid=(rows // (tm * SUBLANES),),
        in_specs=[pl.BlockSpec((2 * tm,), lambda i: (i,), memory_space=pltpu.SMEM),
                  pl.BlockSpec((tm * SUBLANES, LANES), lambda i: (i, 0))],
        out_specs=pl.BlockSpec(memory_space=pl.ANY),
        out_shape=jax.ShapeDtypeStruct((2 * rows, LANES), F32),
        scratch_shapes=[pltpu.SemaphoreType.DMA(())],
        compiler_params=pltpu.CompilerParams(dimension_semantics=("arbitrary",), vmem_limit_bytes=VMEM_LIMIT),
        name="moe_scatter",
    )(pos, h2t)


def _expert_kernel(tile_ref, exp_ref, lo_ref, xs_ref, wg_ref, wu_ref, wd_ref, ys_ref,
                   wg_s, wu_s, wd_s, *, tm):
    s = pl.program_id(0)
    lo = lo_ref[s]
    new_expert = jnp.logical_or(s == 0, exp_ref[s] != exp_ref[jnp.maximum(s - 1, 0)])

    @pl.when(jnp.logical_and(lo < tm, new_expert))
    def _round_weights():
        wg_s[...] = wg_ref[...].astype(BF16)
        wu_s[...] = wu_ref[...].astype(BF16)
        wd_s[...] = wd_ref[...].astype(BF16)

    def expert_rows():
        x = _load_token_tiles(xs_ref, tm).astype(BF16)
        a = _dot(x, wg_s[...])
        u = _dot(x, wu_s[...])
        hid = (a * jax.nn.sigmoid(a) * u).astype(BF16)
        return _dot(hid, wd_s[...])

    @pl.when(lo == 0)
    def _first_visit():
        _store_token_tiles(ys_ref, expert_rows())

    @pl.when(jnp.logical_and(lo > 0, lo < tm))
    def _later_visit():
        row = lax.broadcasted_iota(jnp.int32, (tm, 1), 0)
        _store_token_tiles(ys_ref, jnp.where(row >= lo, expert_rows(), _load_token_tiles(ys_ref, tm)))


def _moe_experts(step_tile, step_exp, step_lo, xs, w_gate, w_up, w_down):
    n_rows = xs.shape[0]
    tm = TM_EXPERT
    n_e, d, f = w_gate.shape
    n_steps = step_tile.shape[0]
    kern = functools.partial(_expert_kernel, tm=tm)
    grid_spec = pltpu.PrefetchScalarGridSpec(
        num_scalar_prefetch=3,
        grid=(n_steps,),
        in_specs=[pl.BlockSpec((tm * SUBLANES, LANES), lambda s, tl, ex, lo: (tl[s], 0)),
                  pl.BlockSpec((None, d, f), lambda s, tl, ex, lo: (ex[s], 0, 0)),
                  pl.BlockSpec((None, d, f), lambda s, tl, ex, lo: (ex[s], 0, 0)),
                  pl.BlockSpec((None, f, d), lambda s, tl, ex, lo: (ex[s], 0, 0))],
        out_specs=pl.BlockSpec((tm * SUBLANES, LANES), lambda s, tl, ex, lo: (tl[s], 0)),
        scratch_shapes=[pltpu.VMEM((d, f), BF16), pltpu.VMEM((d, f), BF16), pltpu.VMEM((f, d), BF16)],
    )
    return pl.pallas_call(
        kern,
        grid_spec=grid_spec,
        out_shape=jax.ShapeDtypeStruct((n_rows, LANES), F32),
        compiler_params=pltpu.CompilerParams(dimension_semantics=("arbitrary",), vmem_limit_bytes=VMEM_LIMIT),
        name="moe_experts",
    )(step_tile, step_exp, step_lo, xs, w_gate, w_up, w_down)


def _expert_steps(counts, n_rows, tm):
    n_e = counts.shape[0]
    n_tiles = n_rows // tm
    n_steps = n_tiles + n_e
    ends = jnp.cumsum(counts)
    starts = ends - counts
    first_tile = starts // tm
    last_tile = jnp.maximum(ends - 1, 0) // tm
    tiles_e = jnp.where(counts > 0, last_tile - first_tile + 1, 0)
    step_end = jnp.cumsum(tiles_e)
    step_start = step_end - tiles_e
    total = step_end[-1]
    s = jnp.arange(n_steps, dtype=jnp.int32)
    s_eff = jnp.minimum(s, total - 1)
    mine = ((step_start[None, :] <= s_eff[:, None]) & (s_eff[:, None] < step_end[None, :])).astype(jnp.int32)
    pick = lambda table: jnp.sum(mine * table[None, :], axis=1)
    exp = pick(jnp.arange(n_e, dtype=jnp.int32))
    tile = pick(first_tile) + (s_eff - pick(step_start))
    lo = jnp.maximum(pick(starts) - tile * tm, 0)
    lo = jnp.where(s < total, lo, tm)
    return tile.astype(jnp.int32), exp, lo.astype(jnp.int32), starts.astype(jnp.int32)


def _combine_kernel(pos_ref, pos_next_ref, x2_ref, gates_ref, p_ref, wproj_ref, wgate_ref, gple_ref, gfin_ref,
                    ys_ref, out_ref, rows_s, sem, *, tm):
    i = pl.program_id(0)
    slot = i % 2

    def gather(p_ref, sl):
        def issue(r, carry):
            for k in range(2):
                _token_copy(ys_ref, p_ref[k * tm + r], rows_s.at[sl, k], r, sem.at[sl]).start(priority=k)
            return carry

        lax.fori_loop(0, tm, issue, 0, unroll=ISSUE_UNROLL)

    @pl.when(i == 0)
    def _first_tile():
        gather(pos_ref, 0)

    @pl.when(i + 1 < pl.num_programs(0))
    def _next_tile():
        gather(pos_next_ref, 1 - slot)

    def drain(r, carry):
        for k in range(2):
            _token_copy(ys_ref, 0, rows_s.at[slot, k], 0, sem.at[slot]).wait()
        return carry

    lax.fori_loop(0, tm, drain, 0, unroll=ISSUE_UNROLL)

    n = tm // COMBINE_CHUNKS
    for ch in range(COMBINE_CHUNKS):
        rows = slice(ch * n, (ch + 1) * n)
        gates = gates_ref[rows, :]
        y = (gates[:, 4:5] * _load_token_tiles(rows_s.at[slot, 0], n, ch * n)
             + gates[:, 5:6] * _load_token_tiles(rows_s.at[slot, 1], n, ch * n))
        x3 = x2_ref[rows, :] + y
        h3 = _rms(x3, gple_ref[...]).astype(BF16)
        gate = jax.nn.sigmoid(_dot(h3, wgate_ref[...]))
        pe = _dot(p_ref[rows, :].astype(BF16), wproj_ref[...])
        x4 = x3 + pe * gate
        out_ref[rows, :] = _rms(x4, gfin_ref[...])


def _combine_ple(pos, x2, meta, p2d, w_ple_proj, w_ple_gate, g_ple, g_final, ys):
    t, d = x2.shape
    d_ple = p2d.shape[1]
    tm = TM
    n_tiles = t // tm
    kern = functools.partial(_combine_kernel, tm=tm)
    tok = lambda w: pl.BlockSpec((tm, w), lambda i: (i, 0))
    full = lambda *shape: pl.BlockSpec(shape, lambda i: (0,) * len(shape))
    return pl.pallas_call(
        kern,
        grid=(t // tm,),
        in_specs=[pl.BlockSpec((2 * tm,), lambda i: (i,), memory_space=pltpu.SMEM),
                  pl.BlockSpec((2 * tm,), lambda i: (jnp.minimum(i + 1, n_tiles - 1),), memory_space=pltpu.SMEM),
                  tok(d), tok(LANES), tok(d_ple), full(d_ple, d), full(d, d), full(1, d), full(1, d),
                  pl.BlockSpec(memory_space=pl.ANY)],
        out_specs=tok(d),
        out_shape=jax.ShapeDtypeStruct((t, d), F32),
        scratch_shapes=[pltpu.VMEM((2, 2, tm * SUBLANES, LANES), F32), pltpu.SemaphoreType.DMA((2,))],
        compiler_params=pltpu.CompilerParams(dimension_semantics=("arbitrary",), vmem_limit_bytes=VMEM_LIMIT),
        name="combine_ple",
    )(pos, pos, x2, meta, p2d, w_ple_proj.astype(BF16), w_ple_gate.astype(BF16),
      g_ple.reshape(1, d).astype(F32), g_final.reshape(1, d).astype(F32), ys)


def _layer(x2d, p2d, seq, g_mix, w_in, w_pool, pool_scale, w_out, near_bias, g_ffn,
           w_r1, b_r1, w_r2, b_r2, w_gate, w_up, w_down, g_ple, w_ple_proj, w_ple_gate, g_final):
    t, d = x2d.shape
    batch = t // seq
    qkv, u = _in_proj(x2d, g_mix, w_in)
    a = _moba_attention(qkv.reshape(batch, seq, 3 * D_ATTN), near_bias).reshape(t, D_ATTN)

    assert N_GROUPS <= SUBLANES and E_PER_GROUP == SUBLANES
    pad_g = SUBLANES - N_GROUPS
    pad_e = LANES - SUBLANES - N_EXPERTS
    wr = jnp.concatenate([w_r1, jnp.zeros((d, pad_g), F32),
                          jnp.transpose(w_r2, (1, 0, 2)).reshape(d, N_EXPERTS),
                          jnp.zeros((d, pad_e), F32)], axis=1).astype(F32)
    br = jnp.concatenate([b_r1, jnp.zeros((pad_g,), F32), b_r2.reshape(N_EXPERTS), jnp.zeros((pad_e,), F32)])
    br = br.astype(F32).reshape(1, LANES)
    assert d == SUBLANES * LANES, "token-tiled rows assume one (8,128) tile per token"
    x2, h2t, meta, metat, cnt = _mix_route(x2d, a, u, w_pool, pool_scale, w_out, g_ffn, wr, br, seq)

    counts = cnt[:, 0].astype(jnp.int32)
    step_tile, step_exp, step_lo, offs = _expert_steps(counts, 2 * t, TM_EXPERT)
    pos = _moe_positions(offs, metat, TM)
    xs = _moe_scatter(pos, h2t)
    f = w_gate.shape[-1]
    ys = _moe_experts(step_tile, step_exp, step_lo, xs,
                      w_gate.reshape(N_EXPERTS, d, f), w_up.reshape(N_EXPERTS, d, f),
                      w_down.reshape(N_EXPERTS, f, d))
    return _combine_ple(pos, x2, meta, p2d, w_ple_proj, w_ple_gate, g_ple, g_final, ys)


def kernel(x, p, g_mix, w_in, w_pool, pool_scale, w_out, rel_bias, g_ffn, w_r1, b_r1, w_r2, b_r2,
           w_gate, w_up, w_down, g_ple, w_ple_proj, w_ple_gate, g_final):
    batch, seq, d = x.shape
    depth = p.shape[0]
    assert depth == 1, "the final norm is fused into the last stage of a single layer"
    near_bias = _bias_tables(rel_bias, seq)
    out = _layer(x.reshape(batch * seq, d), p[0].reshape(batch * seq, -1), seq,
                 g_mix[0], w_in[0], w_pool[0], pool_scale[0], w_out[0], near_bias, g_ffn[0],
                 w_r1[0], b_r1[0], w_r2[0], b_r2[0], w_gate[0], w_up[0], w_down[0],
                 g_ple[0], w_ple_proj[0], w_ple_gate[0], g_final)
    return out.reshape(batch, seq, d)
```

```python
import functools
import math

import numpy as np
import jax
import jax.numpy as jnp
from jax import lax
from jax.experimental import pallas as pl
from jax.experimental.pallas import tpu as pltpu

F32 = jnp.float32
BF16 = jnp.bfloat16

HEAD_DIM = 64
N_HEADS = 8
D_ATTN = N_HEADS * HEAD_DIM
POOL_WINDOWS = (2, 4, 8, 16)
POOL_GROUP_DIM = 128
D_POOL = POOL_GROUP_DIM * len(POOL_WINDOWS)
MOBA_BLOCK = 256
MOBA_TOPK = 3
REL_BUCKETS = 32
REL_MAX_DIST = 128
N_GROUPS = 4
E_PER_GROUP = 8
N_EXPERTS = N_GROUPS * E_PER_GROUP
RMS_EPS = 1e-6

LANES = 128
SUBLANES = 8
VMEM_LIMIT = 56 * 1024 * 1024

MASK_VALUE = -1e30
LOG2E = math.log2(math.e)
HALO = max(POOL_WINDOWS)
TM_PROJ = 1024
TM = 512
TM_EXPERT = 512
PAIR = 2 * HEAD_DIM


def _rms(x, g):
    return x * lax.rsqrt(jnp.mean(x * x, axis=-1, keepdims=True) + RMS_EPS) * g


def _dot(a, b):
    return jnp.dot(a, b, preferred_element_type=F32)


def _dot_nt(a, b):
    return lax.dot_general(a, b, (((1,), (1,)), ((), ())), preferred_element_type=F32)


def _store_token_tiles(ref, x, first=0):
    n = x.shape[0]
    for c in range(x.shape[1] // LANES):
        ref[pl.ds(first * SUBLANES + c, n, stride=SUBLANES), :] = x[:, c * LANES:(c + 1) * LANES]


def _load_token_tiles(ref, n, first=0):
    return jnp.concatenate([ref[pl.ds(first * SUBLANES + c, n, stride=SUBLANES), :] for c in range(SUBLANES)],
                           axis=1)


def _token_copy(src_ref, src_tok, dst_ref, dst_tok, sem):
    src = src_ref.at[pl.ds(pl.multiple_of(src_tok * SUBLANES, SUBLANES), SUBLANES)]
    dst = dst_ref.at[pl.ds(pl.multiple_of(dst_tok * SUBLANES, SUBLANES), SUBLANES)]
    return pltpu.make_async_copy(src, dst, sem)


def _rel_bucket_np(n):
    n = np.maximum(n, 0)
    max_exact = REL_BUCKETS // 2
    nf = np.maximum(n, 1).astype(np.float32)
    large = max_exact + (np.log(nf / np.float32(max_exact)) / np.float32(math.log(REL_MAX_DIST / max_exact))
                         * np.float32(REL_BUCKETS - max_exact)).astype(np.int32)
    large = np.minimum(large, REL_BUCKETS - 1)
    return np.where(n < max_exact, n, large).astype(np.int32)


def _bucket_tiles(seq):
    r = np.arange(MOBA_BLOCK)
    d_own = r[:, None] - r[None, :]
    own = np.where(d_own >= 0, _rel_bucket_np(d_own), -1).astype(np.int32)
    prev = _rel_bucket_np(d_own + MOBA_BLOCK)
    far = _rel_bucket_np(np.arange(MOBA_BLOCK + 1, max(seq, MOBA_BLOCK + 2)))
    assert np.all(far == REL_BUCKETS - 1)
    first = np.concatenate([own, np.full_like(own, -1)], axis=1)
    later = np.concatenate([prev, own], axis=1)
    return np.stack([first.T, later.T])


def _bias_kernel(rb_ref, bucket_ref, near_ref):
    h = pl.program_id(0)
    far = rb_ref[REL_BUCKETS - 1, h]
    bucket = bucket_ref[...]
    tile = jnp.where(bucket < 0, MASK_VALUE, 0.0).astype(F32)
    for b in range(REL_BUCKETS):
        tile = jnp.where(bucket == b, (rb_ref[b, h] - far) * LOG2E, tile)
    near_ref[...] = tile


def _bias_tables(rel_bias, seq):
    buckets = _bucket_tiles(seq)
    blk = MOBA_BLOCK
    return pl.pallas_call(
        _bias_kernel,
        grid=(N_HEADS, 2),
        in_specs=[pl.BlockSpec(memory_space=pltpu.SMEM),
                  pl.BlockSpec((None, 2 * blk, blk), lambda h, v: (v, 0, 0))],
        out_specs=pl.BlockSpec((None, None, 2 * blk, blk), lambda h, v: (h, v, 0, 0)),
        out_shape=jax.ShapeDtypeStruct((N_HEADS, 2, 2 * blk, blk), F32),
        name="bias_tables",
    )(rel_bias.astype(F32), jnp.asarray(buckets))


def _inproj_kernel(x_ref, g_ref, w_ref, qkv_ref, u_ref):
    hb = _rms(x_ref[...], g_ref[...]).astype(BF16)
    scale = HEAD_DIM ** -0.5 * LOG2E
    qkv_ref[:, :D_ATTN] = (_dot(hb, w_ref[:, :D_ATTN]) * scale).astype(BF16)
    qkv_ref[:, D_ATTN:2 * D_ATTN] = _dot(hb, w_ref[:, D_ATTN:2 * D_ATTN]).astype(BF16)
    qkv_ref[:, 2 * D_ATTN:] = _dot(hb, w_ref[:, 2 * D_ATTN:3 * D_ATTN]).astype(BF16)
    u_ref[...] = _dot(hb, w_ref[:, 3 * D_ATTN:])


def _in_proj(x2d, g_mix, w_in):
    t, d = x2d.shape
    n_in = w_in.shape[1]
    tm = min(TM_PROJ, t)
    return pl.pallas_call(
        _inproj_kernel,
        grid=(t // tm,),
        in_specs=[pl.BlockSpec((tm, d), lambda i: (i, 0)),
                  pl.BlockSpec((1, d), lambda i: (0, 0)),
                  pl.BlockSpec((d, n_in), lambda i: (0, 0))],
        out_specs=[pl.BlockSpec((tm, 3 * D_ATTN), lambda i: (i, 0)),
                   pl.BlockSpec((tm, D_POOL), lambda i: (i, 0))],
        out_shape=[jax.ShapeDtypeStruct((t, 3 * D_ATTN), BF16),
                   jax.ShapeDtypeStruct((t, D_POOL), F32)],
        compiler_params=pltpu.CompilerParams(dimension_semantics=("parallel",), vmem_limit_bytes=VMEM_LIMIT),
        name="in_proj",
    )(x2d, g_mix.reshape(1, d).astype(F32), w_in.astype(BF16))


FAR_BLOCKS = 2
PAIRS_PER_STEP = 2
HEADS_PER_STEP = 2 * PAIRS_PER_STEP
QBLOCKS_PER_STEP = 2
STREAMS = QBLOCKS_PER_STEP * HEADS_PER_STEP
ONES_ROWS = 16
VT_ROWS = HEAD_DIM + ONES_ROWS


def _attn_kernel(q_ref, k_ref, v_ref, nearb0_ref, nearb1_ref, o_ref,
                 kaug, vt, kmean, qnear, qfar, sbuf, m_s, acc_s, *, nb):
    blk = MOBA_BLOCK
    step = pl.program_id(2)
    pair_lanes = [slice(pp * LANES, (pp + 1) * LANES) for pp in range(PAIRS_PER_STEP)]
    q_rows = [slice(qb * blk, (qb + 1) * blk) for qb in range(QBLOCKS_PER_STEP)]
    near_bias = (nearb0_ref, nearb1_ref)

    @pl.when(step == 0)
    def _build_keys():
        kmean[...] = jnp.zeros(kmean.shape, F32)
        lane_b = lax.broadcasted_iota(jnp.int32, (blk, LANES), 1)

        def body(j, carry):
            rows = pl.ds(pl.multiple_of(j * blk, blk), blk)
            kmean[pl.ds(j, 1), :] = jnp.mean(k_ref[rows, :].astype(F32), axis=0, keepdims=True)
            for pp in range(PAIRS_PER_STEP):
                kaug[pp, rows, :LANES] = k_ref[rows, pair_lanes[pp]]
                kaug[pp, rows, LANES:] = jnp.where(lane_b == j, 1.0, 0.0).astype(BF16)
                v_t = jnp.transpose(v_ref[rows, pair_lanes[pp]].astype(F32))
                for half in range(2):
                    hh = 2 * pp + half
                    vt[hh, j, :HEAD_DIM, :] = v_t[half * HEAD_DIM:(half + 1) * HEAD_DIM, :].astype(BF16)
                    vt[hh, j, HEAD_DIM:, :] = jnp.ones((ONES_ROWS, blk), BF16)
            return carry

        lax.fori_loop(0, nb, body, 0)

    lane = lax.broadcasted_iota(jnp.int32, (blk, LANES), 1)
    nbp = kmean.shape[0]
    blk_id = lax.broadcasted_iota(jnp.int32, (nbp, blk), 0)

    for st in range(STREAMS):
        qb, hh = divmod(st, HEADS_PER_STEP)
        pp, half = divmod(hh, 2)
        i = QBLOCKS_PER_STEP * step + qb
        q = q_ref[q_rows[qb], pair_lanes[pp]]
        in_head = (lane < HEAD_DIM) if half == 0 else (lane >= HEAD_DIM)
        qm = jnp.where(in_head, q, jnp.zeros_like(q))
        kmean_b = kmean[:, pair_lanes[pp]].astype(BF16)
        gate = jnp.where(blk_id < i, _dot_nt(kmean_b, qm), -jnp.inf)
        chosen_t = jnp.zeros((nbp, blk), F32)
        for _ in range(MOBA_TOPK):
            top = jnp.max(gate, axis=0, keepdims=True)
            idx = jnp.min(jnp.where(gate == top, blk_id, nbp), axis=0, keepdims=True)
            hit = blk_id == idx
            chosen_t = jnp.where(hit, 1.0, chosen_t)
            gate = jnp.where(hit, -jnp.inf, gate)
        if nbp < LANES:
            chosen_t = jnp.concatenate([chosen_t, jnp.zeros((LANES - nbp, blk), F32)], axis=0)
        penalty = jnp.where(jnp.transpose(chosen_t) > 0.0, 0.0, MASK_VALUE)
        qnear[st, :, :LANES] = qm
        qnear[st, :, LANES:] = jnp.where(lane < i, penalty, 0.0).astype(BF16)
        qfar[st, :, :LANES] = qm
        qfar[st, :, LANES:] = jnp.where(lane < i - 1, penalty, MASK_VALUE).astype(BF16)

    def values_t(hh, first_block):
        return jnp.concatenate([vt[hh, first_block + k] for k in range(FAR_BLOCKS)], axis=1)

    def fold(st, s_t, v_t, first):
        m_blk = jnp.max(s_t, axis=0, keepdims=True)
        if first:
            m_new = m_blk
        else:
            m_old = m_s[st][0:1]
            m_new = jnp.maximum(m_old, m_blk)
            alpha = jnp.exp2(m_old - m_new)
        p_t = jnp.exp2(s_t - m_new)
        pv = _dot(v_t, p_t.astype(BF16))
        acc_s[st] = pv if first else alpha * acc_s[st] + pv
        m_s[st] = jnp.broadcast_to(m_new, m_s.shape[1:])

    near_blocks = [jnp.maximum(QBLOCKS_PER_STEP * step + qb - 1, 0) for qb in range(QBLOCKS_PER_STEP)]
    near_scores = []
    for st in range(STREAMS):
        qb, hh = divmod(st, HEADS_PER_STEP)
        near = pl.ds(pl.multiple_of(near_blocks[qb] * blk, blk), 2 * blk)
        near_scores.append(_dot_nt(kaug[hh // 2, near, :], qnear[st]) + near_bias[qb][hh])

    chunk = FAR_BLOCKS * blk
    last_chunk = kaug.shape[1] // chunk - 1
    n_far = (QBLOCKS_PER_STEP * step + QBLOCKS_PER_STEP - 1 + FAR_BLOCKS - 2) // FAR_BLOCKS

    def far_scores(c, slot):
        rows = pl.ds(pl.multiple_of(jnp.minimum(c, last_chunk) * chunk, chunk), chunk)
        for st in range(STREAMS):
            sbuf[slot, st] = _dot_nt(kaug[(st % HEADS_PER_STEP) // 2, rows, :], qfar[st])

    def far_fold(c, slot):
        first_block = jnp.minimum(c, last_chunk) * FAR_BLOCKS
        for st in range(STREAMS):
            fold(st, sbuf[slot, st], values_t(st % HEADS_PER_STEP, first_block), False)

    far_scores(0, 0)
    for st in range(STREAMS):
        qb, hh = divmod(st, HEADS_PER_STEP)
        fold(st, near_scores[st], values_t(hh, near_blocks[qb]), True)

    def far_body(cc, carry):
        c = 2 * cc
        far_scores(c + 1, 1)
        far_fold(c, 0)
        far_scores(c + 2, 0)
        far_fold(c + 1, 1)
        return carry

    lax.fori_loop(0, (n_far + 1) // 2, far_body, 0)

    for qb in range(QBLOCKS_PER_STEP):
        for pp in range(PAIRS_PER_STEP):
            outs = []
            for hh in (2 * pp, 2 * pp + 1):
                acc = acc_s[qb * HEADS_PER_STEP + hh]
                outs.append(acc[:HEAD_DIM] / acc[HEAD_DIM:HEAD_DIM + 1])
            o_ref[q_rows[qb], pair_lanes[pp]] = jnp.transpose(jnp.concatenate(outs, axis=0)).astype(o_ref.dtype)


def _moba_attention(qkv, near_bias):
    b, s, _ = qkv.shape
    blk = MOBA_BLOCK
    nb = s // blk
    nbp = -(-nb // SUBLANES) * SUBLANES
    width = PAIRS_PER_STEP * PAIR
    n_groups = D_ATTN // width
    qrows = QBLOCKS_PER_STEP * blk
    assert FAR_BLOCKS == 2 and QBLOCKS_PER_STEP == 2, "near / far chunk bookkeeping assumes pairs of blocks"
    assert s % (FAR_BLOCKS * blk) == 0 and s % qrows == 0 and nbp <= LANES and D_ATTN % width == 0
    max_far = (nb - 1 + FAR_BLOCKS - 2) // FAR_BLOCKS
    assert max_far % 2 == 0 or max_far <= nb // FAR_BLOCKS - 1
    kern = functools.partial(_attn_kernel, nb=nb)
    hps = HEADS_PER_STEP
    bias_tile = (hps, None, 2 * blk, blk)
    return pl.pallas_call(
        kern,
        grid=(b, n_groups, nb // QBLOCKS_PER_STEP),
        in_specs=[pl.BlockSpec((None, qrows, width), lambda bi, g, i: (bi, i, g)),
                  pl.BlockSpec((None, s, width), lambda bi, g, i: (bi, 0, n_groups + g)),
                  pl.BlockSpec((None, s, width), lambda bi, g, i: (bi, 0, 2 * n_groups + g)),
                  pl.BlockSpec(bias_tile, lambda bi, g, i: (g, jnp.minimum(i, 1), 0, 0)),
                  pl.BlockSpec(bias_tile, lambda bi, g, i: (g, 1, 0, 0))],
        out_specs=pl.BlockSpec((None, qrows, width), lambda bi, g, i: (bi, i, g)),
        out_shape=jax.ShapeDtypeStruct((b, s, D_ATTN), BF16),
        scratch_shapes=[pltpu.VMEM((PAIRS_PER_STEP, s, 2 * LANES), BF16),
                        pltpu.VMEM((hps, nb, VT_ROWS, blk), BF16),
                        pltpu.VMEM((nbp, width), F32),
                        pltpu.VMEM((STREAMS, blk, 2 * LANES), BF16),
                        pltpu.VMEM((STREAMS, blk, 2 * LANES), BF16),
                        pltpu.VMEM((2, STREAMS, FAR_BLOCKS * blk, blk), F32),
                        pltpu.VMEM((STREAMS, SUBLANES, blk), F32),
                        pltpu.VMEM((STREAMS, VT_ROWS, blk), F32)],
        compiler_params=pltpu.CompilerParams(
            dimension_semantics=("parallel", "parallel", "arbitrary"), vmem_limit_bytes=VMEM_LIMIT),
        name="moba_attn",
    )(qkv, qkv, qkv, near_bias, near_bias)


def _mix_kernel(x_ref, a_ref, u_ref, uh_ref, wpool_ref, pscale_ref, wout_ref, gffn_ref,
                wr_hi_ref, wr_lo_ref, br_ref,
                x2_ref, h2_ref, meta_ref, metat_ref, cnt_ref, ext_s, carry_s, *, tm, seq):
    t = pl.program_id(0)

    @pl.when(t == 0)
    def _init():
        carry_s[...] = jnp.zeros(carry_s.shape, F32)

    pos0 = (t * tm) % seq
    ext_s[0:HALO, :] = jnp.where(pos0 == 0, 0.0, uh_ref[...])
    ext_s[HALO:HALO + tm, :] = u_ref[...]
    pos = pos0 + lax.broadcasted_iota(jnp.int32, (tm, 1), 0)
    parts = []
    for gi, w in enumerate(POOL_WINDOWS):
        c0 = gi * POOL_GROUP_DIM
        cols = slice(c0, c0 + POOL_GROUP_DIM)
        tok = ext_s[HALO:HALO + tm, cols]
        win = tok
        for d in range(1, w):
            win = win + ext_s[HALO - d:HALO - d + tm, cols]
        cnt = jnp.minimum(pos + 1, w).astype(F32)
        pooled = win / cnt - tok
        parts.append(_dot(pooled.astype(BF16), wpool_ref[gi]) * pscale_ref[:, cols])
    b = jnp.concatenate(parts, axis=1).astype(BF16)

    mix = _dot(a_ref[...], wout_ref[:D_ATTN, :]) + _dot(b, wout_ref[D_ATTN:, :])
    x2 = x_ref[...] + mix
    x2_ref[...] = x2
    h2 = _rms(x2, gffn_ref[...])
    _store_token_tiles(h2_ref, h2)

    hi = h2.astype(BF16)
    lo = (h2 - hi.astype(F32)).astype(BF16)
    logits = jnp.transpose(_dot(hi, wr_hi_ref[...]) + _dot(lo, wr_hi_ref[...]) + _dot(hi, wr_lo_ref[...])
                           + br_ref[...])
    row = lax.broadcasted_iota(jnp.int32, (SUBLANES, tm), 0)
    lg = jnp.where(row < N_GROUPS, logits[0:SUBLANES], -jnp.inf)
    top_g = jnp.max(lg, axis=0, keepdims=True)
    g_idx = jnp.min(jnp.where(lg == top_g, row, SUBLANES), axis=0, keepdims=True)
    p_g = 1.0 / jnp.sum(jnp.exp(lg - top_g), axis=0, keepdims=True)
    l2 = logits[SUBLANES:2 * SUBLANES]
    for g in range(1, N_GROUPS):
        l2 = jnp.where(g_idx == g, logits[(g + 1) * SUBLANES:(g + 2) * SUBLANES], l2)
    v1 = jnp.max(l2, axis=0, keepdims=True)
    i1 = jnp.min(jnp.where(l2 == v1, row, SUBLANES), axis=0, keepdims=True)
    l2 = jnp.where(row == i1, -jnp.inf, l2)
    v2 = jnp.max(l2, axis=0, keepdims=True)
    i2 = jnp.min(jnp.where(l2 == v2, row, SUBLANES), axis=0, keepdims=True)
    e1 = g_idx * E_PER_GROUP + i1
    e2 = g_idx * E_PER_GROUP + i2
    t2 = jnp.exp(v2 - v1)
    w1 = 1.0 / (1.0 + t2)
    gate1 = p_g * w1
    gate2 = p_g * (t2 * w1)

    e_row = lax.broadcasted_iota(jnp.int32, (N_EXPERTS, tm), 0)
    oh1 = jnp.where(e_row == e1, 1.0, 0.0)
    oh2 = jnp.where(e_row == e2, 1.0, 0.0)
    oh = oh1 + oh2
    src = lax.broadcasted_iota(jnp.int32, (tm, tm), 0)
    dst = lax.broadcasted_iota(jnp.int32, (tm, tm), 1)
    earlier = jnp.where(src < dst, 1.0, 0.0).astype(BF16)
    seen = _dot(oh.astype(BF16), earlier) + carry_s[:, 0:1]
    r1 = jnp.sum(oh1 * seen, axis=0, keepdims=True)
    r2 = jnp.sum(oh2 * seen, axis=0, keepdims=True)
    carry_s[...] = carry_s[...] + jnp.sum(oh, axis=1, keepdims=True)
    cnt_ref[...] = carry_s[...]

    rows = (e1.astype(F32), e2.astype(F32), r1, r2, gate1, gate2)
    meta_t = jnp.zeros((SUBLANES, tm), F32)
    for k, val in enumerate(rows):
        meta_t = jnp.where(row == k, val, meta_t)
    metat_ref[...] = meta_t.astype(jnp.int32)
    meta_ref[...] = jnp.transpose(jnp.concatenate([meta_t, jnp.zeros((LANES - SUBLANES, tm), F32)], axis=0))


def _mix_route(x2d, a2d, u2d, w_pool, pool_scale, w_out, g_ffn, wr, br, seq):
    t, d = x2d.shape
    tm = TM
    assert seq % tm == 0 and tm % HALO == 0
    hb = tm // HALO
    kern = functools.partial(_mix_kernel, tm=tm, seq=seq)
    tok = lambda w: pl.BlockSpec((tm, w), lambda i: (i, 0))
    full = lambda *shape: pl.BlockSpec(shape, lambda i: (0,) * len(shape))
    wr_hi = wr.astype(BF16)
    wr_lo = (wr - wr_hi.astype(F32)).astype(BF16)
    return pl.pallas_call(
        kern,
        grid=(t // tm,),
        in_specs=[tok(d), tok(D_ATTN), tok(D_POOL),
                  pl.BlockSpec((HALO, D_POOL), lambda i: (jnp.maximum(i * hb - 1, 0), 0)),
                  full(len(POOL_WINDOWS), POOL_GROUP_DIM, POOL_GROUP_DIM), full(1, D_POOL),
                  full(D_ATTN + D_POOL, d), full(1, d), full(d, LANES), full(d, LANES), full(1, LANES)],
        out_specs=[tok(d), pl.BlockSpec((tm * SUBLANES, LANES), lambda i: (i, 0)), tok(LANES),
                   pl.BlockSpec((SUBLANES, tm), lambda i: (0, i)),
                   full(N_EXPERTS, LANES)],
        out_shape=[jax.ShapeDtypeStruct((t, d), F32), jax.ShapeDtypeStruct((t * SUBLANES, LANES), F32),
                   jax.ShapeDtypeStruct((t, LANES), F32),
                   jax.ShapeDtypeStruct((SUBLANES, t), jnp.int32),
                   jax.ShapeDtypeStruct((N_EXPERTS, LANES), F32)],
        scratch_shapes=[pltpu.VMEM((HALO + tm, D_POOL), F32), pltpu.VMEM((N_EXPERTS, LANES), F32)],
        compiler_params=pltpu.CompilerParams(dimension_semantics=("arbitrary",), vmem_limit_bytes=VMEM_LIMIT),
        name="mix_route",
    )(x2d, a2d, u2d, u2d, w_pool.astype(BF16), pool_scale.reshape(1, D_POOL).astype(F32),
      w_out.astype(BF16), g_ffn.reshape(1, d).astype(F32), wr_hi, wr_lo, br)


ISSUE_UNROLL = 8
COMBINE_CHUNKS = 2
EXPERT_CHUNKS = 2


def _pos_kernel(offs_ref, meta_ref, pos_ref):
    meta = meta_ref[...]
    start = jnp.zeros(meta.shape, jnp.int32)
    for e in range(N_EXPERTS):
        start = jnp.where(meta == e, offs_ref[e], start)
    pos_ref[...] = start[0:2, :] + meta[2:4, :]


def _moe_positions(offs, metat, tm):
    t = metat.shape[1]
    pos = pl.pallas_call(
        _pos_kernel,
        in_specs=[pl.BlockSpec(memory_space=pltpu.SMEM), pl.BlockSpec(memory_space=pltpu.VMEM)],
        out_specs=pl.BlockSpec(memory_space=pltpu.VMEM),
        out_shape=jax.ShapeDtypeStruct((2, t), jnp.int32),
        name="moe_pos",
    )(offs, metat)
    return pos.reshape(2, t // tm, tm).transpose(1, 0, 2).reshape(2 * t)


def _scatter_kernel(pos_ref, h2_ref, xs_ref, sem, *, tm):
    def issue(r, carry):
        for k in range(2):
            _token_copy(h2_ref, r, xs_ref, pos_ref[k * tm + r], sem).start(priority=k)
        return carry

    lax.fori_loop(0, tm, issue, 0, unroll=ISSUE_UNROLL)

    def drain(r, carry):
        for k in range(2):
            _token_copy(h2_ref, 0, xs_ref, 0, sem).wait()
        return carry

    lax.fori_loop(0, tm, drain, 0, unroll=ISSUE_UNROLL)


def _moe_scatter(pos, h2t):
    rows = h2t.shape[0]
    tm = TM
    kern = functools.partial(_scatter_kernel, tm=tm)
    return pl.pallas_call(
        kern,
        grid=(rows // (tm * SUBLANES),),
        in_specs=[pl.BlockSpec((2 * tm,), lambda i: (i,), memory_space=pltpu.SMEM),
                  pl.BlockSpec((tm * SUBLANES, LANES), lambda i: (i, 0))],
        out_specs=pl.BlockSpec(memory_space=pl.ANY),
        out_shape=jax.ShapeDtypeStruct((2 * rows, LANES), F32),
        scratch_shapes=[pltpu.SemaphoreType.DMA(())],
        compiler_params=pltpu.CompilerParams(dimension_semantics=("arbitrary",), vmem_limit_bytes=VMEM_LIMIT),
        name="moe_scatter",
    )(pos, h2t)


def _expert_kernel(tile_ref, exp_ref, lo_ref, hi_ref, xs_ref, wg_ref, wu_ref, wd_ref, ys_ref,
                   wg_s, wu_s, wd_s, *, tm):
    s = pl.program_id(0)
    lo = lo_ref[s]
    hi = hi_ref[s]
    new_expert = jnp.logical_or(s == 0, exp_ref[s] != exp_ref[jnp.maximum(s - 1, 0)])

    @pl.when(jnp.logical_and(hi > 0, new_expert))
    def _round_weights():
        wg_s[...] = wg_ref[...].astype(BF16)
        wu_s[...] = wu_ref[...].astype(BF16)
        wd_s[...] = wd_ref[...].astype(BF16)

    n = tm // EXPERT_CHUNKS

    def expert_rows(first):
        x = _load_token_tiles(xs_ref, n, first).astype(BF16)
        a = _dot(x, wg_s[...])
        u = _dot(x, wu_s[...])
        hid = (a * jax.nn.sigmoid(a) * u).astype(BF16)
        return _dot(hid, wd_s[...])

    for ch in range(EXPERT_CHUNKS):
        first = ch * n
        active = jnp.logical_and(lo < first + n, hi > first)

        @pl.when(jnp.logical_and(active, lo <= first))
        def _whole_chunk():
            _store_token_tiles(ys_ref, expert_rows(first), first)

        @pl.when(jnp.logical_and(active, lo > first))
        def _chunk_tail():
            row = first + lax.broadcasted_iota(jnp.int32, (n, 1), 0)
            kept = _load_token_tiles(ys_ref, n, first)
            _store_token_tiles(ys_ref, jnp.where(row >= lo, expert_rows(first), kept), first)


def _moe_experts(step_tile, step_exp, step_lo, step_hi, xs, w_gate, w_up, w_down):
    n_rows = xs.shape[0]
    tm = TM_EXPERT
    n_e, d, f = w_gate.shape
    n_steps = step_tile.shape[0]
    kern = functools.partial(_expert_kernel, tm=tm)
    grid_spec = pltpu.PrefetchScalarGridSpec(
        num_scalar_prefetch=4,
        grid=(n_steps,),
        in_specs=[pl.BlockSpec((tm * SUBLANES, LANES), lambda s, tl, ex, lo, hi: (tl[s], 0)),
                  pl.BlockSpec((None, d, f), lambda s, tl, ex, lo, hi: (ex[s], 0, 0)),
                  pl.BlockSpec((None, d, f), lambda s, tl, ex, lo, hi: (ex[s], 0, 0)),
                  pl.BlockSpec((None, f, d), lambda s, tl, ex, lo, hi: (ex[s], 0, 0))],
        out_specs=pl.BlockSpec((tm * SUBLANES, LANES), lambda s, tl, ex, lo, hi: (tl[s], 0)),
        scratch_shapes=[pltpu.VMEM((d, f), BF16), pltpu.VMEM((d, f), BF16), pltpu.VMEM((f, d), BF16)],
    )
    return pl.pallas_call(
        kern,
        grid_spec=grid_spec,
        out_shape=jax.ShapeDtypeStruct((n_rows, LANES), F32),
        compiler_params=pltpu.CompilerParams(dimension_semantics=("arbitrary",), vmem_limit_bytes=VMEM_LIMIT),
        name="moe_experts",
    )(step_tile, step_exp, step_lo, step_hi, xs, w_gate, w_up, w_down)


def _expert_steps(counts, n_rows, tm):
    n_e = counts.shape[0]
    n_tiles = n_rows // tm
    n_steps = n_tiles + n_e
    ends = jnp.cumsum(counts)
    starts = ends - counts
    first_tile = starts // tm
    last_tile = jnp.maximum(ends - 1, 0) // tm
    tiles_e = jnp.where(counts > 0, last_tile - first_tile + 1, 0)
    step_end = jnp.cumsum(tiles_e)
    step_start = step_end - tiles_e
    total = step_end[-1]
    s = jnp.arange(n_steps, dtype=jnp.int32)
    s_eff = jnp.minimum(s, total - 1)
    mine = ((step_start[None, :] <= s_eff[:, None]) & (s_eff[:, None] < step_end[None, :])).astype(jnp.int32)
    pick = lambda table: jnp.sum(mine * table[None, :], axis=1)
    exp = pick(jnp.arange(n_e, dtype=jnp.int32))
    tile = pick(first_tile) + (s_eff - pick(step_start))
    lo = jnp.maximum(pick(starts) - tile * tm, 0)
    hi = jnp.minimum(pick(ends) - tile * tm, tm)
    lo = jnp.where(s < total, lo, tm)
    hi = jnp.where(s < total, hi, 0)
    return tile.astype(jnp.int32), exp, lo.astype(jnp.int32), hi.astype(jnp.int32), starts.astype(jnp.int32)


def _combine_kernel(pos_ref, pos_next_ref, x2_ref, gates_ref, p_ref, wproj_ref, wgate_ref, gple_ref, gfin_ref,
                    ys_ref, out_ref, rows_s, sem, *, tm):
    i = pl.program_id(0)
    slot = i % 2

    def gather(p_ref, sl):
        def issue(r, carry):
            for k in range(2):
                _token_copy(ys_ref, p_ref[k * tm + r], rows_s.at[sl, k], r, sem.at[sl]).start(priority=k)
            return carry

        lax.fori_loop(0, tm, issue, 0, unroll=ISSUE_UNROLL)

    @pl.when(i == 0)
    def _first_tile():
        gather(pos_ref, 0)

    @pl.when(i + 1 < pl.num_programs(0))
    def _next_tile():
        gather(pos_next_ref, 1 - slot)

    def drain(r, carry):
        for k in range(2):
            _token_copy(ys_ref, 0, rows_s.at[slot, k], 0, sem.at[slot]).wait()
        return carry

    lax.fori_loop(0, tm, drain, 0, unroll=ISSUE_UNROLL)

    n = tm // COMBINE_CHUNKS
    for ch in range(COMBINE_CHUNKS):
        rows = slice(ch * n, (ch + 1) * n)
        gates = gates_ref[rows, :]
        y = (gates[:, 4:5] * _load_token_tiles(rows_s.at[slot, 0], n, ch * n)
             + gates[:, 5:6] * _load_token_tiles(rows_s.at[slot, 1], n, ch * n))
        x3 = x2_ref[rows, :] + y
        h3 = _rms(x3, gple_ref[...]).astype(BF16)
        gate = jax.nn.sigmoid(_dot(h3, wgate_ref[...]))
        pe = _dot(p_ref[rows, :].astype(BF16), wproj_ref[...])
        x4 = x3 + pe * gate
        out_ref[rows, :] = _rms(x4, gfin_ref[...])


def _combine_ple(pos, x2, meta, p2d, w_ple_proj, w_ple_gate, g_ple, g_final, ys):
    t, d = x2.shape
    d_ple = p2d.shape[1]
    tm = TM
    n_tiles = t // tm
    kern = functools.partial(_combine_kernel, tm=tm)
    tok = lambda w: pl.BlockSpec((tm, w), lambda i: (i, 0))
    full = lambda *shape: pl.BlockSpec(shape, lambda i: (0,) * len(shape))
    return pl.pallas_call(
        kern,
        grid=(t // tm,),
        in_specs=[pl.BlockSpec((2 * tm,), lambda i: (i,), memory_space=pltpu.SMEM),
                  pl.BlockSpec((2 * tm,), lambda i: (jnp.minimum(i + 1, n_tiles - 1),), memory_space=pltpu.SMEM),
                  tok(d), tok(LANES), tok(d_ple), full(d_ple, d), full(d, d), full(1, d), full(1, d),
                  pl.BlockSpec(memory_space=pl.ANY)],
        out_specs=tok(d),
        out_shape=jax.ShapeDtypeStruct((t, d), F32),
        scratch_shapes=[pltpu.VMEM((2, 2, tm * SUBLANES, LANES), F32), pltpu.SemaphoreType.DMA((2,))],
        compiler_params=pltpu.CompilerParams(dimension_semantics=("arbitrary",), vmem_limit_bytes=VMEM_LIMIT),
        name="combine_ple",
    )(pos, pos, x2, meta, p2d, w_ple_proj.astype(BF16), w_ple_gate.astype(BF16),
      g_ple.reshape(1, d).astype(F32), g_final.reshape(1, d).astype(F32), ys)


def _layer(x2d, p2d, seq, g_mix, w_in, w_pool, pool_scale, w_out, near_bias, g_ffn,
           w_r1, b_r1, w_r2, b_r2, w_gate, w_up, w_down, g_ple, w_ple_proj, w_ple_gate, g_final):
    t, d = x2d.shape
    batch = t // seq
    qkv, u = _in_proj(x2d, g_mix, w_in)
    a = _moba_attention(qkv.reshape(batch, seq, 3 * D_ATTN), near_bias).reshape(t, D_ATTN)

    assert N_GROUPS <= SUBLANES and E_PER_GROUP == SUBLANES
    pad_g = SUBLANES - N_GROUPS
    pad_e = LANES - SUBLANES - N_EXPERTS
    wr = jnp.concatenate([w_r1, jnp.zeros((d, pad_g), F32),
                          jnp.transpose(w_r2, (1, 0, 2)).reshape(d, N_EXPERTS),
                          jnp.zeros((d, pad_e), F32)], axis=1).astype(F32)
    br = jnp.concatenate([b_r1, jnp.zeros((pad_g,), F32), b_r2.reshape(N_EXPERTS), jnp.zeros((pad_e,), F32)])
    br = br.astype(F32).reshape(1, LANES)
    assert d == SUBLANES * LANES, "token-tiled rows assume one (8,128) tile per token"
    x2, h2t, meta, metat, cnt = _mix_route(x2d, a, u, w_pool, pool_scale, w_out, g_ffn, wr, br, seq)

    counts = cnt[:, 0].astype(jnp.int32)
    step_tile, step_exp, step_lo, step_hi, offs = _expert_steps(counts, 2 * t, TM_EXPERT)
    pos = _moe_positions(offs, metat, TM)
    xs = _moe_scatter(pos, h2t)
    f = w_gate.shape[-1]
    ys = _moe_experts(step_tile, step_exp, step_lo, step_hi, xs,
                      w_gate.reshape(N_EXPERTS, d, f), w_up.reshape(N_EXPERTS, d, f),
                      w_down.reshape(N_EXPERTS, f, d))
    return _combine_ple(pos, x2, meta, p2d, w_ple_proj, w_ple_gate, g_ple, g_final, ys)


def kernel(x, p, g_mix, w_in, w_pool, pool_scale, w_out, rel_bias, g_ffn, w_r1, b_r1, w_r2, b_r2,
           w_gate, w_up, w_down, g_ple, w_ple_proj, w_ple_gate, g_final):
    batch, seq, d = x.shape
    depth = p.shape[0]
    assert depth == 1, "the final norm is fused into the last stage of a single layer"
    near_bias = _bias_tables(rel_bias, seq)
    out = _layer(x.reshape(batch * seq, d), p[0].reshape(batch * seq, -1), seq,
                 g_mix[0], w_in[0], w_pool[0], pool_scale[0], w_out[0], near_bias, g_ffn[0],
                 w_r1[0], b_r1[0], w_r2[0], b_r2[0], w_gate[0], w_up[0], w_down[0],
                 g_ple[0], w_ple_proj[0], w_ple_gate[0], g_final)
    return out.reshape(batch, seq, d)
```

```python
import functools
import math

import numpy as np
import jax
import jax.numpy as jnp
from jax import lax
from jax.experimental import pallas as pl
from jax.experimental.pallas import tpu as pltpu

F32 = jnp.float32
BF16 = jnp.bfloat16

HEAD_DIM = 64
N_HEADS = 8
D_ATTN = N_HEADS * HEAD_DIM
POOL_WINDOWS = (2, 4, 8, 16)
POOL_GROUP_DIM = 128
D_POOL = POOL_GROUP_DIM * len(POOL_WINDOWS)
MOBA_BLOCK = 256
MOBA_TOPK = 3
REL_BUCKETS = 32
REL_MAX_DIST = 128
N_GROUPS = 4
E_PER_GROUP = 8
N_EXPERTS = N_GROUPS * E_PER_GROUP
RMS_EPS = 1e-6

LANES = 128
SUBLANES = 8
VMEM_LIMIT = 56 * 1024 * 1024

MASK_VALUE = -1e30
LOG2E = math.log2(math.e)
HALO = max(POOL_WINDOWS)
TM_PROJ = 1024
TM = 512
TM_EXPERT = 512
PAIR = 2 * HEAD_DIM


def _rms(x, g):
    return x * lax.rsqrt(jnp.mean(x * x, axis=-1, keepdims=True) + RMS_EPS) * g


def _dot(a, b):
    return jnp.dot(a, b, preferred_element_type=F32)


def _dot_nt(a, b):
    return lax.dot_general(a, b, (((1,), (1,)), ((), ())), preferred_element_type=F32)


TOKEN_ROWS = 4
U32 = jnp.uint32
HIGH_HALF = 0xFFFF0000


def _pack_bf16_pairs(x):
    half = x.shape[1] // 2
    bits = lambda v: lax.bitcast_convert_type(v.astype(BF16).astype(F32), U32)
    return (bits(x[:, :half]) >> 16) | (bits(x[:, half:]) & U32(HIGH_HALF))


def _unpack_bf16_pairs(w):
    low = lax.bitcast_convert_type(w << 16, F32)
    high = lax.bitcast_convert_type(w & U32(HIGH_HALF), F32)
    return jnp.concatenate([low, high], axis=1)


def _store_token_tiles(ref, x, first=0):
    n = x.shape[0]
    w = _pack_bf16_pairs(x)
    for c in range(TOKEN_ROWS):
        ref[pl.ds(first * TOKEN_ROWS + c, n, stride=TOKEN_ROWS), :] = w[:, c * LANES:(c + 1) * LANES]


def _load_token_tiles(ref, n, first=0):
    w = jnp.concatenate([ref[pl.ds(first * TOKEN_ROWS + c, n, stride=TOKEN_ROWS), :] for c in range(TOKEN_ROWS)],
                        axis=1)
    return _unpack_bf16_pairs(w)


def _token_copy(src_ref, src_tok, dst_ref, dst_tok, sem):
    src = src_ref.at[pl.ds(pl.multiple_of(src_tok * TOKEN_ROWS, TOKEN_ROWS), TOKEN_ROWS)]
    dst = dst_ref.at[pl.ds(pl.multiple_of(dst_tok * TOKEN_ROWS, TOKEN_ROWS), TOKEN_ROWS)]
    return pltpu.make_async_copy(src, dst, sem)


def _rel_bucket_np(n):
    n = np.maximum(n, 0)
    max_exact = REL_BUCKETS // 2
    nf = np.maximum(n, 1).astype(np.float32)
    large = max_exact + (np.log(nf / np.float32(max_exact)) / np.float32(math.log(REL_MAX_DIST / max_exact))
                         * np.float32(REL_BUCKETS - max_exact)).astype(np.int32)
    large = np.minimum(large, REL_BUCKETS - 1)
    return np.where(n < max_exact, n, large).astype(np.int32)


def _bucket_tiles(seq):
    r = np.arange(MOBA_BLOCK)
    d_own = r[:, None] - r[None, :]
    own = np.where(d_own >= 0, _rel_bucket_np(d_own), -1).astype(np.int32)
    prev = _rel_bucket_np(d_own + MOBA_BLOCK)
    far = _rel_bucket_np(np.arange(MOBA_BLOCK + 1, max(seq, MOBA_BLOCK + 2)))
    assert np.all(far == REL_BUCKETS - 1)
    first = np.concatenate([own, np.full_like(own, -1)], axis=1)
    later = np.concatenate([prev, own], axis=1)
    return np.stack([first.T, later.T])


def _bias_kernel(rb_ref, bucket_ref, near_ref):
    h = pl.program_id(0)
    far = rb_ref[REL_BUCKETS - 1, h]
    bucket = bucket_ref[...]
    tile = jnp.where(bucket < 0, MASK_VALUE, 0.0).astype(F32)
    for b in range(REL_BUCKETS):
        tile = jnp.where(bucket == b, (rb_ref[b, h] - far) * LOG2E, tile)
    near_ref[...] = tile


def _bias_tables(rel_bias, seq):
    buckets = _bucket_tiles(seq)
    blk = MOBA_BLOCK
    return pl.pallas_call(
        _bias_kernel,
        grid=(N_HEADS, 2),
        in_specs=[pl.BlockSpec(memory_space=pltpu.SMEM),
                  pl.BlockSpec((None, 2 * blk, blk), lambda h, v: (v, 0, 0))],
        out_specs=pl.BlockSpec((None, None, 2 * blk, blk), lambda h, v: (h, v, 0, 0)),
        out_shape=jax.ShapeDtypeStruct((N_HEADS, 2, 2 * blk, blk), F32),
        name="bias_tables",
    )(rel_bias.astype(F32), jnp.asarray(buckets))


def _inproj_kernel(x_ref, g_ref, w_ref, qkv_ref, u_ref):
    hb = _rms(x_ref[...], g_ref[...]).astype(BF16)
    scale = HEAD_DIM ** -0.5 * LOG2E
    qkv_ref[:, :D_ATTN] = (_dot(hb, w_ref[:, :D_ATTN]) * scale).astype(BF16)
    qkv_ref[:, D_ATTN:2 * D_ATTN] = _dot(hb, w_ref[:, D_ATTN:2 * D_ATTN]).astype(BF16)
    qkv_ref[:, 2 * D_ATTN:] = _dot(hb, w_ref[:, 2 * D_ATTN:3 * D_ATTN]).astype(BF16)
    u_ref[...] = _dot(hb, w_ref[:, 3 * D_ATTN:])


def _in_proj(x2d, g_mix, w_in):
    t, d = x2d.shape
    n_in = w_in.shape[1]
    tm = min(TM_PROJ, t)
    return pl.pallas_call(
        _inproj_kernel,
        grid=(t // tm,),
        in_specs=[pl.BlockSpec((tm, d), lambda i: (i, 0)),
                  pl.BlockSpec((1, d), lambda i: (0, 0)),
                  pl.BlockSpec((d, n_in), lambda i: (0, 0))],
        out_specs=[pl.BlockSpec((tm, 3 * D_ATTN), lambda i: (i, 0)),
                   pl.BlockSpec((tm, D_POOL), lambda i: (i, 0))],
        out_shape=[jax.ShapeDtypeStruct((t, 3 * D_ATTN), BF16),
                   jax.ShapeDtypeStruct((t, D_POOL), F32)],
        compiler_params=pltpu.CompilerParams(dimension_semantics=("parallel",), vmem_limit_bytes=VMEM_LIMIT),
        name="in_proj",
    )(x2d, g_mix.reshape(1, d).astype(F32), w_in.astype(BF16))


FAR_BLOCKS = 2
PAIRS_PER_STEP = 2
HEADS_PER_STEP = 2 * PAIRS_PER_STEP
QBLOCKS_PER_STEP = 2
STREAMS = QBLOCKS_PER_STEP * HEADS_PER_STEP
ONES_ROWS = 16
VT_ROWS = HEAD_DIM + ONES_ROWS


def _attn_kernel(q_ref, k_ref, v_ref, nearb0_ref, nearb1_ref, o_ref,
                 kaug, vt, kmean, qnear, qfar, sbuf, m_s, acc_s, *, nb):
    blk = MOBA_BLOCK
    step = pl.program_id(2)
    pair_lanes = [slice(pp * LANES, (pp + 1) * LANES) for pp in range(PAIRS_PER_STEP)]
    q_rows = [slice(qb * blk, (qb + 1) * blk) for qb in range(QBLOCKS_PER_STEP)]
    near_bias = (nearb0_ref, nearb1_ref)

    @pl.when(step == 0)
    def _build_keys():
        kmean[...] = jnp.zeros(kmean.shape, F32)
        lane_b = lax.broadcasted_iota(jnp.int32, (blk, LANES), 1)

        def body(j, carry):
            rows = pl.ds(pl.multiple_of(j * blk, blk), blk)
            kmean[pl.ds(j, 1), :] = jnp.mean(k_ref[rows, :].astype(F32), axis=0, keepdims=True)
            for pp in range(PAIRS_PER_STEP):
                kaug[pp, rows, :LANES] = k_ref[rows, pair_lanes[pp]]
                kaug[pp, rows, LANES:] = jnp.where(lane_b == j, 1.0, 0.0).astype(BF16)
                v_t = jnp.transpose(v_ref[rows, pair_lanes[pp]].astype(F32))
                for half in range(2):
                    hh = 2 * pp + half
                    vt[hh, j, :HEAD_DIM, :] = v_t[half * HEAD_DIM:(half + 1) * HEAD_DIM, :].astype(BF16)
                    vt[hh, j, HEAD_DIM:, :] = jnp.ones((ONES_ROWS, blk), BF16)
            return carry

        lax.fori_loop(0, nb, body, 0)

    lane = lax.broadcasted_iota(jnp.int32, (blk, LANES), 1)
    nbp = kmean.shape[0]
    blk_id = lax.broadcasted_iota(jnp.int32, (nbp, blk), 0)

    for st in range(STREAMS):
        qb, hh = divmod(st, HEADS_PER_STEP)
        pp, half = divmod(hh, 2)
        i = QBLOCKS_PER_STEP * step + qb
        q = q_ref[q_rows[qb], pair_lanes[pp]]
        in_head = (lane < HEAD_DIM) if half == 0 else (lane >= HEAD_DIM)
        qm = jnp.where(in_head, q, jnp.zeros_like(q))
        kmean_b = kmean[:, pair_lanes[pp]].astype(BF16)
        gate = jnp.where(blk_id < i, _dot_nt(kmean_b, qm), -jnp.inf)
        chosen_t = jnp.zeros((nbp, blk), F32)
        for _ in range(MOBA_TOPK):
            top = jnp.max(gate, axis=0, keepdims=True)
            idx = jnp.min(jnp.where(gate == top, blk_id, nbp), axis=0, keepdims=True)
            hit = blk_id == idx
            chosen_t = jnp.where(hit, 1.0, chosen_t)
            gate = jnp.where(hit, -jnp.inf, gate)
        if nbp < LANES:
            chosen_t = jnp.concatenate([chosen_t, jnp.zeros((LANES - nbp, blk), F32)], axis=0)
        penalty = jnp.where(jnp.transpose(chosen_t) > 0.0, 0.0, MASK_VALUE)
        qnear[st, :, :LANES] = qm
        qnear[st, :, LANES:] = jnp.where(lane < i, penalty, 0.0).astype(BF16)
        qfar[st, :, :LANES] = qm
        qfar[st, :, LANES:] = jnp.where(lane < i - 1, penalty, MASK_VALUE).astype(BF16)

    def values_t(hh, first_block):
        return jnp.concatenate([vt[hh, first_block + k] for k in range(FAR_BLOCKS)], axis=1)

    def fold(st, s_t, v_t, first):
        m_blk = jnp.max(s_t, axis=0, keepdims=True)
        if first:
            m_new = m_blk
        else:
            m_old = m_s[st][0:1]
            m_new = jnp.maximum(m_old, m_blk)
            alpha = jnp.exp2(m_old - m_new)
        p_t = jnp.exp2(s_t - m_new)
        pv = _dot(v_t, p_t.astype(BF16))
        acc_s[st] = pv if first else alpha * acc_s[st] + pv
        m_s[st] = jnp.broadcast_to(m_new, m_s.shape[1:])

    near_blocks = [jnp.maximum(QBLOCKS_PER_STEP * step + qb - 1, 0) for qb in range(QBLOCKS_PER_STEP)]
    near_scores = []
    for st in range(STREAMS):
        qb, hh = divmod(st, HEADS_PER_STEP)
        near = pl.ds(pl.multiple_of(near_blocks[qb] * blk, blk), 2 * blk)
        near_scores.append(_dot_nt(kaug[hh // 2, near, :], qnear[st]) + near_bias[qb][hh])

    chunk = FAR_BLOCKS * blk
    last_chunk = kaug.shape[1] // chunk - 1
    n_far = (QBLOCKS_PER_STEP * step + QBLOCKS_PER_STEP - 1 + FAR_BLOCKS - 2) // FAR_BLOCKS

    def far_scores(c, slot):
        rows = pl.ds(pl.multiple_of(jnp.minimum(c, last_chunk) * chunk, chunk), chunk)
        for st in range(STREAMS):
            sbuf[slot, st] = _dot_nt(kaug[(st % HEADS_PER_STEP) // 2, rows, :], qfar[st])

    def far_fold(c, slot):
        first_block = jnp.minimum(c, last_chunk) * FAR_BLOCKS
        for st in range(STREAMS):
            fold(st, sbuf[slot, st], values_t(st % HEADS_PER_STEP, first_block), False)

    far_scores(0, 0)
    for st in range(STREAMS):
        qb, hh = divmod(st, HEADS_PER_STEP)
        fold(st, near_scores[st], values_t(hh, near_blocks[qb]), True)

    def far_body(cc, carry):
        c = 2 * cc
        far_scores(c + 1, 1)
        far_fold(c, 0)
        far_scores(c + 2, 0)
        far_fold(c + 1, 1)
        return carry

    lax.fori_loop(0, (n_far + 1) // 2, far_body, 0)

    for qb in range(QBLOCKS_PER_STEP):
        for pp in range(PAIRS_PER_STEP):
            outs = []
            for hh in (2 * pp, 2 * pp + 1):
                acc = acc_s[qb * HEADS_PER_STEP + hh]
                outs.append(acc[:HEAD_DIM] / acc[HEAD_DIM:HEAD_DIM + 1])
            o_ref[q_rows[qb], pair_lanes[pp]] = jnp.transpose(jnp.concatenate(outs, axis=0)).astype(o_ref.dtype)


def _moba_attention(qkv, near_bias):
    b, s, _ = qkv.shape
    blk = MOBA_BLOCK
    nb = s // blk
    nbp = -(-nb // SUBLANES) * SUBLANES
    width = PAIRS_PER_STEP * PAIR
    n_groups = D_ATTN // width
    qrows = QBLOCKS_PER_STEP * blk
    assert FAR_BLOCKS == 2 and QBLOCKS_PER_STEP == 2, "near / far chunk bookkeeping assumes pairs of blocks"
    assert s % (FAR_BLOCKS * blk) == 0 and s % qrows == 0 and nbp <= LANES and D_ATTN % width == 0
    max_far = (nb - 1 + FAR_BLOCKS - 2) // FAR_BLOCKS
    assert max_far % 2 == 0 or max_far <= nb // FAR_BLOCKS - 1
    kern = functools.partial(_attn_kernel, nb=nb)
    hps = HEADS_PER_STEP
    bias_tile = (hps, None, 2 * blk, blk)
    return pl.pallas_call(
        kern,
        grid=(b, n_groups, nb // QBLOCKS_PER_STEP),
        in_specs=[pl.BlockSpec((None, qrows, width), lambda bi, g, i: (bi, i, g)),
                  pl.BlockSpec((None, s, width), lambda bi, g, i: (bi, 0, n_groups + g)),
                  pl.BlockSpec((None, s, width), lambda bi, g, i: (bi, 0, 2 * n_groups + g)),
                  pl.BlockSpec(bias_tile, lambda bi, g, i: (g, jnp.minimum(i, 1), 0, 0)),
                  pl.BlockSpec(bias_tile, lambda bi, g, i: (g, 1, 0, 0))],
        out_specs=pl.BlockSpec((None, qrows, width), lambda bi, g, i: (bi, i, g)),
        out_shape=jax.ShapeDtypeStruct((b, s, D_ATTN), BF16),
        scratch_shapes=[pltpu.VMEM((PAIRS_PER_STEP, s, 2 * LANES), BF16),
                        pltpu.VMEM((hps, nb, VT_ROWS, blk), BF16),
                        pltpu.VMEM((nbp, width), F32),
                        pltpu.VMEM((STREAMS, blk, 2 * LANES), BF16),
                        pltpu.VMEM((STREAMS, blk, 2 * LANES), BF16),
                        pltpu.VMEM((2, STREAMS, FAR_BLOCKS * blk, blk), F32),
                        pltpu.VMEM((STREAMS, SUBLANES, blk), F32),
                        pltpu.VMEM((STREAMS, VT_ROWS, blk), F32)],
        compiler_params=pltpu.CompilerParams(
            dimension_semantics=("parallel", "parallel", "arbitrary"), vmem_limit_bytes=VMEM_LIMIT),
        name="moba_attn",
    )(qkv, qkv, qkv, near_bias, near_bias)


def _mix_kernel(x_ref, a_ref, u_ref, uh_ref, wpool_ref, pscale_ref, wout_ref, gffn_ref,
                wr_hi_ref, wr_lo_ref, br_ref,
                x2_ref, h2_ref, meta_ref, metat_ref, cnt_ref, ext_s, carry_s, *, tm, seq):
    t = pl.program_id(0)

    @pl.when(t == 0)
    def _init():
        carry_s[...] = jnp.zeros(carry_s.shape, F32)

    pos0 = (t * tm) % seq
    ext_s[0:HALO, :] = jnp.where(pos0 == 0, 0.0, uh_ref[...])
    ext_s[HALO:HALO + tm, :] = u_ref[...]
    pos = pos0 + lax.broadcasted_iota(jnp.int32, (tm, 1), 0)
    parts = []
    for gi, w in enumerate(POOL_WINDOWS):
        c0 = gi * POOL_GROUP_DIM
        cols = slice(c0, c0 + POOL_GROUP_DIM)
        tok = ext_s[HALO:HALO + tm, cols]
        win = tok
        for d in range(1, w):
            win = win + ext_s[HALO - d:HALO - d + tm, cols]
        cnt = jnp.minimum(pos + 1, w).astype(F32)
        pooled = win / cnt - tok
        parts.append(_dot(pooled.astype(BF16), wpool_ref[gi]) * pscale_ref[:, cols])
    b = jnp.concatenate(parts, axis=1).astype(BF16)

    mix = _dot(a_ref[...], wout_ref[:D_ATTN, :]) + _dot(b, wout_ref[D_ATTN:, :])
    x2 = x_ref[...] + mix
    x2_ref[...] = x2
    h2 = _rms(x2, gffn_ref[...])
    _store_token_tiles(h2_ref, h2)

    hi = h2.astype(BF16)
    lo = (h2 - hi.astype(F32)).astype(BF16)
    logits = jnp.transpose(_dot(hi, wr_hi_ref[...]) + _dot(lo, wr_hi_ref[...]) + _dot(hi, wr_lo_ref[...])
                           + br_ref[...])
    row = lax.broadcasted_iota(jnp.int32, (SUBLANES, tm), 0)
    lg = jnp.where(row < N_GROUPS, logits[0:SUBLANES], -jnp.inf)
    top_g = jnp.max(lg, axis=0, keepdims=True)
    g_idx = jnp.min(jnp.where(lg == top_g, row, SUBLANES), axis=0, keepdims=True)
    p_g = 1.0 / jnp.sum(jnp.exp(lg - top_g), axis=0, keepdims=True)
    l2 = logits[SUBLANES:2 * SUBLANES]
    for g in range(1, N_GROUPS):
        l2 = jnp.where(g_idx == g, logits[(g + 1) * SUBLANES:(g + 2) * SUBLANES], l2)
    v1 = jnp.max(l2, axis=0, keepdims=True)
    i1 = jnp.min(jnp.where(l2 == v1, row, SUBLANES), axis=0, keepdims=True)
    l2 = jnp.where(row == i1, -jnp.inf, l2)
    v2 = jnp.max(l2, axis=0, keepdims=True)
    i2 = jnp.min(jnp.where(l2 == v2, row, SUBLANES), axis=0, keepdims=True)
    e1 = g_idx * E_PER_GROUP + i1
    e2 = g_idx * E_PER_GROUP + i2
    t2 = jnp.exp(v2 - v1)
    w1 = 1.0 / (1.0 + t2)
    gate1 = p_g * w1
    gate2 = p_g * (t2 * w1)

    e_row = lax.broadcasted_iota(jnp.int32, (N_EXPERTS, tm), 0)
    oh1 = jnp.where(e_row == e1, 1.0, 0.0)
    oh2 = jnp.where(e_row == e2, 1.0, 0.0)
    oh = oh1 + oh2
    src = lax.broadcasted_iota(jnp.int32, (tm, tm), 0)
    dst = lax.broadcasted_iota(jnp.int32, (tm, tm), 1)
    earlier = jnp.where(src < dst, 1.0, 0.0).astype(BF16)
    seen = _dot(oh.astype(BF16), earlier) + carry_s[:, 0:1]
    r1 = jnp.sum(oh1 * seen, axis=0, keepdims=True)
    r2 = jnp.sum(oh2 * seen, axis=0, keepdims=True)
    carry_s[...] = carry_s[...] + jnp.sum(oh, axis=1, keepdims=True)
    cnt_ref[...] = carry_s[...]

    rows = (e1.astype(F32), e2.astype(F32), r1, r2, gate1, gate2)
    meta_t = jnp.zeros((SUBLANES, tm), F32)
    for k, val in enumerate(rows):
        meta_t = jnp.where(row == k, val, meta_t)
    metat_ref[...] = meta_t.astype(jnp.int32)
    meta_ref[...] = jnp.transpose(jnp.concatenate([meta_t, jnp.zeros((LANES - SUBLANES, tm), F32)], axis=0))


def _mix_route(x2d, a2d, u2d, w_pool, pool_scale, w_out, g_ffn, wr, br, seq):
    t, d = x2d.shape
    tm = TM
    assert seq % tm == 0 and tm % HALO == 0
    hb = tm // HALO
    kern = functools.partial(_mix_kernel, tm=tm, seq=seq)
    tok = lambda w: pl.BlockSpec((tm, w), lambda i: (i, 0))
    full = lambda *shape: pl.BlockSpec(shape, lambda i: (0,) * len(shape))
    wr_hi = wr.astype(BF16)
    wr_lo = (wr - wr_hi.astype(F32)).astype(BF16)
    return pl.pallas_call(
        kern,
        grid=(t // tm,),
        in_specs=[tok(d), tok(D_ATTN), tok(D_POOL),
                  pl.BlockSpec((HALO, D_POOL), lambda i: (jnp.maximum(i * hb - 1, 0), 0)),
                  full(len(POOL_WINDOWS), POOL_GROUP_DIM, POOL_GROUP_DIM), full(1, D_POOL),
                  full(D_ATTN + D_POOL, d), full(1, d), full(d, LANES), full(d, LANES), full(1, LANES)],
        out_specs=[tok(d), pl.BlockSpec((tm * TOKEN_ROWS, LANES), lambda i: (i, 0)), tok(LANES),
                   pl.BlockSpec((SUBLANES, tm), lambda i: (0, i)),
                   full(N_EXPERTS, LANES)],
        out_shape=[jax.ShapeDtypeStruct((t, d), F32), jax.ShapeDtypeStruct((t * TOKEN_ROWS, LANES), U32),
                   jax.ShapeDtypeStruct((t, LANES), F32),
                   jax.ShapeDtypeStruct((SUBLANES, t), jnp.int32),
                   jax.ShapeDtypeStruct((N_EXPERTS, LANES), F32)],
        scratch_shapes=[pltpu.VMEM((HALO + tm, D_POOL), F32), pltpu.VMEM((N_EXPERTS, LANES), F32)],
        compiler_params=pltpu.CompilerParams(dimension_semantics=("arbitrary",), vmem_limit_bytes=VMEM_LIMIT),
        name="mix_route",
    )(x2d, a2d, u2d, u2d, w_pool.astype(BF16), pool_scale.reshape(1, D_POOL).astype(F32),
      w_out.astype(BF16), g_ffn.reshape(1, d).astype(F32), wr_hi, wr_lo, br)


ISSUE_UNROLL = 8
COMBINE_CHUNKS = 2
EXPERT_CHUNKS = 2


def _pos_kernel(offs_ref, meta_ref, pos_ref):
    meta = meta_ref[...]
    start = jnp.zeros(meta.shape, jnp.int32)
    for e in range(N_EXPERTS):
        start = jnp.where(meta == e, offs_ref[e], start)
    pos_ref[...] = start[0:2, :] + meta[2:4, :]


def _moe_positions(offs, metat, tm):
    t = metat.shape[1]
    pos = pl.pallas_call(
        _pos_kernel,
        in_specs=[pl.BlockSpec(memory_space=pltpu.SMEM), pl.BlockSpec(memory_space=pltpu.VMEM)],
        out_specs=pl.BlockSpec(memory_space=pltpu.VMEM),
        out_shape=jax.ShapeDtypeStruct((2, t), jnp.int32),
        name="moe_pos",
    )(offs, metat)
    return pos.reshape(2, t // tm, tm).transpose(1, 0, 2).reshape(2 * t)


def _scatter_kernel(pos_ref, h2_ref, xs_ref, sem, *, tm):
    def issue(r, carry):
        for k in range(2):
            _token_copy(h2_ref, r, xs_ref, pos_ref[k * tm + r], sem).start(priority=k)
        return carry

    lax.fori_loop(0, tm, issue, 0, unroll=ISSUE_UNROLL)

    def drain(r, carry):
        for k in range(2):
            _token_copy(h2_ref, 0, xs_ref, 0, sem).wait()
        return carry

    lax.fori_loop(0, tm, drain, 0, unroll=ISSUE_UNROLL)


def _moe_scatter(pos, h2t):
    rows = h2t.shape[0]
    tm = TM
    kern = functools.partial(_scatter_kernel, tm=tm)
    return pl.pallas_call(
        kern,
        grid=(rows // (tm * TOKEN_ROWS),),
        in_specs=[pl.BlockSpec((2 * tm,), lambda i: (i,), memory_space=pltpu.SMEM),
                  pl.BlockSpec((tm * TOKEN_ROWS, LANES), lambda i: (i, 0))],
        out_specs=pl.BlockSpec(memory_space=pl.ANY),
        out_shape=jax.ShapeDtypeStruct((2 * rows, LANES), U32),
        scratch_shapes=[pltpu.SemaphoreType.DMA(())],
        compiler_params=pltpu.CompilerParams(dimension_semantics=("arbitrary",), vmem_limit_bytes=VMEM_LIMIT),
        name="moe_scatter",
    )(pos, h2t)


def _expert_kernel(tile_ref, exp_ref, lo_ref, hi_ref, xs_ref, wg_ref, wu_ref, wd_ref, ys_ref,
                   wg_s, wu_s, wd_s, *, tm):
    s = pl.program_id(0)
    lo = lo_ref[s]
    hi = hi_ref[s]
    new_expert = jnp.logical_or(s == 0, exp_ref[s] != exp_ref[jnp.maximum(s - 1, 0)])

    @pl.when(jnp.logical_and(hi > 0, new_expert))
    def _round_weights():
        wg_s[...] = wg_ref[...].astype(BF16)
        wu_s[...] = wu_ref[...].astype(BF16)
        wd_s[...] = wd_ref[...].astype(BF16)

    n = tm // EXPERT_CHUNKS

    def expert_rows(first):
        x = _load_token_tiles(xs_ref, n, first).astype(BF16)
        a = _dot(x, wg_s[...])
        u = _dot(x, wu_s[...])
        hid = (a * jax.nn.sigmoid(a) * u).astype(BF16)
        return _dot(hid, wd_s[...])

    for ch in range(EXPERT_CHUNKS):
        first = ch * n
        active = jnp.logical_and(lo < first + n, hi > first)

        @pl.when(jnp.logical_and(active, lo <= first))
        def _whole_chunk():
            _store_token_tiles(ys_ref, expert_rows(first), first)

        @pl.when(jnp.logical_and(active, lo > first))
        def _chunk_tail():
            row = first + lax.broadcasted_iota(jnp.int32, (n, 1), 0)
            kept = _load_token_tiles(ys_ref, n, first)
            _store_token_tiles(ys_ref, jnp.where(row >= lo, expert_rows(first), kept), first)


def _moe_experts(step_tile, step_exp, step_lo, step_hi, xs, w_gate, w_up, w_down):
    n_rows = xs.shape[0]
    tm = TM_EXPERT
    n_e, d, f = w_gate.shape
    n_steps = step_tile.shape[0]
    kern = functools.partial(_expert_kernel, tm=tm)
    grid_spec = pltpu.PrefetchScalarGridSpec(
        num_scalar_prefetch=4,
        grid=(n_steps,),
        in_specs=[pl.BlockSpec((tm * TOKEN_ROWS, LANES), lambda s, tl, ex, lo, hi: (tl[s], 0)),
                  pl.BlockSpec((None, d, f), lambda s, tl, ex, lo, hi: (ex[s], 0, 0)),
                  pl.BlockSpec((None, d, f), lambda s, tl, ex, lo, hi: (ex[s], 0, 0)),
                  pl.BlockSpec((None, f, d), lambda s, tl, ex, lo, hi: (ex[s], 0, 0))],
        out_specs=pl.BlockSpec((tm * TOKEN_ROWS, LANES), lambda s, tl, ex, lo, hi: (tl[s], 0)),
        scratch_shapes=[pltpu.VMEM((d, f), BF16), pltpu.VMEM((d, f), BF16), pltpu.VMEM((f, d), BF16)],
    )
    return pl.pallas_call(
        kern,
        grid_spec=grid_spec,
        out_shape=jax.ShapeDtypeStruct((n_rows, LANES), U32),
        compiler_params=pltpu.CompilerParams(dimension_semantics=("arbitrary",), vmem_limit_bytes=VMEM_LIMIT),
        name="moe_experts",
    )(step_tile, step_exp, step_lo, step_hi, xs, w_gate, w_up, w_down)


def _expert_steps(counts, n_rows, tm):
    n_e = counts.shape[0]
    n_tiles = n_rows // tm
    n_steps = n_tiles + n_e
    ends = jnp.cumsum(counts)
    starts = ends - counts
    first_tile = starts // tm
    last_tile = jnp.maximum(ends - 1, 0) // tm
    tiles_e = jnp.where(counts > 0, last_tile - first_tile + 1, 0)
    step_end = jnp.cumsum(tiles_e)
    step_start = step_end - tiles_e
    total = step_end[-1]
    s = jnp.arange(n_steps, dtype=jnp.int32)
    s_eff = jnp.minimum(s, total - 1)
    mine = ((step_start[None, :] <= s_eff[:, None]) & (s_eff[:, None] < step_end[None, :])).astype(jnp.int32)
    pick = lambda table: jnp.sum(mine * table[None, :], axis=1)
    exp = pick(jnp.arange(n_e, dtype=jnp.int32))
    tile = pick(first_tile) + (s_eff - pick(step_start))
    lo = jnp.maximum(pick(starts) - tile * tm, 0)
    hi = jnp.minimum(pick(ends) - tile * tm, tm)
    lo = jnp.where(s < total, lo, tm)
    hi = jnp.where(s < total, hi, 0)
    return tile.astype(jnp.int32), exp, lo.astype(jnp.int32), hi.astype(jnp.int32), starts.astype(jnp.int32)


def _combine_kernel(pos_ref, pos_next_ref, x2_ref, gates_ref, p_ref, wproj_ref, wgate_ref, gple_ref, gfin_ref,
                    ys_ref, out_ref, rows_s, sem, *, tm):
    i = pl.program_id(0)
    slot = i % 2

    def gather(p_ref, sl):
        def issue(r, carry):
            for k in range(2):
                _token_copy(ys_ref, p_ref[k * tm + r], rows_s.at[sl, k], r, sem.at[sl]).start(priority=k)
            return carry

        lax.fori_loop(0, tm, issue, 0, unroll=ISSUE_UNROLL)

    @pl.when(i == 0)
    def _first_tile():
        gather(pos_ref, 0)

    @pl.when(i + 1 < pl.num_programs(0))
    def _next_tile():
        gather(pos_next_ref, 1 - slot)

    def drain(r, carry):
        for k in range(2):
            _token_copy(ys_ref, 0, rows_s.at[slot, k], 0, sem.at[slot]).wait()
        return carry

    lax.fori_loop(0, tm, drain, 0, unroll=ISSUE_UNROLL)

    n = tm // COMBINE_CHUNKS
    for ch in range(COMBINE_CHUNKS):
        rows = slice(ch * n, (ch + 1) * n)
        gates = gates_ref[rows, :]
        y = (gates[:, 4:5] * _load_token_tiles(rows_s.at[slot, 0], n, ch * n)
             + gates[:, 5:6] * _load_token_tiles(rows_s.at[slot, 1], n, ch * n))
        x3 = x2_ref[rows, :] + y
        h3 = _rms(x3, gple_ref[...]).astype(BF16)
        gate = jax.nn.sigmoid(_dot(h3, wgate_ref[...]))
        pe = _dot(p_ref[rows, :].astype(BF16), wproj_ref[...])
        x4 = x3 + pe * gate
        out_ref[rows, :] = _rms(x4, gfin_ref[...])


def _combine_ple(pos, x2, meta, p2d, w_ple_proj, w_ple_gate, g_ple, g_final, ys):
    t, d = x2.shape
    d_ple = p2d.shape[1]
    tm = TM
    n_tiles = t // tm
    kern = functools.partial(_combine_kernel, tm=tm)
    tok = lambda w: pl.BlockSpec((tm, w), lambda i: (i, 0))
    full = lambda *shape: pl.BlockSpec(shape, lambda i: (0,) * len(shape))
    return pl.pallas_call(
        kern,
        grid=(t // tm,),
        in_specs=[pl.BlockSpec((2 * tm,), lambda i: (i,), memory_space=pltpu.SMEM),
                  pl.BlockSpec((2 * tm,), lambda i: (jnp.minimum(i + 1, n_tiles - 1),), memory_space=pltpu.SMEM),
                  tok(d), tok(LANES), tok(d_ple), full(d_ple, d), full(d, d), full(1, d), full(1, d),
                  pl.BlockSpec(memory_space=pl.ANY)],
        out_specs=tok(d),
        out_shape=jax.ShapeDtypeStruct((t, d), F32),
        scratch_shapes=[pltpu.VMEM((2, 2, tm * TOKEN_ROWS, LANES), U32), pltpu.SemaphoreType.DMA((2,))],
        compiler_params=pltpu.CompilerParams(dimension_semantics=("arbitrary",), vmem_limit_bytes=VMEM_LIMIT),
        name="combine_ple",
    )(pos, pos, x2, meta, p2d, w_ple_proj.astype(BF16), w_ple_gate.astype(BF16),
      g_ple.reshape(1, d).astype(F32), g_final.reshape(1, d).astype(F32), ys)


def _layer(x2d, p2d, seq, g_mix, w_in, w_pool, pool_scale, w_out, near_bias, g_ffn,
           w_r1, b_r1, w_r2, b_r2, w_gate, w_up, w_down, g_ple, w_ple_proj, w_ple_gate, g_final):
    t, d = x2d.shape
    batch = t // seq
    qkv, u = _in_proj(x2d, g_mix, w_in)
    a = _moba_attention(qkv.reshape(batch, seq, 3 * D_ATTN), near_bias).reshape(t, D_ATTN)

    assert N_GROUPS <= SUBLANES and E_PER_GROUP == SUBLANES
    pad_g = SUBLANES - N_GROUPS
    pad_e = LANES - SUBLANES - N_EXPERTS
    wr = jnp.concatenate([w_r1, jnp.zeros((d, pad_g), F32),
                          jnp.transpose(w_r2, (1, 0, 2)).reshape(d, N_EXPERTS),
                          jnp.zeros((d, pad_e), F32)], axis=1).astype(F32)
    br = jnp.concatenate([b_r1, jnp.zeros((pad_g,), F32), b_r2.reshape(N_EXPERTS), jnp.zeros((pad_e,), F32)])
    br = br.astype(F32).reshape(1, LANES)
    assert d == 2 * TOKEN_ROWS * LANES, "token tiles hold d/2 words in TOKEN_ROWS rows of 128"
    x2, h2t, meta, metat, cnt = _mix_route(x2d, a, u, w_pool, pool_scale, w_out, g_ffn, wr, br, seq)

    counts = cnt[:, 0].astype(jnp.int32)
    step_tile, step_exp, step_lo, step_hi, offs = _expert_steps(counts, 2 * t, TM_EXPERT)
    pos = _moe_positions(offs, metat, TM)
    xs = _moe_scatter(pos, h2t)
    f = w_gate.shape[-1]
    ys = _moe_experts(step_tile, step_exp, step_lo, step_hi, xs,
                      w_gate.reshape(N_EXPERTS, d, f), w_up.reshape(N_EXPERTS, d, f),
                      w_down.reshape(N_EXPERTS, f, d))
    return _combine_ple(pos, x2, meta, p2d, w_ple_proj, w_ple_gate, g_ple, g_final, ys)


def kernel(x, p, g_mix, w_in, w_pool, pool_scale, w_out, rel_bias, g_ffn, w_r1, b_r1, w_r2, b_r2,
           w_gate, w_up, w_down, g_ple, w_ple_proj, w_ple_gate, g_final):
    batch, seq, d = x.shape
    depth = p.shape[0]
    assert depth == 1, "the final norm is fused into the last stage of a single layer"
    near_bias = _bias_tables(rel_bias, seq)
    out = _layer(x.reshape(batch * seq, d), p[0].reshape(batch * seq, -1), seq,
                 g_mix[0], w_in[0], w_pool[0], pool_scale[0], w_out[0], near_bias, g_ffn[0],
                 w_r1[0], b_r1[0], w_r2[0], b_r2[0], w_gate[0], w_up[0], w_down[0],
                 g_ple[0], w_ple_proj[0], w_ple_gate[0], g_final)
    return out.reshape(batch, seq, d)
```

```python
import functools
import math

import numpy as np
import jax
import jax.numpy as jnp
from jax import lax
from jax.experimental import pallas as pl
from jax.experimental.pallas import tpu as pltpu

F32 = jnp.float32
BF16 = jnp.bfloat16

HEAD_DIM = 64
N_HEADS = 8
D_ATTN = N_HEADS * HEAD_DIM
POOL_WINDOWS = (2, 4, 8, 16)
POOL_GROUP_DIM = 128
D_POOL = POOL_GROUP_DIM * len(POOL_WINDOWS)
MOBA_BLOCK = 256
MOBA_TOPK = 3
REL_BUCKETS = 32
REL_MAX_DIST = 128
N_GROUPS = 4
E_PER_GROUP = 8
N_EXPERTS = N_GROUPS * E_PER_GROUP
RMS_EPS = 1e-6

LANES = 128
SUBLANES = 8
VMEM_LIMIT = 56 * 1024 * 1024

MASK_VALUE = -1e30
LOG2E = math.log2(math.e)
HALO = max(POOL_WINDOWS)
TM_PROJ = 1024
TM = 512
TM_EXPERT = 512
PAIR = 2 * HEAD_DIM


def _rms(x, g):
    return x * lax.rsqrt(jnp.mean(x * x, axis=-1, keepdims=True) + RMS_EPS) * g


def _dot(a, b):
    return jnp.dot(a, b, preferred_element_type=F32)


def _dot_nt(a, b):
    return lax.dot_general(a, b, (((1,), (1,)), ((), ())), preferred_element_type=F32)


TOKEN_ROWS = 4
U32 = jnp.uint32
HIGH_HALF = 0xFFFF0000


def _pack_bf16_pairs(x):
    half = x.shape[1] // 2
    bits = lambda v: lax.bitcast_convert_type(v.astype(BF16).astype(F32), U32)
    return (bits(x[:, :half]) >> 16) | (bits(x[:, half:]) & U32(HIGH_HALF))


def _unpack_bf16_pairs(w):
    low = lax.bitcast_convert_type(w << 16, F32)
    high = lax.bitcast_convert_type(w & U32(HIGH_HALF), F32)
    return jnp.concatenate([low, high], axis=1)


def _store_token_tiles(ref, x, first=0):
    n = x.shape[0]
    w = _pack_bf16_pairs(x)
    for c in range(TOKEN_ROWS):
        ref[pl.ds(first * TOKEN_ROWS + c, n, stride=TOKEN_ROWS), :] = w[:, c * LANES:(c + 1) * LANES]


def _load_token_tiles(ref, n, first=0):
    w = jnp.concatenate([ref[pl.ds(first * TOKEN_ROWS + c, n, stride=TOKEN_ROWS), :] for c in range(TOKEN_ROWS)],
                        axis=1)
    return _unpack_bf16_pairs(w)


def _token_copy(src_ref, src_tok, dst_ref, dst_tok, sem):
    src = src_ref.at[pl.ds(pl.multiple_of(src_tok * TOKEN_ROWS, TOKEN_ROWS), TOKEN_ROWS)]
    dst = dst_ref.at[pl.ds(pl.multiple_of(dst_tok * TOKEN_ROWS, TOKEN_ROWS), TOKEN_ROWS)]
    return pltpu.make_async_copy(src, dst, sem)


def _rel_bucket_np(n):
    n = np.maximum(n, 0)
    max_exact = REL_BUCKETS // 2
    nf = np.maximum(n, 1).astype(np.float32)
    large = max_exact + (np.log(nf / np.float32(max_exact)) / np.float32(math.log(REL_MAX_DIST / max_exact))
                         * np.float32(REL_BUCKETS - max_exact)).astype(np.int32)
    large = np.minimum(large, REL_BUCKETS - 1)
    return np.where(n < max_exact, n, large).astype(np.int32)


def _bucket_tiles(seq):
    r = np.arange(MOBA_BLOCK)
    d_own = r[:, None] - r[None, :]
    own = np.where(d_own >= 0, _rel_bucket_np(d_own), -1).astype(np.int32)
    prev = _rel_bucket_np(d_own + MOBA_BLOCK)
    far = _rel_bucket_np(np.arange(MOBA_BLOCK + 1, max(seq, MOBA_BLOCK + 2)))
    assert np.all(far == REL_BUCKETS - 1)
    first = np.concatenate([own, np.full_like(own, -1)], axis=1)
    later = np.concatenate([prev, own], axis=1)
    return np.stack([first.T, later.T])


def _bias_kernel(rb_ref, bucket_ref, near_ref):
    h = pl.program_id(0)
    far = rb_ref[REL_BUCKETS - 1, h]
    bucket = bucket_ref[...]
    tile = jnp.where(bucket < 0, MASK_VALUE, 0.0).astype(F32)
    for b in range(REL_BUCKETS):
        tile = jnp.where(bucket == b, (rb_ref[b, h] - far) * LOG2E, tile)
    near_ref[...] = tile


def _bias_tables(rel_bias, seq):
    buckets = _bucket_tiles(seq)
    blk = MOBA_BLOCK
    return pl.pallas_call(
        _bias_kernel,
        grid=(N_HEADS, 2),
        in_specs=[pl.BlockSpec(memory_space=pltpu.SMEM),
                  pl.BlockSpec((None, 2 * blk, blk), lambda h, v: (v, 0, 0))],
        out_specs=pl.BlockSpec((None, None, 2 * blk, blk), lambda h, v: (h, v, 0, 0)),
        out_shape=jax.ShapeDtypeStruct((N_HEADS, 2, 2 * blk, blk), F32),
        name="bias_tables",
    )(rel_bias.astype(F32), jnp.asarray(buckets))


def _inproj_kernel(x_ref, g_ref, w_ref, qkv_ref, u_ref):
    hb = _rms(x_ref[...], g_ref[...]).astype(BF16)
    scale = HEAD_DIM ** -0.5 * LOG2E
    qkv_ref[:, :D_ATTN] = (_dot(hb, w_ref[:, :D_ATTN]) * scale).astype(BF16)
    qkv_ref[:, D_ATTN:2 * D_ATTN] = _dot(hb, w_ref[:, D_ATTN:2 * D_ATTN]).astype(BF16)
    qkv_ref[:, 2 * D_ATTN:] = _dot(hb, w_ref[:, 2 * D_ATTN:3 * D_ATTN]).astype(BF16)
    u_ref[...] = _dot(hb, w_ref[:, 3 * D_ATTN:])


def _in_proj(x2d, g_mix, w_in):
    t, d = x2d.shape
    n_in = w_in.shape[1]
    tm = min(TM_PROJ, t)
    return pl.pallas_call(
        _inproj_kernel,
        grid=(t // tm,),
        in_specs=[pl.BlockSpec((tm, d), lambda i: (i, 0)),
                  pl.BlockSpec((1, d), lambda i: (0, 0)),
                  pl.BlockSpec((d, n_in), lambda i: (0, 0))],
        out_specs=[pl.BlockSpec((tm, 3 * D_ATTN), lambda i: (i, 0)),
                   pl.BlockSpec((tm, D_POOL), lambda i: (i, 0))],
        out_shape=[jax.ShapeDtypeStruct((t, 3 * D_ATTN), BF16),
                   jax.ShapeDtypeStruct((t, D_POOL), F32)],
        compiler_params=pltpu.CompilerParams(dimension_semantics=("parallel",), vmem_limit_bytes=VMEM_LIMIT),
        name="in_proj",
    )(x2d, g_mix.reshape(1, d).astype(F32), w_in.astype(BF16))


FAR_BLOCKS = 2
PAIRS_PER_STEP = 2
HEADS_PER_STEP = 2 * PAIRS_PER_STEP
QBLOCKS_PER_STEP = 2
STREAMS = QBLOCKS_PER_STEP * HEADS_PER_STEP
ONES_ROWS = 16
VT_ROWS = HEAD_DIM + ONES_ROWS


def _attn_kernel(q_ref, k_ref, v_ref, nearb0_ref, nearb1_ref, o_ref,
                 kaug, vt, kmean, qnear, qfar, sbuf, m_s, acc_s, *, nb):
    blk = MOBA_BLOCK
    step = pl.program_id(2)
    pair_lanes = [slice(pp * LANES, (pp + 1) * LANES) for pp in range(PAIRS_PER_STEP)]
    q_rows = [slice(qb * blk, (qb + 1) * blk) for qb in range(QBLOCKS_PER_STEP)]
    near_bias = (nearb0_ref, nearb1_ref)

    @pl.when(step == 0)
    def _build_keys():
        kmean[...] = jnp.zeros(kmean.shape, F32)
        lane_b = lax.broadcasted_iota(jnp.int32, (blk, LANES), 1)

        def body(j, carry):
            rows = pl.ds(pl.multiple_of(j * blk, blk), blk)
            kmean[pl.ds(j, 1), :] = jnp.mean(k_ref[rows, :].astype(F32), axis=0, keepdims=True)
            for pp in range(PAIRS_PER_STEP):
                kaug[pp, rows, :LANES] = k_ref[rows, pair_lanes[pp]]
                kaug[pp, rows, LANES:] = jnp.where(lane_b == j, 1.0, 0.0).astype(BF16)
                v_t = jnp.transpose(v_ref[rows, pair_lanes[pp]].astype(F32))
                for half in range(2):
                    hh = 2 * pp + half
                    vt[hh, j, :HEAD_DIM, :] = v_t[half * HEAD_DIM:(half + 1) * HEAD_DIM, :].astype(BF16)
                    vt[hh, j, HEAD_DIM:, :] = jnp.ones((ONES_ROWS, blk), BF16)
            return carry

        lax.fori_loop(0, nb, body, 0)

    lane = lax.broadcasted_iota(jnp.int32, (blk, LANES), 1)
    nbp = kmean.shape[0]
    blk_id = lax.broadcasted_iota(jnp.int32, (nbp, blk), 0)

    for st in range(STREAMS):
        qb, hh = divmod(st, HEADS_PER_STEP)
        pp, half = divmod(hh, 2)
        i = QBLOCKS_PER_STEP * step + qb
        q = q_ref[q_rows[qb], pair_lanes[pp]]
        in_head = (lane < HEAD_DIM) if half == 0 else (lane >= HEAD_DIM)
        qm = jnp.where(in_head, q, jnp.zeros_like(q))
        kmean_b = kmean[:, pair_lanes[pp]].astype(BF16)
        gate = jnp.where(blk_id < i, _dot_nt(kmean_b, qm), -jnp.inf)
        chosen_t = jnp.zeros((nbp, blk), F32)
        for _ in range(MOBA_TOPK):
            top = jnp.max(gate, axis=0, keepdims=True)
            idx = jnp.min(jnp.where(gate == top, blk_id, nbp), axis=0, keepdims=True)
            hit = blk_id == idx
            chosen_t = jnp.where(hit, 1.0, chosen_t)
            gate = jnp.where(hit, -jnp.inf, gate)
        if nbp < LANES:
            chosen_t = jnp.concatenate([chosen_t, jnp.zeros((LANES - nbp, blk), F32)], axis=0)
        penalty = jnp.where(jnp.transpose(chosen_t) > 0.0, 0.0, MASK_VALUE)
        qnear[st, :, :LANES] = qm
        qnear[st, :, LANES:] = jnp.where(lane < i, penalty, 0.0).astype(BF16)
        qfar[st, :, :LANES] = qm
        qfar[st, :, LANES:] = jnp.where(lane < i - 1, penalty, MASK_VALUE).astype(BF16)

    def values_t(hh, first_block):
        return jnp.concatenate([vt[hh, first_block + k] for k in range(FAR_BLOCKS)], axis=1)

    def fold(st, s_t, v_t, first):
        m_blk = jnp.max(s_t, axis=0, keepdims=True)
        if first:
            m_new = m_blk
        else:
            m_old = m_s[st][0:1]
            m_new = jnp.maximum(m_old, m_blk)
            alpha = jnp.exp2(m_old - m_new)
        p_t = jnp.exp2(s_t - m_new)
        pv = _dot(v_t, p_t.astype(BF16))
        acc_s[st] = pv if first else alpha * acc_s[st] + pv
        m_s[st] = jnp.broadcast_to(m_new, m_s.shape[1:])

    near_blocks = [jnp.maximum(QBLOCKS_PER_STEP * step + qb - 1, 0) for qb in range(QBLOCKS_PER_STEP)]
    near_scores = []
    for st in range(STREAMS):
        qb, hh = divmod(st, HEADS_PER_STEP)
        near = pl.ds(pl.multiple_of(near_blocks[qb] * blk, blk), 2 * blk)
        near_scores.append(_dot_nt(kaug[hh // 2, near, :], qnear[st]) + near_bias[qb][hh])

    chunk = FAR_BLOCKS * blk
    last_chunk = kaug.shape[1] // chunk - 1
    n_far = (QBLOCKS_PER_STEP * step + QBLOCKS_PER_STEP - 1 + FAR_BLOCKS - 2) // FAR_BLOCKS

    def far_scores(c, slot):
        rows = pl.ds(pl.multiple_of(jnp.minimum(c, last_chunk) * chunk, chunk), chunk)
        for st in range(STREAMS):
            sbuf[slot, st] = _dot_nt(kaug[(st % HEADS_PER_STEP) // 2, rows, :], qfar[st])

    def far_fold(c, slot):
        first_block = jnp.minimum(c, last_chunk) * FAR_BLOCKS
        for st in range(STREAMS):
            fold(st, sbuf[slot, st], values_t(st % HEADS_PER_STEP, first_block), False)

    far_scores(0, 0)
    for st in range(STREAMS):
        qb, hh = divmod(st, HEADS_PER_STEP)
        fold(st, near_scores[st], values_t(hh, near_blocks[qb]), True)

    def far_body(cc, carry):
        c = 2 * cc
        far_scores(c + 1, 1)
        far_fold(c, 0)
        far_scores(c + 2, 0)
        far_fold(c + 1, 1)
        return carry

    lax.fori_loop(0, (n_far + 1) // 2, far_body, 0)

    for qb in range(QBLOCKS_PER_STEP):
        for pp in range(PAIRS_PER_STEP):
            outs = []
            for hh in (2 * pp, 2 * pp + 1):
                acc = acc_s[qb * HEADS_PER_STEP + hh]
                outs.append(acc[:HEAD_DIM] / acc[HEAD_DIM:HEAD_DIM + 1])
            o_ref[q_rows[qb], pair_lanes[pp]] = jnp.transpose(jnp.concatenate(outs, axis=0)).astype(o_ref.dtype)


def _moba_attention(qkv, near_bias):
    b, s, _ = qkv.shape
    blk = MOBA_BLOCK
    nb = s // blk
    nbp = -(-nb // SUBLANES) * SUBLANES
    width = PAIRS_PER_STEP * PAIR
    n_groups = D_ATTN // width
    qrows = QBLOCKS_PER_STEP * blk
    assert FAR_BLOCKS == 2 and QBLOCKS_PER_STEP == 2, "near / far chunk bookkeeping assumes pairs of blocks"
    assert s % (FAR_BLOCKS * blk) == 0 and s % qrows == 0 and nbp <= LANES and D_ATTN % width == 0
    max_far = (nb - 1 + FAR_BLOCKS - 2) // FAR_BLOCKS
    assert max_far % 2 == 0 or max_far <= nb // FAR_BLOCKS - 1
    kern = functools.partial(_attn_kernel, nb=nb)
    hps = HEADS_PER_STEP
    bias_tile = (hps, None, 2 * blk, blk)
    return pl.pallas_call(
        kern,
        grid=(b, n_groups, nb // QBLOCKS_PER_STEP),
        in_specs=[pl.BlockSpec((None, qrows, width), lambda bi, g, i: (bi, i, g)),
                  pl.BlockSpec((None, s, width), lambda bi, g, i: (bi, 0, n_groups + g)),
                  pl.BlockSpec((None, s, width), lambda bi, g, i: (bi, 0, 2 * n_groups + g)),
                  pl.BlockSpec(bias_tile, lambda bi, g, i: (g, jnp.minimum(i, 1), 0, 0)),
                  pl.BlockSpec(bias_tile, lambda bi, g, i: (g, 1, 0, 0))],
        out_specs=pl.BlockSpec((None, qrows, width), lambda bi, g, i: (bi, i, g)),
        out_shape=jax.ShapeDtypeStruct((b, s, D_ATTN), BF16),
        scratch_shapes=[pltpu.VMEM((PAIRS_PER_STEP, s, 2 * LANES), BF16),
                        pltpu.VMEM((hps, nb, VT_ROWS, blk), BF16),
                        pltpu.VMEM((nbp, width), F32),
                        pltpu.VMEM((STREAMS, blk, 2 * LANES), BF16),
                        pltpu.VMEM((STREAMS, blk, 2 * LANES), BF16),
                        pltpu.VMEM((2, STREAMS, FAR_BLOCKS * blk, blk), F32),
                        pltpu.VMEM((STREAMS, SUBLANES, blk), F32),
                        pltpu.VMEM((STREAMS, VT_ROWS, blk), F32)],
        compiler_params=pltpu.CompilerParams(
            dimension_semantics=("parallel", "parallel", "arbitrary"), vmem_limit_bytes=VMEM_LIMIT),
        name="moba_attn",
    )(qkv, qkv, qkv, near_bias, near_bias)


def _mix_kernel(x_ref, a_ref, u_ref, uh_ref, wpool_ref, pscale_ref, wout_ref, gffn_ref,
                wr_hi_ref, wr_lo_ref, br_ref,
                x2_ref, h2_ref, meta_ref, metat_ref, cnt_ref, ext_s, carry_s, *, tm, seq):
    t = pl.program_id(0)

    @pl.when(t == 0)
    def _init():
        carry_s[...] = jnp.zeros(carry_s.shape, F32)

    pos0 = (t * tm) % seq
    ext_s[0:HALO, :] = jnp.where(pos0 == 0, 0.0, uh_ref[...])
    ext_s[HALO:HALO + tm, :] = u_ref[...]
    pos = pos0 + lax.broadcasted_iota(jnp.int32, (tm, 1), 0)
    parts = []
    for gi, w in enumerate(POOL_WINDOWS):
        c0 = gi * POOL_GROUP_DIM
        cols = slice(c0, c0 + POOL_GROUP_DIM)
        tok = ext_s[HALO:HALO + tm, cols]
        win = tok
        for d in range(1, w):
            win = win + ext_s[HALO - d:HALO - d + tm, cols]
        cnt = jnp.minimum(pos + 1, w).astype(F32)
        pooled = win / cnt - tok
        parts.append(_dot(pooled.astype(BF16), wpool_ref[gi]) * pscale_ref[:, cols])
    b = jnp.concatenate(parts, axis=1).astype(BF16)

    mix = _dot(a_ref[...], wout_ref[:D_ATTN, :]) + _dot(b, wout_ref[D_ATTN:, :])
    x2 = x_ref[...] + mix
    x2_ref[...] = x2
    h2 = _rms(x2, gffn_ref[...])
    _store_token_tiles(h2_ref, h2)

    hi = h2.astype(BF16)
    lo = (h2 - hi.astype(F32)).astype(BF16)
    logits = jnp.transpose(_dot(hi, wr_hi_ref[...]) + _dot(lo, wr_hi_ref[...]) + _dot(hi, wr_lo_ref[...])
                           + br_ref[...])
    row = lax.broadcasted_iota(jnp.int32, (SUBLANES, tm), 0)
    lg = jnp.where(row < N_GROUPS, logits[0:SUBLANES], -jnp.inf)
    top_g = jnp.max(lg, axis=0, keepdims=True)
    g_idx = jnp.min(jnp.where(lg == top_g, row, SUBLANES), axis=0, keepdims=True)
    p_g = 1.0 / jnp.sum(jnp.exp(lg - top_g), axis=0, keepdims=True)
    l2 = logits[SUBLANES:2 * SUBLANES]
    for g in range(1, N_GROUPS):
        l2 = jnp.where(g_idx == g, logits[(g + 1) * SUBLANES:(g + 2) * SUBLANES], l2)
    v1 = jnp.max(l2, axis=0, keepdims=True)
    i1 = jnp.min(jnp.where(l2 == v1, row, SUBLANES), axis=0, keepdims=True)
    l2 = jnp.where(row == i1, -jnp.inf, l2)
    v2 = jnp.max(l2, axis=0, keepdims=True)
    i2 = jnp.min(jnp.where(l2 == v2, row, SUBLANES), axis=0, keepdims=True)
    e1 = g_idx * E_PER_GROUP + i1
    e2 = g_idx * E_PER_GROUP + i2
    t2 = jnp.exp(v2 - v1)
    w1 = 1.0 / (1.0 + t2)
    gate1 = p_g * w1
    gate2 = p_g * (t2 * w1)

    e_row = lax.broadcasted_iota(jnp.int32, (N_EXPERTS, tm), 0)
    oh1 = jnp.where(e_row == e1, 1.0, 0.0)
    oh2 = jnp.where(e_row == e2, 1.0, 0.0)
    oh = oh1 + oh2
    src = lax.broadcasted_iota(jnp.int32, (tm, tm), 0)
    dst = lax.broadcasted_iota(jnp.int32, (tm, tm), 1)
    earlier = jnp.where(src < dst, 1.0, 0.0).astype(BF16)
    seen = _dot(oh.astype(BF16), earlier) + carry_s[:, 0:1]
    r1 = jnp.sum(oh1 * seen, axis=0, keepdims=True)
    r2 = jnp.sum(oh2 * seen, axis=0, keepdims=True)
    carry_s[...] = carry_s[...] + jnp.sum(oh, axis=1, keepdims=True)
    cnt_ref[...] = carry_s[...]

    rows = (e1.astype(F32), e2.astype(F32), r1, r2, gate1, gate2)
    meta_t = jnp.zeros((SUBLANES, tm), F32)
    for k, val in enumerate(rows):
        meta_t = jnp.where(row == k, val, meta_t)
    metat_ref[...] = meta_t.astype(jnp.int32)
    meta_ref[...] = jnp.transpose(jnp.concatenate([meta_t, jnp.zeros((LANES - SUBLANES, tm), F32)], axis=0))


def _mix_route(x2d, a2d, u2d, w_pool, pool_scale, w_out, g_ffn, wr, br, seq):
    t, d = x2d.shape
    tm = TM
    assert seq % tm == 0 and tm % HALO == 0
    hb = tm // HALO
    kern = functools.partial(_mix_kernel, tm=tm, seq=seq)
    tok = lambda w: pl.BlockSpec((tm, w), lambda i: (i, 0))
    full = lambda *shape: pl.BlockSpec(shape, lambda i: (0,) * len(shape))
    wr_hi = wr.astype(BF16)
    wr_lo = (wr - wr_hi.astype(F32)).astype(BF16)
    return pl.pallas_call(
        kern,
        grid=(t // tm,),
        in_specs=[tok(d), tok(D_ATTN), tok(D_POOL),
                  pl.BlockSpec((HALO, D_POOL), lambda i: (jnp.maximum(i * hb - 1, 0), 0)),
                  full(len(POOL_WINDOWS), POOL_GROUP_DIM, POOL_GROUP_DIM), full(1, D_POOL),
                  full(D_ATTN + D_POOL, d), full(1, d), full(d, LANES), full(d, LANES), full(1, LANES)],
        out_specs=[tok(d), pl.BlockSpec((tm * TOKEN_ROWS, LANES), lambda i: (i, 0)), tok(LANES),
                   pl.BlockSpec((SUBLANES, tm), lambda i: (0, i)),
                   full(N_EXPERTS, LANES)],
        out_shape=[jax.ShapeDtypeStruct((t, d), F32), jax.ShapeDtypeStruct((t * TOKEN_ROWS, LANES), U32),
                   jax.ShapeDtypeStruct((t, LANES), F32),
                   jax.ShapeDtypeStruct((SUBLANES, t), jnp.int32),
                   jax.ShapeDtypeStruct((N_EXPERTS, LANES), F32)],
        scratch_shapes=[pltpu.VMEM((HALO + tm, D_POOL), F32), pltpu.VMEM((N_EXPERTS, LANES), F32)],
        compiler_params=pltpu.CompilerParams(dimension_semantics=("arbitrary",), vmem_limit_bytes=VMEM_LIMIT),
        name="mix_route",
    )(x2d, a2d, u2d, u2d, w_pool.astype(BF16), pool_scale.reshape(1, D_POOL).astype(F32),
      w_out.astype(BF16), g_ffn.reshape(1, d).astype(F32), wr_hi, wr_lo, br)


ISSUE_UNROLL = 8
COMBINE_CHUNKS = 2
EXPERT_CHUNKS = 2


def _pos_kernel(offs_ref, meta_ref, pos_ref):
    meta = meta_ref[...]
    start = jnp.zeros(meta.shape, jnp.int32)
    for e in range(N_EXPERTS):
        start = jnp.where(meta == e, offs_ref[e], start)
    pos_ref[...] = start[0:2, :] + meta[2:4, :]


def _moe_positions(offs, metat, tm):
    t = metat.shape[1]
    pos = pl.pallas_call(
        _pos_kernel,
        in_specs=[pl.BlockSpec(memory_space=pltpu.SMEM), pl.BlockSpec(memory_space=pltpu.VMEM)],
        out_specs=pl.BlockSpec(memory_space=pltpu.VMEM),
        out_shape=jax.ShapeDtypeStruct((2, t), jnp.int32),
        name="moe_pos",
    )(offs, metat)
    return pos.reshape(2, t // tm, tm).transpose(1, 0, 2).reshape(2 * t)


def _scatter_kernel(pos_ref, h2_ref, xs_ref, sem, *, tm):
    def issue(r, carry):
        for k in range(2):
            _token_copy(h2_ref, r, xs_ref, pos_ref[k * tm + r], sem).start(priority=k)
        return carry

    lax.fori_loop(0, tm, issue, 0, unroll=ISSUE_UNROLL)

    def drain(r, carry):
        for k in range(2):
            _token_copy(h2_ref, 0, xs_ref, 0, sem).wait()
        return carry

    lax.fori_loop(0, tm, drain, 0, unroll=ISSUE_UNROLL)


def _moe_scatter(pos, h2t):
    rows = h2t.shape[0]
    tm = TM
    kern = functools.partial(_scatter_kernel, tm=tm)
    return pl.pallas_call(
        kern,
        grid=(rows // (tm * TOKEN_ROWS),),
        in_specs=[pl.BlockSpec((2 * tm,), lambda i: (i,), memory_space=pltpu.SMEM),
                  pl.BlockSpec((tm * TOKEN_ROWS, LANES), lambda i: (i, 0))],
        out_specs=pl.BlockSpec(memory_space=pl.ANY),
        out_shape=jax.ShapeDtypeStruct((2 * rows, LANES), U32),
        scratch_shapes=[pltpu.SemaphoreType.DMA(())],
        compiler_params=pltpu.CompilerParams(dimension_semantics=("arbitrary",), vmem_limit_bytes=VMEM_LIMIT),
        name="moe_scatter",
    )(pos, h2t)


def _expert_kernel(tile_ref, exp_ref, lo_ref, hi_ref, next_ref, xs_ref, wg_hbm, wu_hbm, wd_hbm, ys_ref,
                   wg_f, wu_f, wd_f, wg_s, wu_s, wd_s, sem, *, tm):
    s = pl.program_id(0)
    lo = lo_ref[s]
    hi = hi_ref[s]
    new_expert = jnp.logical_or(s == 0, exp_ref[s] != exp_ref[jnp.maximum(s - 1, 0)])

    def weight_copies(e):
        return [pltpu.make_async_copy(src.at[e], dst, sem.at[k])
                for k, (src, dst) in enumerate(((wg_hbm, wg_f), (wu_hbm, wu_f), (wd_hbm, wd_f)))]

    @pl.when(s == 0)
    def _first_fetch():
        for cp in weight_copies(exp_ref[0]):
            cp.start()

    @pl.when(jnp.logical_and(hi > 0, new_expert))
    def _round_weights():
        for cp in weight_copies(exp_ref[s]):
            cp.wait()
        wg_s[...] = wg_f[...].astype(BF16)
        wu_s[...] = wu_f[...].astype(BF16)
        wd_s[...] = wd_f[...].astype(BF16)

        @pl.when(next_ref[s] >= 0)
        def _fetch_next():
            for cp in weight_copies(next_ref[s]):
                cp.start()

    n = tm // EXPERT_CHUNKS

    def expert_rows(first):
        x = _load_token_tiles(xs_ref, n, first).astype(BF16)
        a = _dot(x, wg_s[...])
        u = _dot(x, wu_s[...])
        hid = (a * jax.nn.sigmoid(a) * u).astype(BF16)
        return _dot(hid, wd_s[...])

    for ch in range(EXPERT_CHUNKS):
        first = ch * n
        active = jnp.logical_and(lo < first + n, hi > first)

        @pl.when(jnp.logical_and(active, lo <= first))
        def _whole_chunk():
            _store_token_tiles(ys_ref, expert_rows(first), first)

        @pl.when(jnp.logical_and(active, lo > first))
        def _chunk_tail():
            row = first + lax.broadcasted_iota(jnp.int32, (n, 1), 0)
            kept = _load_token_tiles(ys_ref, n, first)
            _store_token_tiles(ys_ref, jnp.where(row >= lo, expert_rows(first), kept), first)


def _moe_experts(step_tile, step_exp, step_lo, step_hi, step_next, xs, w_gate, w_up, w_down):
    n_rows = xs.shape[0]
    tm = TM_EXPERT
    n_e, d, f = w_gate.shape
    n_steps = step_tile.shape[0]
    kern = functools.partial(_expert_kernel, tm=tm)
    rows = lambda s, tl, ex, lo, hi, nx: (tl[s], 0)
    grid_spec = pltpu.PrefetchScalarGridSpec(
        num_scalar_prefetch=5,
        grid=(n_steps,),
        in_specs=[pl.BlockSpec((tm * TOKEN_ROWS, LANES), rows),
                  pl.BlockSpec(memory_space=pl.ANY), pl.BlockSpec(memory_space=pl.ANY),
                  pl.BlockSpec(memory_space=pl.ANY)],
        out_specs=pl.BlockSpec((tm * TOKEN_ROWS, LANES), rows),
        scratch_shapes=[pltpu.VMEM((d, f), F32), pltpu.VMEM((d, f), F32), pltpu.VMEM((f, d), F32),
                        pltpu.VMEM((d, f), BF16), pltpu.VMEM((d, f), BF16), pltpu.VMEM((f, d), BF16),
                        pltpu.SemaphoreType.DMA((3,))],
    )
    return pl.pallas_call(
        kern,
        grid_spec=grid_spec,
        out_shape=jax.ShapeDtypeStruct((n_rows, LANES), U32),
        compiler_params=pltpu.CompilerParams(dimension_semantics=("arbitrary",), vmem_limit_bytes=VMEM_LIMIT),
        name="moe_experts",
    )(step_tile, step_exp, step_lo, step_hi, step_next, xs, w_gate, w_up, w_down)


def _expert_steps(counts, n_rows, tm):
    n_e = counts.shape[0]
    n_tiles = n_rows // tm
    n_steps = n_tiles + n_e
    ends = jnp.cumsum(counts)
    starts = ends - counts
    first_tile = starts // tm
    last_tile = jnp.maximum(ends - 1, 0) // tm
    tiles_e = jnp.where(counts > 0, last_tile - first_tile + 1, 0)
    step_end = jnp.cumsum(tiles_e)
    step_start = step_end - tiles_e
    total = step_end[-1]
    s = jnp.arange(n_steps, dtype=jnp.int32)
    s_eff = jnp.minimum(s, total - 1)
    mine = ((step_start[None, :] <= s_eff[:, None]) & (s_eff[:, None] < step_end[None, :])).astype(jnp.int32)
    pick = lambda table: jnp.sum(mine * table[None, :], axis=1)
    exp = pick(jnp.arange(n_e, dtype=jnp.int32))
    tile = pick(first_tile) + (s_eff - pick(step_start))
    lo = jnp.maximum(pick(starts) - tile * tm, 0)
    hi = jnp.minimum(pick(ends) - tile * tm, tm)
    lo = jnp.where(s < total, lo, tm)
    hi = jnp.where(s < total, hi, 0)
    e_ids = jnp.arange(n_e, dtype=jnp.int32)
    later = (e_ids[None, :] > e_ids[:, None]) & (counts[None, :] > 0)
    next_e = jnp.min(jnp.where(later, e_ids[None, :], n_e), axis=1)
    nxt = pick(jnp.where(next_e < n_e, next_e, -1))
    return (tile.astype(jnp.int32), exp, lo.astype(jnp.int32), hi.astype(jnp.int32), nxt.astype(jnp.int32),
            starts.astype(jnp.int32))


def _combine_kernel(pos_ref, pos_next_ref, x2_ref, gates_ref, p_ref, wproj_ref, wgate_ref, gple_ref, gfin_ref,
                    ys_ref, out_ref, rows_s, sem, *, tm):
    i = pl.program_id(0)
    slot = i % 2

    def gather(p_ref, sl):
        def issue(r, carry):
            for k in range(2):
                _token_copy(ys_ref, p_ref[k * tm + r], rows_s.at[sl, k], r, sem.at[sl]).start(priority=k)
            return carry

        lax.fori_loop(0, tm, issue, 0, unroll=ISSUE_UNROLL)

    @pl.when(i == 0)
    def _first_tile():
        gather(pos_ref, 0)

    @pl.when(i + 1 < pl.num_programs(0))
    def _next_tile():
        gather(pos_next_ref, 1 - slot)

    def drain(r, carry):
        for k in range(2):
            _token_copy(ys_ref, 0, rows_s.at[slot, k], 0, sem.at[slot]).wait()
        return carry

    lax.fori_loop(0, tm, drain, 0, unroll=ISSUE_UNROLL)

    n = tm // COMBINE_CHUNKS
    for ch in range(COMBINE_CHUNKS):
        rows = slice(ch * n, (ch + 1) * n)
        gates = gates_ref[rows, :]
        y = (gates[:, 4:5] * _load_token_tiles(rows_s.at[slot, 0], n, ch * n)
             + gates[:, 5:6] * _load_token_tiles(rows_s.at[slot, 1], n, ch * n))
        x3 = x2_ref[rows, :] + y
        h3 = _rms(x3, gple_ref[...]).astype(BF16)
        gate = jax.nn.sigmoid(_dot(h3, wgate_ref[...]))
        pe = _dot(p_ref[rows, :].astype(BF16), wproj_ref[...])
        x4 = x3 + pe * gate
        out_ref[rows, :] = _rms(x4, gfin_ref[...])


def _combine_ple(pos, x2, meta, p2d, w_ple_proj, w_ple_gate, g_ple, g_final, ys):
    t, d = x2.shape
    d_ple = p2d.shape[1]
    tm = TM
    n_tiles = t // tm
    kern = functools.partial(_combine_kernel, tm=tm)
    tok = lambda w: pl.BlockSpec((tm, w), lambda i: (i, 0))
    full = lambda *shape: pl.BlockSpec(shape, lambda i: (0,) * len(shape))
    return pl.pallas_call(
        kern,
        grid=(t // tm,),
        in_specs=[pl.BlockSpec((2 * tm,), lambda i: (i,), memory_space=pltpu.SMEM),
                  pl.BlockSpec((2 * tm,), lambda i: (jnp.minimum(i + 1, n_tiles - 1),), memory_space=pltpu.SMEM),
                  tok(d), tok(LANES), tok(d_ple), full(d_ple, d), full(d, d), full(1, d), full(1, d),
                  pl.BlockSpec(memory_space=pl.ANY)],
        out_specs=tok(d),
        out_shape=jax.ShapeDtypeStruct((t, d), F32),
        scratch_shapes=[pltpu.VMEM((2, 2, tm * TOKEN_ROWS, LANES), U32), pltpu.SemaphoreType.DMA((2,))],
        compiler_params=pltpu.CompilerParams(dimension_semantics=("arbitrary",), vmem_limit_bytes=VMEM_LIMIT),
        name="combine_ple",
    )(pos, pos, x2, meta, p2d, w_ple_proj.astype(BF16), w_ple_gate.astype(BF16),
      g_ple.reshape(1, d).astype(F32), g_final.reshape(1, d).astype(F32), ys)


def _layer(x2d, p2d, seq, g_mix, w_in, w_pool, pool_scale, w_out, near_bias, g_ffn,
           w_r1, b_r1, w_r2, b_r2, w_gate, w_up, w_down, g_ple, w_ple_proj, w_ple_gate, g_final):
    t, d = x2d.shape
    batch = t // seq
    qkv, u = _in_proj(x2d, g_mix, w_in)
    a = _moba_attention(qkv.reshape(batch, seq, 3 * D_ATTN), near_bias).reshape(t, D_ATTN)

    assert N_GROUPS <= SUBLANES and E_PER_GROUP == SUBLANES
    pad_g = SUBLANES - N_GROUPS
    pad_e = LANES - SUBLANES - N_EXPERTS
    wr = jnp.concatenate([w_r1, jnp.zeros((d, pad_g), F32),
                          jnp.transpose(w_r2, (1, 0, 2)).reshape(d, N_EXPERTS),
                          jnp.zeros((d, pad_e), F32)], axis=1).astype(F32)
    br = jnp.concatenate([b_r1, jnp.zeros((pad_g,), F32), b_r2.reshape(N_EXPERTS), jnp.zeros((pad_e,), F32)])
    br = br.astype(F32).reshape(1, LANES)
    assert d == 2 * TOKEN_ROWS * LANES, "token tiles hold d/2 words in TOKEN_ROWS rows of 128"
    x2, h2t, meta, metat, cnt = _mix_route(x2d, a, u, w_pool, pool_scale, w_out, g_ffn, wr, br, seq)

    counts = cnt[:, 0].astype(jnp.int32)
    step_tile, step_exp, step_lo, step_hi, step_next, offs = _expert_steps(counts, 2 * t, TM_EXPERT)
    pos = _moe_positions(offs, metat, TM)
    xs = _moe_scatter(pos, h2t)
    f = w_gate.shape[-1]
    ys = _moe_experts(step_tile, step_exp, step_lo, step_hi, step_next, xs,
                      w_gate.reshape(N_EXPERTS, d, f), w_up.reshape(N_EXPERTS, d, f),
                      w_down.reshape(N_EXPERTS, f, d))
    return _combine_ple(pos, x2, meta, p2d, w_ple_proj, w_ple_gate, g_ple, g_final, ys)


def kernel(x, p, g_mix, w_in, w_pool, pool_scale, w_out, rel_bias, g_ffn, w_r1, b_r1, w_r2, b_r2,
           w_gate, w_up, w_down, g_ple, w_ple_proj, w_ple_gate, g_final):
    batch, seq, d = x.shape
    depth = p.shape[0]
    assert depth == 1, "the final norm is fused into the last stage of a single layer"
    near_bias = _bias_tables(rel_bias, seq)
    out = _layer(x.reshape(batch * seq, d), p[0].reshape(batch * seq, -1), seq,
                 g_mix[0], w_in[0], w_pool[0], pool_scale[0], w_out[0], near_bias, g_ffn[0],
                 w_r1[0], b_r1[0], w_r2[0], b_r2[0], w_gate[0], w_up[0], w_down[0],
                 g_ple[0], w_ple_proj[0], w_ple_gate[0], g_final)
    return out.reshape(batch, seq, d)
```

```python
import functools
import math

import numpy as np
import jax
import jax.numpy as jnp
from jax import lax
from jax.experimental import pallas as pl
from jax.experimental.pallas import tpu as pltpu

F32 = jnp.float32
BF16 = jnp.bfloat16

HEAD_DIM = 64
N_HEADS = 8
D_ATTN = N_HEADS * HEAD_DIM
POOL_WINDOWS = (2, 4, 8, 16)
POOL_GROUP_DIM = 128
D_POOL = POOL_GROUP_DIM * len(POOL_WINDOWS)
MOBA_BLOCK = 256
MOBA_TOPK = 3
REL_BUCKETS = 32
REL_MAX_DIST = 128
N_GROUPS = 4
E_PER_GROUP = 8
N_EXPERTS = N_GROUPS * E_PER_GROUP
RMS_EPS = 1e-6

LANES = 128
SUBLANES = 8
VMEM_LIMIT = 56 * 1024 * 1024

MASK_VALUE = -1e30
LOG2E = math.log2(math.e)
HALO = max(POOL_WINDOWS)
TM_PROJ = 1024
TM = 512
TM_EXPERT = 512
PAIR = 2 * HEAD_DIM


def _rms(x, g):
    return x * lax.rsqrt(jnp.mean(x * x, axis=-1, keepdims=True) + RMS_EPS) * g


def _dot(a, b):
    return jnp.dot(a, b, preferred_element_type=F32)


def _dot_nt(a, b):
    return lax.dot_general(a, b, (((1,), (1,)), ((), ())), preferred_element_type=F32)


TOKEN_ROWS = 4
U32 = jnp.uint32
HIGH_HALF = 0xFFFF0000


def _pack_bf16_pairs(x):
    half = x.shape[1] // 2
    bits = lambda v: lax.bitcast_convert_type(v.astype(BF16).astype(F32), U32)
    return (bits(x[:, :half]) >> 16) | (bits(x[:, half:]) & U32(HIGH_HALF))


def _unpack_bf16_pairs(w):
    low = lax.bitcast_convert_type(w << 16, F32)
    high = lax.bitcast_convert_type(w & U32(HIGH_HALF), F32)
    return jnp.concatenate([low, high], axis=1)


def _store_token_tiles(ref, x, first=0):
    n = x.shape[0]
    w = _pack_bf16_pairs(x)
    for c in range(TOKEN_ROWS):
        ref[pl.ds(first * TOKEN_ROWS + c, n, stride=TOKEN_ROWS), :] = w[:, c * LANES:(c + 1) * LANES]


def _load_token_tiles(ref, n, first=0):
    w = jnp.concatenate([ref[pl.ds(first * TOKEN_ROWS + c, n, stride=TOKEN_ROWS), :] for c in range(TOKEN_ROWS)],
                        axis=1)
    return _unpack_bf16_pairs(w)


def _token_copy(src_ref, src_tok, dst_ref, dst_tok, sem):
    src = src_ref.at[pl.ds(pl.multiple_of(src_tok * TOKEN_ROWS, TOKEN_ROWS), TOKEN_ROWS)]
    dst = dst_ref.at[pl.ds(pl.multiple_of(dst_tok * TOKEN_ROWS, TOKEN_ROWS), TOKEN_ROWS)]
    return pltpu.make_async_copy(src, dst, sem)


def _rel_bucket_np(n):
    n = np.maximum(n, 0)
    max_exact = REL_BUCKETS // 2
    nf = np.maximum(n, 1).astype(np.float32)
    large = max_exact + (np.log(nf / np.float32(max_exact)) / np.float32(math.log(REL_MAX_DIST / max_exact))
                         * np.float32(REL_BUCKETS - max_exact)).astype(np.int32)
    large = np.minimum(large, REL_BUCKETS - 1)
    return np.where(n < max_exact, n, large).astype(np.int32)


def _bucket_tiles(seq):
    r = np.arange(MOBA_BLOCK)
    d_own = r[:, None] - r[None, :]
    own = np.where(d_own >= 0, _rel_bucket_np(d_own), -1).astype(np.int32)
    prev = _rel_bucket_np(d_own + MOBA_BLOCK)
    far = _rel_bucket_np(np.arange(MOBA_BLOCK + 1, max(seq, MOBA_BLOCK + 2)))
    assert np.all(far == REL_BUCKETS - 1)
    first = np.concatenate([own, np.full_like(own, -1)], axis=1)
    later = np.concatenate([prev, own], axis=1)
    return np.stack([first.T, later.T])


def _bias_kernel(rb_ref, bucket_ref, near_ref):
    h = pl.program_id(0)
    far = rb_ref[REL_BUCKETS - 1, h]
    bucket = bucket_ref[...]
    tile = jnp.where(bucket < 0, MASK_VALUE, 0.0).astype(F32)
    for b in range(REL_BUCKETS):
        tile = jnp.where(bucket == b, (rb_ref[b, h] - far) * LOG2E, tile)
    near_ref[...] = tile


def _bias_tables(rel_bias, seq):
    buckets = _bucket_tiles(seq)
    blk = MOBA_BLOCK
    return pl.pallas_call(
        _bias_kernel,
        grid=(N_HEADS, 2),
        in_specs=[pl.BlockSpec(memory_space=pltpu.SMEM),
                  pl.BlockSpec((None, 2 * blk, blk), lambda h, v: (v, 0, 0))],
        out_specs=pl.BlockSpec((None, None, 2 * blk, blk), lambda h, v: (h, v, 0, 0)),
        out_shape=jax.ShapeDtypeStruct((N_HEADS, 2, 2 * blk, blk), F32),
        name="bias_tables",
    )(rel_bias.astype(F32), jnp.asarray(buckets))


def _inproj_kernel(x_ref, g_ref, w_ref, qkv_ref, u_ref):
    hb = _rms(x_ref[...], g_ref[...]).astype(BF16)
    scale = HEAD_DIM ** -0.5 * LOG2E
    qkv_ref[:, :D_ATTN] = (_dot(hb, w_ref[:, :D_ATTN]) * scale).astype(BF16)
    qkv_ref[:, D_ATTN:2 * D_ATTN] = _dot(hb, w_ref[:, D_ATTN:2 * D_ATTN]).astype(BF16)
    qkv_ref[:, 2 * D_ATTN:] = _dot(hb, w_ref[:, 2 * D_ATTN:3 * D_ATTN]).astype(BF16)
    u_ref[...] = _dot(hb, w_ref[:, 3 * D_ATTN:])


def _in_proj(x2d, g_mix, w_in):
    t, d = x2d.shape
    n_in = w_in.shape[1]
    tm = min(TM_PROJ, t)
    return pl.pallas_call(
        _inproj_kernel,
        grid=(t // tm,),
        in_specs=[pl.BlockSpec((tm, d), lambda i: (i, 0)),
                  pl.BlockSpec((1, d), lambda i: (0, 0)),
                  pl.BlockSpec((d, n_in), lambda i: (0, 0))],
        out_specs=[pl.BlockSpec((tm, 3 * D_ATTN), lambda i: (i, 0)),
                   pl.BlockSpec((tm, D_POOL), lambda i: (i, 0))],
        out_shape=[jax.ShapeDtypeStruct((t, 3 * D_ATTN), BF16),
                   jax.ShapeDtypeStruct((t, D_POOL), F32)],
        compiler_params=pltpu.CompilerParams(dimension_semantics=("parallel",), vmem_limit_bytes=VMEM_LIMIT),
        name="in_proj",
    )(x2d, g_mix.reshape(1, d).astype(F32), w_in.astype(BF16))


FAR_BLOCKS = 2
PAIRS_PER_STEP = 2
HEADS_PER_STEP = 2 * PAIRS_PER_STEP
QBLOCKS_PER_STEP = 2
STREAMS = QBLOCKS_PER_STEP * HEADS_PER_STEP
ONES_ROWS = 16
VT_ROWS = HEAD_DIM + ONES_ROWS


def _attn_kernel(q_ref, k_ref, v_ref, nearb0_ref, nearb1_ref, o_ref,
                 kaug, vt, kmean, qnear, qfar, sbuf, m_s, acc_s, *, nb):
    blk = MOBA_BLOCK
    step = pl.program_id(2)
    pair_lanes = [slice(pp * LANES, (pp + 1) * LANES) for pp in range(PAIRS_PER_STEP)]
    q_rows = [slice(qb * blk, (qb + 1) * blk) for qb in range(QBLOCKS_PER_STEP)]
    near_bias = (nearb0_ref, nearb1_ref)

    @pl.when(step == 0)
    def _build_keys():
        kmean[...] = jnp.zeros(kmean.shape, F32)
        lane_b = lax.broadcasted_iota(jnp.int32, (blk, LANES), 1)

        def body(j, carry):
            rows = pl.ds(pl.multiple_of(j * blk, blk), blk)
            kmean[pl.ds(j, 1), :] = jnp.mean(k_ref[rows, :].astype(F32), axis=0, keepdims=True)
            for pp in range(PAIRS_PER_STEP):
                kaug[pp, rows, :LANES] = k_ref[rows, pair_lanes[pp]]
                kaug[pp, rows, LANES:] = jnp.where(lane_b == j, 1.0, 0.0).astype(BF16)
                v_t = jnp.transpose(v_ref[rows, pair_lanes[pp]].astype(F32))
                for half in range(2):
                    hh = 2 * pp + half
                    vt[hh, j, :HEAD_DIM, :] = v_t[half * HEAD_DIM:(half + 1) * HEAD_DIM, :].astype(BF16)
                    vt[hh, j, HEAD_DIM:, :] = jnp.ones((ONES_ROWS, blk), BF16)
            return carry

        lax.fori_loop(0, nb, body, 0)

    lane = lax.broadcasted_iota(jnp.int32, (blk, LANES), 1)
    nbp = kmean.shape[0]
    blk_id = lax.broadcasted_iota(jnp.int32, (nbp, blk), 0)

    for st in range(STREAMS):
        qb, hh = divmod(st, HEADS_PER_STEP)
        pp, half = divmod(hh, 2)
        i = QBLOCKS_PER_STEP * step + qb
        q = q_ref[q_rows[qb], pair_lanes[pp]]
        in_head = (lane < HEAD_DIM) if half == 0 else (lane >= HEAD_DIM)
        qm = jnp.where(in_head, q, jnp.zeros_like(q))
        kmean_b = kmean[:, pair_lanes[pp]].astype(BF16)
        gate = jnp.where(blk_id < i, _dot_nt(kmean_b, qm), -jnp.inf)
        chosen_t = jnp.zeros((nbp, blk), F32)
        for _ in range(MOBA_TOPK):
            top = jnp.max(gate, axis=0, keepdims=True)
            idx = jnp.min(jnp.where(gate == top, blk_id, nbp), axis=0, keepdims=True)
            hit = blk_id == idx
            chosen_t = jnp.where(hit, 1.0, chosen_t)
            gate = jnp.where(hit, -jnp.inf, gate)
        if nbp < LANES:
            chosen_t = jnp.concatenate([chosen_t, jnp.zeros((LANES - nbp, blk), F32)], axis=0)
        penalty = jnp.where(jnp.transpose(chosen_t) > 0.0, 0.0, MASK_VALUE)
        qnear[st, :, :LANES] = qm
        qnear[st, :, LANES:] = jnp.where(lane < i, penalty, 0.0).astype(BF16)
        qfar[st, :, :LANES] = qm
        qfar[st, :, LANES:] = jnp.where(lane < i - 1, penalty, MASK_VALUE).astype(BF16)

    def values_t(hh, first_block):
        return jnp.concatenate([vt[hh, first_block + k] for k in range(FAR_BLOCKS)], axis=1)

    def fold(st, s_t, v_t, first):
        m_blk = jnp.max(s_t, axis=0, keepdims=True)
        if first:
            m_new = m_blk
        else:
            m_old = m_s[st][0:1]
            m_new = jnp.maximum(m_old, m_blk)
            alpha = jnp.exp2(m_old - m_new)
        p_t = jnp.exp2(s_t - m_new)
        pv = _dot(v_t, p_t.astype(BF16))
        acc_s[st] = pv if first else alpha * acc_s[st] + pv
        m_s[st] = jnp.broadcast_to(m_new, m_s.shape[1:])

    near_blocks = [jnp.maximum(QBLOCKS_PER_STEP * step + qb - 1, 0) for qb in range(QBLOCKS_PER_STEP)]
    near_scores = []
    for st in range(STREAMS):
        qb, hh = divmod(st, HEADS_PER_STEP)
        near = pl.ds(pl.multiple_of(near_blocks[qb] * blk, blk), 2 * blk)
        near_scores.append(_dot_nt(kaug[hh // 2, near, :], qnear[st]) + near_bias[qb][hh])

    chunk = FAR_BLOCKS * blk
    last_chunk = kaug.shape[1] // chunk - 1
    n_far = (QBLOCKS_PER_STEP * step + QBLOCKS_PER_STEP - 1 + FAR_BLOCKS - 2) // FAR_BLOCKS

    def far_scores(c, slot):
        rows = pl.ds(pl.multiple_of(jnp.minimum(c, last_chunk) * chunk, chunk), chunk)
        for st in range(STREAMS):
            sbuf[slot, st] = _dot_nt(kaug[(st % HEADS_PER_STEP) // 2, rows, :], qfar[st])

    def far_fold(c, slot):
        first_block = jnp.minimum(c, last_chunk) * FAR_BLOCKS
        for st in range(STREAMS):
            fold(st, sbuf[slot, st], values_t(st % HEADS_PER_STEP, first_block), False)

    far_scores(0, 0)
    for st in range(STREAMS):
        qb, hh = divmod(st, HEADS_PER_STEP)
        fold(st, near_scores[st], values_t(hh, near_blocks[qb]), True)

    def far_body(cc, carry):
        c = 2 * cc
        far_scores(c + 1, 1)
        far_fold(c, 0)
        far_scores(c + 2, 0)
        far_fold(c + 1, 1)
        return carry

    lax.fori_loop(0, n_far // 2, far_body, 0)

    @pl.when(n_far % 2 == 1)
    def _last_chunk():
        far_fold(n_far - 1, 0)

    for qb in range(QBLOCKS_PER_STEP):
        for pp in range(PAIRS_PER_STEP):
            outs = []
            for hh in (2 * pp, 2 * pp + 1):
                acc = acc_s[qb * HEADS_PER_STEP + hh]
                outs.append(acc[:HEAD_DIM] / acc[HEAD_DIM:HEAD_DIM + 1])
            o_ref[q_rows[qb], pair_lanes[pp]] = jnp.transpose(jnp.concatenate(outs, axis=0)).astype(o_ref.dtype)


def _moba_attention(qkv, near_bias):
    b, s, _ = qkv.shape
    blk = MOBA_BLOCK
    nb = s // blk
    nbp = -(-nb // SUBLANES) * SUBLANES
    width = PAIRS_PER_STEP * PAIR
    n_groups = D_ATTN // width
    qrows = QBLOCKS_PER_STEP * blk
    assert FAR_BLOCKS == 2 and QBLOCKS_PER_STEP == 2, "near / far chunk bookkeeping assumes pairs of blocks"
    assert s % (FAR_BLOCKS * blk) == 0 and s % qrows == 0 and nbp <= LANES and D_ATTN % width == 0
    kern = functools.partial(_attn_kernel, nb=nb)
    hps = HEADS_PER_STEP
    bias_tile = (hps, None, 2 * blk, blk)
    return pl.pallas_call(
        kern,
        grid=(b, n_groups, nb // QBLOCKS_PER_STEP),
        in_specs=[pl.BlockSpec((None, qrows, width), lambda bi, g, i: (bi, i, g)),
                  pl.BlockSpec((None, s, width), lambda bi, g, i: (bi, 0, n_groups + g)),
                  pl.BlockSpec((None, s, width), lambda bi, g, i: (bi, 0, 2 * n_groups + g)),
                  pl.BlockSpec(bias_tile, lambda bi, g, i: (g, jnp.minimum(i, 1), 0, 0)),
                  pl.BlockSpec(bias_tile, lambda bi, g, i: (g, 1, 0, 0))],
        out_specs=pl.BlockSpec((None, qrows, width), lambda bi, g, i: (bi, i, g)),
        out_shape=jax.ShapeDtypeStruct((b, s, D_ATTN), BF16),
        scratch_shapes=[pltpu.VMEM((PAIRS_PER_STEP, s, 2 * LANES), BF16),
                        pltpu.VMEM((hps, nb, VT_ROWS, blk), BF16),
                        pltpu.VMEM((nbp, width), F32),
                        pltpu.VMEM((STREAMS, blk, 2 * LANES), BF16),
                        pltpu.VMEM((STREAMS, blk, 2 * LANES), BF16),
                        pltpu.VMEM((2, STREAMS, FAR_BLOCKS * blk, blk), F32),
                        pltpu.VMEM((STREAMS, SUBLANES, blk), F32),
                        pltpu.VMEM((STREAMS, VT_ROWS, blk), F32)],
        compiler_params=pltpu.CompilerParams(
            dimension_semantics=("parallel", "parallel", "arbitrary"), vmem_limit_bytes=VMEM_LIMIT),
        name="moba_attn",
    )(qkv, qkv, qkv, near_bias, near_bias)


def _mix_kernel(x_ref, a_ref, u_ref, uh_ref, wpool_ref, pscale_ref, wout_ref, gffn_ref,
                wr_hi_ref, wr_lo_ref, br_ref,
                x2_ref, h2_ref, meta_ref, metat_ref, cnt_ref, ext_s, carry_s, *, tm, seq):
    t = pl.program_id(0)

    @pl.when(t == 0)
    def _init():
        carry_s[...] = jnp.zeros(carry_s.shape, F32)

    pos0 = (t * tm) % seq
    ext_s[0:HALO, :] = jnp.where(pos0 == 0, 0.0, uh_ref[...])
    ext_s[HALO:HALO + tm, :] = u_ref[...]
    pos = pos0 + lax.broadcasted_iota(jnp.int32, (tm, 1), 0)
    parts = []
    for gi, w in enumerate(POOL_WINDOWS):
        c0 = gi * POOL_GROUP_DIM
        cols = slice(c0, c0 + POOL_GROUP_DIM)
        tok = ext_s[HALO:HALO + tm, cols]
        win = tok
        for d in range(1, w):
            win = win + ext_s[HALO - d:HALO - d + tm, cols]
        cnt = jnp.minimum(pos + 1, w).astype(F32)
        pooled = win / cnt - tok
        parts.append(_dot(pooled.astype(BF16), wpool_ref[gi]) * pscale_ref[:, cols])
    b = jnp.concatenate(parts, axis=1).astype(BF16)

    mix = _dot(a_ref[...], wout_ref[:D_ATTN, :]) + _dot(b, wout_ref[D_ATTN:, :])
    x2 = x_ref[...] + mix
    x2_ref[...] = x2
    h2 = _rms(x2, gffn_ref[...])
    _store_token_tiles(h2_ref, h2)

    hi = h2.astype(BF16)
    lo = (h2 - hi.astype(F32)).astype(BF16)
    logits = jnp.transpose(_dot(hi, wr_hi_ref[...]) + _dot(lo, wr_hi_ref[...]) + _dot(hi, wr_lo_ref[...])
                           + br_ref[...])
    row = lax.broadcasted_iota(jnp.int32, (SUBLANES, tm), 0)
    lg = jnp.where(row < N_GROUPS, logits[0:SUBLANES], -jnp.inf)
    top_g = jnp.max(lg, axis=0, keepdims=True)
    g_idx = jnp.min(jnp.where(lg == top_g, row, SUBLANES), axis=0, keepdims=True)
    p_g = 1.0 / jnp.sum(jnp.exp(lg - top_g), axis=0, keepdims=True)
    l2 = logits[SUBLANES:2 * SUBLANES]
    for g in range(1, N_GROUPS):
        l2 = jnp.where(g_idx == g, logits[(g + 1) * SUBLANES:(g + 2) * SUBLANES], l2)
    v1 = jnp.max(l2, axis=0, keepdims=True)
    i1 = jnp.min(jnp.where(l2 == v1, row, SUBLANES), axis=0, keepdims=True)
    l2 = jnp.where(row == i1, -jnp.inf, l2)
    v2 = jnp.max(l2, axis=0, keepdims=True)
    i2 = jnp.min(jnp.where(l2 == v2, row, SUBLANES), axis=0, keepdims=True)
    e1 = g_idx * E_PER_GROUP + i1
    e2 = g_idx * E_PER_GROUP + i2
    t2 = jnp.exp(v2 - v1)
    w1 = 1.0 / (1.0 + t2)
    gate1 = p_g * w1
    gate2 = p_g * (t2 * w1)

    e_row = lax.broadcasted_iota(jnp.int32, (N_EXPERTS, tm), 0)
    oh1 = jnp.where(e_row == e1, 1.0, 0.0)
    oh2 = jnp.where(e_row == e2, 1.0, 0.0)
    oh = oh1 + oh2
    src = lax.broadcasted_iota(jnp.int32, (tm, tm), 0)
    dst = lax.broadcasted_iota(jnp.int32, (tm, tm), 1)
    earlier = jnp.where(src < dst, 1.0, 0.0).astype(BF16)
    seen = _dot(oh.astype(BF16), earlier) + carry_s[:, 0:1]
    r1 = jnp.sum(oh1 * seen, axis=0, keepdims=True)
    r2 = jnp.sum(oh2 * seen, axis=0, keepdims=True)
    carry_s[...] = carry_s[...] + jnp.sum(oh, axis=1, keepdims=True)
    cnt_ref[...] = carry_s[...]

    rows = (e1.astype(F32), e2.astype(F32), r1, r2, gate1, gate2)
    meta_t = jnp.zeros((SUBLANES, tm), F32)
    for k, val in enumerate(rows):
        meta_t = jnp.where(row == k, val, meta_t)
    metat_ref[...] = meta_t.astype(jnp.int32)
    meta_ref[...] = jnp.transpose(jnp.concatenate([meta_t, jnp.zeros((LANES - SUBLANES, tm), F32)], axis=0))


def _mix_route(x2d, a2d, u2d, w_pool, pool_scale, w_out, g_ffn, wr, br, seq):
    t, d = x2d.shape
    tm = TM
    assert seq % tm == 0 and tm % HALO == 0
    hb = tm // HALO
    kern = functools.partial(_mix_kernel, tm=tm, seq=seq)
    tok = lambda w: pl.BlockSpec((tm, w), lambda i: (i, 0))
    full = lambda *shape: pl.BlockSpec(shape, lambda i: (0,) * len(shape))
    wr_hi = wr.astype(BF16)
    wr_lo = (wr - wr_hi.astype(F32)).astype(BF16)
    return pl.pallas_call(
        kern,
        grid=(t // tm,),
        in_specs=[tok(d), tok(D_ATTN), tok(D_POOL),
                  pl.BlockSpec((HALO, D_POOL), lambda i: (jnp.maximum(i * hb - 1, 0), 0)),
                  full(len(POOL_WINDOWS), POOL_GROUP_DIM, POOL_GROUP_DIM), full(1, D_POOL),
                  full(D_ATTN + D_POOL, d), full(1, d), full(d, LANES), full(d, LANES), full(1, LANES)],
        out_specs=[tok(d), pl.BlockSpec((tm * TOKEN_ROWS, LANES), lambda i: (i, 0)), tok(LANES),
                   pl.BlockSpec((SUBLANES, tm), lambda i: (0, i)),
                   full(N_EXPERTS, LANES)],
        out_shape=[jax.ShapeDtypeStruct((t, d), F32), jax.ShapeDtypeStruct((t * TOKEN_ROWS, LANES), U32),
                   jax.ShapeDtypeStruct((t, LANES), F32),
                   jax.ShapeDtypeStruct((SUBLANES, t), jnp.int32),
                   jax.ShapeDtypeStruct((N_EXPERTS, LANES), F32)],
        scratch_shapes=[pltpu.VMEM((HALO + tm, D_POOL), F32), pltpu.VMEM((N_EXPERTS, LANES), F32)],
        compiler_params=pltpu.CompilerParams(dimension_semantics=("arbitrary",), vmem_limit_bytes=VMEM_LIMIT),
        name="mix_route",
    )(x2d, a2d, u2d, u2d, w_pool.astype(BF16), pool_scale.reshape(1, D_POOL).astype(F32),
      w_out.astype(BF16), g_ffn.reshape(1, d).astype(F32), wr_hi, wr_lo, br)


ISSUE_UNROLL = 8
COMBINE_CHUNKS = 2
EXPERT_CHUNKS = 2


def _pos_kernel(offs_ref, meta_ref, pos_ref):
    meta = meta_ref[...]
    start = jnp.zeros(meta.shape, jnp.int32)
    for e in range(N_EXPERTS):
        start = jnp.where(meta == e, offs_ref[e], start)
    pos_ref[...] = start[0:2, :] + meta[2:4, :]


def _moe_positions(offs, metat, tm):
    t = metat.shape[1]
    pos = pl.pallas_call(
        _pos_kernel,
        in_specs=[pl.BlockSpec(memory_space=pltpu.SMEM), pl.BlockSpec(memory_space=pltpu.VMEM)],
        out_specs=pl.BlockSpec(memory_space=pltpu.VMEM),
        out_shape=jax.ShapeDtypeStruct((2, t), jnp.int32),
        name="moe_pos",
    )(offs, metat)
    return pos.reshape(2, t // tm, tm).transpose(1, 0, 2).reshape(2 * t)


def _scatter_kernel(pos_ref, h2_ref, xs_ref, sem, *, tm):
    def issue(r, carry):
        for k in range(2):
            _token_copy(h2_ref, r, xs_ref, pos_ref[k * tm + r], sem).start(priority=k)
        return carry

    lax.fori_loop(0, tm, issue, 0, unroll=ISSUE_UNROLL)

    def drain(r, carry):
        for k in range(2):
            _token_copy(h2_ref, 0, xs_ref, 0, sem).wait()
        return carry

    lax.fori_loop(0, tm, drain, 0, unroll=ISSUE_UNROLL)


def _moe_scatter(pos, h2t):
    rows = h2t.shape[0]
    tm = TM
    kern = functools.partial(_scatter_kernel, tm=tm)
    return pl.pallas_call(
        kern,
        grid=(rows // (tm * TOKEN_ROWS),),
        in_specs=[pl.BlockSpec((2 * tm,), lambda i: (i,), memory_space=pltpu.SMEM),
                  pl.BlockSpec((tm * TOKEN_ROWS, LANES), lambda i: (i, 0))],
        out_specs=pl.BlockSpec(memory_space=pl.ANY),
        out_shape=jax.ShapeDtypeStruct((2 * rows, LANES), U32),
        scratch_shapes=[pltpu.SemaphoreType.DMA(())],
        compiler_params=pltpu.CompilerParams(dimension_semantics=("arbitrary",), vmem_limit_bytes=VMEM_LIMIT),
        name="moe_scatter",
    )(pos, h2t)


def _expert_kernel(tile_ref, exp_ref, lo_ref, hi_ref, next_ref, xs_ref, wg_hbm, wu_hbm, wd_hbm, ys_ref,
                   wg_f, wu_f, wd_f, wg_s, wu_s, wd_s, sem, *, tm):
    s = pl.program_id(0)
    lo = lo_ref[s]
    hi = hi_ref[s]
    new_expert = jnp.logical_or(s == 0, exp_ref[s] != exp_ref[jnp.maximum(s - 1, 0)])

    def weight_copies(e):
        return [pltpu.make_async_copy(src.at[e], dst, sem.at[k])
                for k, (src, dst) in enumerate(((wg_hbm, wg_f), (wu_hbm, wu_f), (wd_hbm, wd_f)))]

    @pl.when(s == 0)
    def _first_fetch():
        for cp in weight_copies(exp_ref[0]):
            cp.start()

    @pl.when(jnp.logical_and(hi > 0, new_expert))
    def _round_weights():
        for cp in weight_copies(exp_ref[s]):
            cp.wait()
        wg_s[...] = wg_f[...].astype(BF16)
        wu_s[...] = wu_f[...].astype(BF16)
        wd_s[...] = wd_f[...].astype(BF16)

        @pl.when(next_ref[s] >= 0)
        def _fetch_next():
            for cp in weight_copies(next_ref[s]):
                cp.start()

    n = tm // EXPERT_CHUNKS

    def expert_rows(first):
        x = _load_token_tiles(xs_ref, n, first).astype(BF16)
        a = _dot(x, wg_s[...])
        u = _dot(x, wu_s[...])
        hid = (a * jax.nn.sigmoid(a) * u).astype(BF16)
        return _dot(hid, wd_s[...])

    for ch in range(EXPERT_CHUNKS):
        first = ch * n
        active = jnp.logical_and(lo < first + n, hi > first)

        @pl.when(jnp.logical_and(active, lo <= first))
        def _whole_chunk():
            _store_token_tiles(ys_ref, expert_rows(first), first)

        @pl.when(jnp.logical_and(active, lo > first))
        def _chunk_tail():
            row = first + lax.broadcasted_iota(jnp.int32, (n, 1), 0)
            kept = _load_token_tiles(ys_ref, n, first)
            _store_token_tiles(ys_ref, jnp.where(row >= lo, expert_rows(first), kept), first)


def _moe_experts(step_tile, step_exp, step_lo, step_hi, step_next, xs, w_gate, w_up, w_down):
    n_rows = xs.shape[0]
    tm = TM_EXPERT
    n_e, d, f = w_gate.shape
    n_steps = step_tile.shape[0]
    kern = functools.partial(_expert_kernel, tm=tm)
    rows = lambda s, tl, ex, lo, hi, nx: (tl[s], 0)
    grid_spec = pltpu.PrefetchScalarGridSpec(
        num_scalar_prefetch=5,
        grid=(n_steps,),
        in_specs=[pl.BlockSpec((tm * TOKEN_ROWS, LANES), rows),
                  pl.BlockSpec(memory_space=pl.ANY), pl.BlockSpec(memory_space=pl.ANY),
                  pl.BlockSpec(memory_space=pl.ANY)],
        out_specs=pl.BlockSpec((tm * TOKEN_ROWS, LANES), rows),
        scratch_shapes=[pltpu.VMEM((d, f), F32), pltpu.VMEM((d, f), F32), pltpu.VMEM((f, d), F32),
                        pltpu.VMEM((d, f), BF16), pltpu.VMEM((d, f), BF16), pltpu.VMEM((f, d), BF16),
                        pltpu.SemaphoreType.DMA((3,))],
    )
    return pl.pallas_call(
        kern,
        grid_spec=grid_spec,
        out_shape=jax.ShapeDtypeStruct((n_rows, LANES), U32),
        compiler_params=pltpu.CompilerParams(dimension_semantics=("arbitrary",), vmem_limit_bytes=VMEM_LIMIT),
        name="moe_experts",
    )(step_tile, step_exp, step_lo, step_hi, step_next, xs, w_gate, w_up, w_down)


def _expert_steps(counts, n_rows, tm):
    n_e = counts.shape[0]
    n_tiles = n_rows // tm
    n_steps = n_tiles + n_e
    ends = jnp.cumsum(counts)
    starts = ends - counts
    first_tile = starts // tm
    last_tile = jnp.maximum(ends - 1, 0) // tm
    tiles_e = jnp.where(counts > 0, last_tile - first_tile + 1, 0)
    step_end = jnp.cumsum(tiles_e)
    step_start = step_end - tiles_e
    total = step_end[-1]
    s = jnp.arange(n_steps, dtype=jnp.int32)
    s_eff = jnp.minimum(s, total - 1)
    mine = ((step_start[None, :] <= s_eff[:, None]) & (s_eff[:, None] < step_end[None, :])).astype(jnp.int32)
    pick = lambda table: jnp.sum(mine * table[None, :], axis=1)
    exp = pick(jnp.arange(n_e, dtype=jnp.int32))
    tile = pick(first_tile) + (s_eff - pick(step_start))
    lo = jnp.maximum(pick(starts) - tile * tm, 0)
    hi = jnp.minimum(pick(ends) - tile * tm, tm)
    lo = jnp.where(s < total, lo, tm)
    hi = jnp.where(s < total, hi, 0)
    e_ids = jnp.arange(n_e, dtype=jnp.int32)
    later = (e_ids[None, :] > e_ids[:, None]) & (counts[None, :] > 0)
    next_e = jnp.min(jnp.where(later, e_ids[None, :], n_e), axis=1)
    nxt = pick(jnp.where(next_e < n_e, next_e, -1))
    return (tile.astype(jnp.int32), exp, lo.astype(jnp.int32), hi.astype(jnp.int32), nxt.astype(jnp.int32),
            starts.astype(jnp.int32))


def _combine_kernel(pos_ref, pos_next_ref, x2_ref, gates_ref, p_ref, wproj_ref, wgate_ref, gple_ref, gfin_ref,
                    ys_ref, out_ref, rows_s, sem, *, tm):
    i = pl.program_id(0)
    slot = i % 2

    def gather(p_ref, sl):
        def issue(r, carry):
            for k in range(2):
                _token_copy(ys_ref, p_ref[k * tm + r], rows_s.at[sl, k], r, sem.at[sl]).start(priority=k)
            return carry

        lax.fori_loop(0, tm, issue, 0, unroll=ISSUE_UNROLL)

    @pl.when(i == 0)
    def _first_tile():
        gather(pos_ref, 0)

    @pl.when(i + 1 < pl.num_programs(0))
    def _next_tile():
        gather(pos_next_ref, 1 - slot)

    def drain(r, carry):
        for k in range(2):
            _token_copy(ys_ref, 0, rows_s.at[slot, k], 0, sem.at[slot]).wait()
        return carry

    lax.fori_loop(0, tm, drain, 0, unroll=ISSUE_UNROLL)

    n = tm // COMBINE_CHUNKS
    for ch in range(COMBINE_CHUNKS):
        rows = slice(ch * n, (ch + 1) * n)
        gates = gates_ref[rows, :]
        y = (gates[:, 4:5] * _load_token_tiles(rows_s.at[slot, 0], n, ch * n)
             + gates[:, 5:6] * _load_token_tiles(rows_s.at[slot, 1], n, ch * n))
        x3 = x2_ref[rows, :] + y
        h3 = _rms(x3, gple_ref[...]).astype(BF16)
        gate = jax.nn.sigmoid(_dot(h3, wgate_ref[...]))
        pe = _dot(p_ref[rows, :].astype(BF16), wproj_ref[...])
        x4 = x3 + pe * gate
        out_ref[rows, :] = _rms(x4, gfin_ref[...])


def _combine_ple(pos, x2, meta, p2d, w_ple_proj, w_ple_gate, g_ple, g_final, ys):
    t, d = x2.shape
    d_ple = p2d.shape[1]
    tm = TM
    n_tiles = t // tm
    kern = functools.partial(_combine_kernel, tm=tm)
    tok = lambda w: pl.BlockSpec((tm, w), lambda i: (i, 0))
    full = lambda *shape: pl.BlockSpec(shape, lambda i: (0,) * len(shape))
    return pl.pallas_call(
        kern,
        grid=(t // tm,),
        in_specs=[pl.BlockSpec((2 * tm,), lambda i: (i,), memory_space=pltpu.SMEM),
                  pl.BlockSpec((2 * tm,), lambda i: (jnp.minimum(i + 1, n_tiles - 1),), memory_space=pltpu.SMEM),
                  tok(d), tok(LANES), tok(d_ple), full(d_ple, d), full(d, d), full(1, d), full(1, d),
                  pl.BlockSpec(memory_space=pl.ANY)],
        out_specs=tok(d),
        out_shape=jax.ShapeDtypeStruct((t, d), F32),
        scratch_shapes=[pltpu.VMEM((2, 2, tm * TOKEN_ROWS, LANES), U32), pltpu.SemaphoreType.DMA((2,))],
        compiler_params=pltpu.CompilerParams(dimension_semantics=("arbitrary",), vmem_limit_bytes=VMEM_LIMIT),
        name="combine_ple",
    )(pos, pos, x2, meta, p2d, w_ple_proj.astype(BF16), w_ple_gate.astype(BF16),
      g_ple.reshape(1, d).astype(F32), g_final.reshape(1, d).astype(F32), ys)


def _layer(x2d, p2d, seq, g_mix, w_in, w_pool, pool_scale, w_out, near_bias, g_ffn,
           w_r1, b_r1, w_r2, b_r2, w_gate, w_up, w_down, g_ple, w_ple_proj, w_ple_gate, g_final):
    t, d = x2d.shape
    batch = t // seq
    qkv, u = _in_proj(x2d, g_mix, w_in)
    a = _moba_attention(qkv.reshape(batch, seq, 3 * D_ATTN), near_bias).reshape(t, D_ATTN)

    assert N_GROUPS <= SUBLANES and E_PER_GROUP == SUBLANES
    pad_g = SUBLANES - N_GROUPS
    pad_e = LANES - SUBLANES - N_EXPERTS
    wr = jnp.concatenate([w_r1, jnp.zeros((d, pad_g), F32),
                          jnp.transpose(w_r2, (1, 0, 2)).reshape(d, N_EXPERTS),
                          jnp.zeros((d, pad_e), F32)], axis=1).astype(F32)
    br = jnp.concatenate([b_r1, jnp.zeros((pad_g,), F32), b_r2.reshape(N_EXPERTS), jnp.zeros((pad_e,), F32)])
    br = br.astype(F32).reshape(1, LANES)
    assert d == 2 * TOKEN_ROWS * LANES, "token tiles hold d/2 words in TOKEN_ROWS rows of 128"
    x2, h2t, meta, metat, cnt = _mix_route(x2d, a, u, w_pool, pool_scale, w_out, g_ffn, wr, br, seq)

    counts = cnt[:, 0].astype(jnp.int32)
    step_tile, step_exp, step_lo, step_hi, step_next, offs = _expert_steps(counts, 2 * t, TM_EXPERT)
    pos = _moe_positions(offs, metat, TM)
    xs = _moe_scatter(pos, h2t)
    f = w_gate.shape[-1]
    ys = _moe_experts(step_tile, step_exp, step_lo, step_hi, step_next, xs,
                      w_gate.reshape(N_EXPERTS, d, f), w_up.reshape(N_EXPERTS, d, f),
                      w_down.reshape(N_EXPERTS, f, d))
    return _combine_ple(pos, x2, meta, p2d, w_ple_proj, w_ple_gate, g_ple, g_final, ys)


def kernel(x, p, g_mix, w_in, w_pool, pool_scale, w_out, rel_bias, g_ffn, w_r1, b_r1, w_r2, b_r2,
           w_gate, w_up, w_down, g_ple, w_ple_proj, w_ple_gate, g_final):
    batch, seq, d = x.shape
    depth = p.shape[0]
    assert depth == 1, "the final norm is fused into the last stage of a single layer"
    near_bias = _bias_tables(rel_bias, seq)
    out = _layer(x.reshape(batch * seq, d), p[0].reshape(batch * seq, -1), seq,
                 g_mix[0], w_in[0], w_pool[0], pool_scale[0], w_out[0], near_bias, g_ffn[0],
                 w_r1[0], b_r1[0], w_r2[0], b_r2[0], w_gate[0], w_up[0], w_down[0],
                 g_ple[0], w_ple_proj[0], w_ple_gate[0], g_final)
    return out.reshape(batch, seq, d)
```

```python
import functools
import math

import numpy as np
import jax
import jax.numpy as jnp
from jax import lax
from jax.experimental import pallas as pl
from jax.experimental.pallas import tpu as pltpu

F32 = jnp.float32
BF16 = jnp.bfloat16

HEAD_DIM = 64
N_HEADS = 8
D_ATTN = N_HEADS * HEAD_DIM
POOL_WINDOWS = (2, 4, 8, 16)
POOL_GROUP_DIM = 128
D_POOL = POOL_GROUP_DIM * len(POOL_WINDOWS)
MOBA_BLOCK = 256
MOBA_TOPK = 3
REL_BUCKETS = 32
REL_MAX_DIST = 128
N_GROUPS = 4
E_PER_GROUP = 8
N_EXPERTS = N_GROUPS * E_PER_GROUP
RMS_EPS = 1e-6

LANES = 128
SUBLANES = 8
V7X_VMEM_BYTES = 64 * 1024 * 1024
VMEM_LIMIT = V7X_VMEM_BYTES // 8 * 7

MASK_VALUE = -1e30
LOG2E = math.log2(math.e)
HALO = max(POOL_WINDOWS)
TM_PROJ = 1024
TM = 512
TM_EXPERT = 512
PAIR = 2 * HEAD_DIM


def _rms(x, g):
    return x * lax.rsqrt(jnp.mean(x * x, axis=-1, keepdims=True) + RMS_EPS) * g


def _dot(a, b):
    return jnp.dot(a, b, preferred_element_type=F32)


def _dot_nt(a, b):
    return lax.dot_general(a, b, (((1,), (1,)), ((), ())), preferred_element_type=F32)


TOKEN_ROWS = 4
U32 = jnp.uint32
HIGH_HALF = 0xFFFF0000


def _pack_bf16_pairs(x):
    half = x.shape[1] // 2
    bits = lambda v: lax.bitcast_convert_type(v.astype(BF16).astype(F32), U32)
    return (bits(x[:, :half]) >> 16) | (bits(x[:, half:]) & U32(HIGH_HALF))


def _unpack_bf16_pairs(w):
    low = lax.bitcast_convert_type(w << 16, F32)
    high = lax.bitcast_convert_type(w & U32(HIGH_HALF), F32)
    return jnp.concatenate([low, high], axis=1)


def _store_token_tiles(ref, x, first=0):
    n = x.shape[0]
    w = _pack_bf16_pairs(x)
    for c in range(TOKEN_ROWS):
        ref[pl.ds(first * TOKEN_ROWS + c, n, stride=TOKEN_ROWS), :] = w[:, c * LANES:(c + 1) * LANES]


def _load_token_tiles(ref, n, first=0):
    w = jnp.concatenate([ref[pl.ds(first * TOKEN_ROWS + c, n, stride=TOKEN_ROWS), :] for c in range(TOKEN_ROWS)],
                        axis=1)
    return _unpack_bf16_pairs(w)


def _token_copy(src_ref, src_tok, dst_ref, dst_tok, sem):
    src = src_ref.at[pl.ds(pl.multiple_of(src_tok * TOKEN_ROWS, TOKEN_ROWS), TOKEN_ROWS)]
    dst = dst_ref.at[pl.ds(pl.multiple_of(dst_tok * TOKEN_ROWS, TOKEN_ROWS), TOKEN_ROWS)]
    return pltpu.make_async_copy(src, dst, sem)


def _rel_bucket_np(n):
    n = np.maximum(n, 0)
    max_exact = REL_BUCKETS // 2
    nf = np.maximum(n, 1).astype(np.float32)
    large = max_exact + (np.log(nf / np.float32(max_exact)) / np.float32(math.log(REL_MAX_DIST / max_exact))
                         * np.float32(REL_BUCKETS - max_exact)).astype(np.int32)
    large = np.minimum(large, REL_BUCKETS - 1)
    return np.where(n < max_exact, n, large).astype(np.int32)


def _bucket_tiles(seq):
    r = np.arange(MOBA_BLOCK)
    dist = r[None, :] - r[:, None]
    own = np.where(dist >= 0, _rel_bucket_np(dist), -1).astype(np.int32)
    prev = _rel_bucket_np(dist + MOBA_BLOCK)
    far = _rel_bucket_np(np.arange(MOBA_BLOCK + 1, max(seq, MOBA_BLOCK + 2)))
    assert np.all(far == REL_BUCKETS - 1)
    return np.stack([own, prev])


def _bias_kernel(rb_ref, bucket_ref, near_ref):
    h = pl.program_id(0)
    far = rb_ref[REL_BUCKETS - 1, h]
    tiles = []
    for which in range(2):
        bucket = bucket_ref[which]
        tile = jnp.where(bucket < 0, MASK_VALUE, 0.0).astype(F32)
        for b in range(REL_BUCKETS):
            tile = jnp.where(bucket == b, (rb_ref[b, h] - far) * LOG2E, tile)
        tiles.append(tile)
    own, prev = tiles
    blk = own.shape[0]
    near_ref[0, :blk, :] = own
    near_ref[0, blk:, :] = jnp.full(own.shape, MASK_VALUE, F32)
    near_ref[1, :blk, :] = prev
    near_ref[1, blk:, :] = own


def _bias_tables(rel_bias, seq):
    buckets = _bucket_tiles(seq)
    blk = MOBA_BLOCK
    return pl.pallas_call(
        _bias_kernel,
        grid=(N_HEADS,),
        in_specs=[pl.BlockSpec(memory_space=pltpu.SMEM),
                  pl.BlockSpec((2, blk, blk), lambda h: (0, 0, 0))],
        out_specs=pl.BlockSpec((None, 2, 2 * blk, blk), lambda h: (h, 0, 0, 0)),
        out_shape=jax.ShapeDtypeStruct((N_HEADS, 2, 2 * blk, blk), F32),
        name="bias_tables",
    )(rel_bias.astype(F32), jnp.asarray(buckets))


def _inproj_kernel(x_ref, g_ref, w_ref, qkv_ref, u_ref):
    hb = _rms(x_ref[...], g_ref[...]).astype(BF16)
    scale = HEAD_DIM ** -0.5 * LOG2E
    qkv_ref[:, :D_ATTN] = (_dot(hb, w_ref[:, :D_ATTN]) * scale).astype(BF16)
    qkv_ref[:, D_ATTN:2 * D_ATTN] = _dot(hb, w_ref[:, D_ATTN:2 * D_ATTN]).astype(BF16)
    qkv_ref[:, 2 * D_ATTN:] = _dot(hb, w_ref[:, 2 * D_ATTN:3 * D_ATTN]).astype(BF16)
    u_ref[...] = _dot(hb, w_ref[:, 3 * D_ATTN:])


def _in_proj(x2d, g_mix, w_in):
    t, d = x2d.shape
    n_in = w_in.shape[1]
    tm = min(TM_PROJ, t)
    return pl.pallas_call(
        _inproj_kernel,
        grid=(t // tm,),
        in_specs=[pl.BlockSpec((tm, d), lambda i: (i, 0)),
                  pl.BlockSpec((1, d), lambda i: (0, 0)),
                  pl.BlockSpec((d, n_in), lambda i: (0, 0))],
        out_specs=[pl.BlockSpec((tm, 3 * D_ATTN), lambda i: (i, 0)),
                   pl.BlockSpec((tm, D_POOL), lambda i: (i, 0))],
        out_shape=[jax.ShapeDtypeStruct((t, 3 * D_ATTN), BF16),
                   jax.ShapeDtypeStruct((t, D_POOL), F32)],
        compiler_params=pltpu.CompilerParams(dimension_semantics=("parallel",), vmem_limit_bytes=VMEM_LIMIT),
        name="in_proj",
    )(x2d, g_mix.reshape(1, d).astype(F32), w_in.astype(BF16))


FAR_BLOCKS = 2
PAIRS_PER_STEP = 2
HEADS_PER_STEP = 2 * PAIRS_PER_STEP
QBLOCKS_PER_STEP = 2
STREAMS = QBLOCKS_PER_STEP * HEADS_PER_STEP
ONES_ROWS = 16
VT_ROWS = HEAD_DIM + ONES_ROWS


def _attn_kernel(q_ref, k_ref, v_ref, nearb0_ref, nearb1_ref, o_ref,
                 kaug, vt, kmean, qnear, qfar, sbuf, m_s, acc_s, *, nb):
    blk = MOBA_BLOCK
    step = pl.program_id(2)
    pair_lanes = [slice(pp * LANES, (pp + 1) * LANES) for pp in range(PAIRS_PER_STEP)]
    q_rows = [slice(qb * blk, (qb + 1) * blk) for qb in range(QBLOCKS_PER_STEP)]
    near_bias = (nearb0_ref, nearb1_ref)

    @pl.when(step == 0)
    def _build_keys():
        kmean[...] = jnp.zeros(kmean.shape, F32)
        lane_b = lax.broadcasted_iota(jnp.int32, (blk, LANES), 1)

        def body(j, carry):
            rows = pl.ds(pl.multiple_of(j * blk, blk), blk)
            kmean[pl.ds(j, 1), :] = jnp.mean(k_ref[rows, :].astype(F32), axis=0, keepdims=True)
            for pp in range(PAIRS_PER_STEP):
                kaug[pp, rows, :LANES] = k_ref[rows, pair_lanes[pp]]
                kaug[pp, rows, LANES:] = jnp.where(lane_b == j, 1.0, 0.0).astype(BF16)
                v_t = jnp.transpose(v_ref[rows, pair_lanes[pp]].astype(F32))
                for half in range(2):
                    hh = 2 * pp + half
                    vt[hh, j, :HEAD_DIM, :] = v_t[half * HEAD_DIM:(half + 1) * HEAD_DIM, :].astype(BF16)
                    vt[hh, j, HEAD_DIM:, :] = jnp.ones((ONES_ROWS, blk), BF16)
            return carry

        lax.fori_loop(0, nb, body, 0)

    lane = lax.broadcasted_iota(jnp.int32, (blk, LANES), 1)
    nbp = kmean.shape[0]
    blk_id = lax.broadcasted_iota(jnp.int32, (nbp, blk), 0)

    for st in range(STREAMS):
        qb, hh = divmod(st, HEADS_PER_STEP)
        pp, half = divmod(hh, 2)
        i = QBLOCKS_PER_STEP * step + qb
        q = q_ref[q_rows[qb], pair_lanes[pp]]
        in_head = (lane < HEAD_DIM) if half == 0 else (lane >= HEAD_DIM)
        qm = jnp.where(in_head, q, jnp.zeros_like(q))
        kmean_b = kmean[:, pair_lanes[pp]].astype(BF16)
        gate = jnp.where(blk_id < i, _dot_nt(kmean_b, qm), -jnp.inf)
        chosen_t = jnp.zeros((nbp, blk), F32)
        for _ in range(MOBA_TOPK):
            top = jnp.max(gate, axis=0, keepdims=True)
            idx = jnp.min(jnp.where(gate == top, blk_id, nbp), axis=0, keepdims=True)
            hit = blk_id == idx
            chosen_t = jnp.where(hit, 1.0, chosen_t)
            gate = jnp.where(hit, -jnp.inf, gate)
        if nbp < LANES:
            chosen_t = jnp.concatenate([chosen_t, jnp.zeros((LANES - nbp, blk), F32)], axis=0)
        penalty = jnp.where(jnp.transpose(chosen_t) > 0.0, 0.0, MASK_VALUE)
        qnear[st, :, :LANES] = qm
        qnear[st, :, LANES:] = jnp.where(lane < i, penalty, 0.0).astype(BF16)
        qfar[st, :, :LANES] = qm
        qfar[st, :, LANES:] = jnp.where(lane < i - 1, penalty, MASK_VALUE).astype(BF16)

    def values_t(hh, first_block):
        return jnp.concatenate([vt[hh, first_block + k] for k in range(FAR_BLOCKS)], axis=1)

    def fold(st, s_t, v_t, first):
        m_blk = jnp.max(s_t, axis=0, keepdims=True)
        if first:
            m_new = m_blk
        else:
            m_old = m_s[st][0:1]
            m_new = jnp.maximum(m_old, m_blk)
            alpha = jnp.exp2(m_old - m_new)
        p_t = jnp.exp2(s_t - m_new)
        pv = _dot(v_t, p_t.astype(BF16))
        acc_s[st] = pv if first else alpha * acc_s[st] + pv
        m_s[st] = jnp.broadcast_to(m_new, m_s.shape[1:])

    near_blocks = [jnp.maximum(QBLOCKS_PER_STEP * step + qb - 1, 0) for qb in range(QBLOCKS_PER_STEP)]
    near_scores = []
    for st in range(STREAMS):
        qb, hh = divmod(st, HEADS_PER_STEP)
        near = pl.ds(pl.multiple_of(near_blocks[qb] * blk, blk), 2 * blk)
        near_scores.append(_dot_nt(kaug[hh // 2, near, :], qnear[st]) + near_bias[qb][hh])

    chunk = FAR_BLOCKS * blk
    last_chunk = kaug.shape[1] // chunk - 1
    n_far = (QBLOCKS_PER_STEP * step + QBLOCKS_PER_STEP - 1 + FAR_BLOCKS - 2) // FAR_BLOCKS

    def far_scores(c, slot):
        rows = pl.ds(pl.multiple_of(jnp.minimum(c, last_chunk) * chunk, chunk), chunk)
        for st in range(STREAMS):
            sbuf[slot, st] = _dot_nt(kaug[(st % HEADS_PER_STEP) // 2, rows, :], qfar[st])

    def far_fold(c, slot):
        first_block = jnp.minimum(c, last_chunk) * FAR_BLOCKS
        for st in range(STREAMS):
            fold(st, sbuf[slot, st], values_t(st % HEADS_PER_STEP, first_block), False)

    far_scores(0, 0)
    for st in range(STREAMS):
        qb, hh = divmod(st, HEADS_PER_STEP)
        fold(st, near_scores[st], values_t(hh, near_blocks[qb]), True)

    def far_body(cc, carry):
        c = 2 * cc
        far_scores(c + 1, 1)
        far_fold(c, 0)
        far_scores(c + 2, 0)
        far_fold(c + 1, 1)
        return carry

    lax.fori_loop(0, n_far // 2, far_body, 0)

    @pl.when(n_far % 2 == 1)
    def _last_chunk():
        far_fold(n_far - 1, 0)

    for qb in range(QBLOCKS_PER_STEP):
        for pp in range(PAIRS_PER_STEP):
            outs = []
            for hh in (2 * pp, 2 * pp + 1):
                acc = acc_s[qb * HEADS_PER_STEP + hh]
                outs.append(acc[:HEAD_DIM] / acc[HEAD_DIM:HEAD_DIM + 1])
            o_ref[q_rows[qb], pair_lanes[pp]] = jnp.transpose(jnp.concatenate(outs, axis=0)).astype(o_ref.dtype)


def _moba_attention(qkv, near_bias):
    b, s, _ = qkv.shape
    blk = MOBA_BLOCK
    nb = s // blk
    nbp = -(-nb // SUBLANES) * SUBLANES
    width = PAIRS_PER_STEP * PAIR
    n_groups = D_ATTN // width
    qrows = QBLOCKS_PER_STEP * blk
    assert FAR_BLOCKS == 2 and QBLOCKS_PER_STEP == 2, "near / far chunk bookkeeping assumes pairs of blocks"
    assert s % (FAR_BLOCKS * blk) == 0 and s % qrows == 0 and nbp <= LANES and D_ATTN % width == 0
    kern = functools.partial(_attn_kernel, nb=nb)
    hps = HEADS_PER_STEP
    bias_tile = (hps, None, 2 * blk, blk)
    return pl.pallas_call(
        kern,
        grid=(b, n_groups, nb // QBLOCKS_PER_STEP),
        in_specs=[pl.BlockSpec((None, qrows, width), lambda bi, g, i: (bi, i, g)),
                  pl.BlockSpec((None, s, width), lambda bi, g, i: (bi, 0, n_groups + g)),
                  pl.BlockSpec((None, s, width), lambda bi, g, i: (bi, 0, 2 * n_groups + g)),
                  pl.BlockSpec(bias_tile, lambda bi, g, i: (g, jnp.minimum(i, 1), 0, 0)),
                  pl.BlockSpec(bias_tile, lambda bi, g, i: (g, 1, 0, 0))],
        out_specs=pl.BlockSpec((None, qrows, width), lambda bi, g, i: (bi, i, g)),
        out_shape=jax.ShapeDtypeStruct((b, s, D_ATTN), BF16),
        scratch_shapes=[pltpu.VMEM((PAIRS_PER_STEP, s, 2 * LANES), BF16),
                        pltpu.VMEM((hps, nb, VT_ROWS, blk), BF16),
                        pltpu.VMEM((nbp, width), F32),
                        pltpu.VMEM((STREAMS, blk, 2 * LANES), BF16),
                        pltpu.VMEM((STREAMS, blk, 2 * LANES), BF16),
                        pltpu.VMEM((2, STREAMS, FAR_BLOCKS * blk, blk), F32),
                        pltpu.VMEM((STREAMS, SUBLANES, blk), F32),
                        pltpu.VMEM((STREAMS, VT_ROWS, blk), F32)],
        compiler_params=pltpu.CompilerParams(
            dimension_semantics=("parallel", "parallel", "arbitrary"), vmem_limit_bytes=VMEM_LIMIT),
        name="moba_attn",
    )(qkv, qkv, qkv, near_bias, near_bias)


def _mix_kernel(x_ref, a_ref, u_ref, uh_ref, wpool_ref, pscale_ref, wout_ref, gffn_ref,
                wr_hi_ref, wr_lo_ref, br_ref,
                x2_ref, h2_ref, meta_ref, metat_ref, cnt_ref, ext_s, carry_s, *, tm, seq):
    t = pl.program_id(0)

    @pl.when(t == 0)
    def _init():
        carry_s[...] = jnp.zeros(carry_s.shape, F32)

    pos0 = (t * tm) % seq
    ext_s[0:HALO, :] = jnp.where(pos0 == 0, 0.0, uh_ref[...])
    ext_s[HALO:HALO + tm, :] = u_ref[...]
    pos = pos0 + lax.broadcasted_iota(jnp.int32, (tm, 1), 0)
    parts = []
    for gi, w in enumerate(POOL_WINDOWS):
        c0 = gi * POOL_GROUP_DIM
        cols = slice(c0, c0 + POOL_GROUP_DIM)
        tok = ext_s[HALO:HALO + tm, cols]
        win = tok
        for d in range(1, w):
            win = win + ext_s[HALO - d:HALO - d + tm, cols]
        cnt = jnp.minimum(pos + 1, w).astype(F32)
        pooled = win / cnt - tok
        parts.append(_dot(pooled.astype(BF16), wpool_ref[gi]) * pscale_ref[:, cols])
    b = jnp.concatenate(parts, axis=1).astype(BF16)

    mix = _dot(a_ref[...], wout_ref[:D_ATTN, :]) + _dot(b, wout_ref[D_ATTN:, :])
    x2 = x_ref[...] + mix
    x2_ref[...] = x2
    h2 = _rms(x2, gffn_ref[...])
    _store_token_tiles(h2_ref, h2)

    hi = h2.astype(BF16)
    lo = (h2 - hi.astype(F32)).astype(BF16)
    logits = jnp.transpose(_dot(hi, wr_hi_ref[...]) + _dot(lo, wr_hi_ref[...]) + _dot(hi, wr_lo_ref[...])
                           + br_ref[...])
    row = lax.broadcasted_iota(jnp.int32, (SUBLANES, tm), 0)
    lg = jnp.where(row < N_GROUPS, logits[0:SUBLANES], -jnp.inf)
    top_g = jnp.max(lg, axis=0, keepdims=True)
    g_idx = jnp.min(jnp.where(lg == top_g, row, SUBLANES), axis=0, keepdims=True)
    p_g = 1.0 / jnp.sum(jnp.exp(lg - top_g), axis=0, keepdims=True)
    l2 = logits[SUBLANES:2 * SUBLANES]
    for g in range(1, N_GROUPS):
        l2 = jnp.where(g_idx == g, logits[(g + 1) * SUBLANES:(g + 2) * SUBLANES], l2)
    v1 = jnp.max(l2, axis=0, keepdims=True)
    i1 = jnp.min(jnp.where(l2 == v1, row, SUBLANES), axis=0, keepdims=True)
    l2 = jnp.where(row == i1, -jnp.inf, l2)
    v2 = jnp.max(l2, axis=0, keepdims=True)
    i2 = jnp.min(jnp.where(l2 == v2, row, SUBLANES), axis=0, keepdims=True)
    e1 = g_idx * E_PER_GROUP + i1
    e2 = g_idx * E_PER_GROUP + i2
    t2 = jnp.exp(v2 - v1)
    w1 = 1.0 / (1.0 + t2)
    gate1 = p_g * w1
    gate2 = p_g * (t2 * w1)

    e_row = lax.broadcasted_iota(jnp.int32, (N_EXPERTS, tm), 0)
    oh1 = jnp.where(e_row == e1, 1.0, 0.0)
    oh2 = jnp.where(e_row == e2, 1.0, 0.0)
    oh = oh1 + oh2
    src = lax.broadcasted_iota(jnp.int32, (tm, tm), 0)
    dst = lax.broadcasted_iota(jnp.int32, (tm, tm), 1)
    earlier = jnp.where(src < dst, 1.0, 0.0).astype(BF16)
    seen = _dot(oh.astype(BF16), earlier) + carry_s[:, 0:1]
    r1 = jnp.sum(oh1 * seen, axis=0, keepdims=True)
    r2 = jnp.sum(oh2 * seen, axis=0, keepdims=True)
    carry_s[...] = carry_s[...] + jnp.sum(oh, axis=1, keepdims=True)
    cnt_ref[...] = carry_s[...]

    rows = (e1.astype(F32), e2.astype(F32), r1, r2, gate1, gate2)
    meta_t = jnp.zeros((SUBLANES, tm), F32)
    for k, val in enumerate(rows):
        meta_t = jnp.where(row == k, val, meta_t)
    metat_ref[...] = meta_t.astype(jnp.int32)
    meta_ref[...] = jnp.transpose(jnp.concatenate([meta_t, jnp.zeros((LANES - SUBLANES, tm), F32)], axis=0))


def _mix_route(x2d, a2d, u2d, w_pool, pool_scale, w_out, g_ffn, wr, br, seq):
    t, d = x2d.shape
    tm = TM
    assert seq % tm == 0 and tm % HALO == 0
    hb = tm // HALO
    kern = functools.partial(_mix_kernel, tm=tm, seq=seq)
    tok = lambda w: pl.BlockSpec((tm, w), lambda i: (i, 0))
    full = lambda *shape: pl.BlockSpec(shape, lambda i: (0,) * len(shape))
    wr_hi = wr.astype(BF16)
    wr_lo = (wr - wr_hi.astype(F32)).astype(BF16)
    return pl.pallas_call(
        kern,
        grid=(t // tm,),
        in_specs=[tok(d), tok(D_ATTN), tok(D_POOL),
                  pl.BlockSpec((HALO, D_POOL), lambda i: (jnp.maximum(i * hb - 1, 0), 0)),
                  full(len(POOL_WINDOWS), POOL_GROUP_DIM, POOL_GROUP_DIM), full(1, D_POOL),
                  full(D_ATTN + D_POOL, d), full(1, d), full(d, LANES), full(d, LANES), full(1, LANES)],
        out_specs=[tok(d), pl.BlockSpec((tm * TOKEN_ROWS, LANES), lambda i: (i, 0)), tok(LANES),
                   pl.BlockSpec((SUBLANES, tm), lambda i: (0, i)),
                   full(N_EXPERTS, LANES)],
        out_shape=[jax.ShapeDtypeStruct((t, d), F32), jax.ShapeDtypeStruct((t * TOKEN_ROWS, LANES), U32),
                   jax.ShapeDtypeStruct((t, LANES), F32),
                   jax.ShapeDtypeStruct((SUBLANES, t), jnp.int32),
                   jax.ShapeDtypeStruct((N_EXPERTS, LANES), F32)],
        scratch_shapes=[pltpu.VMEM((HALO + tm, D_POOL), F32), pltpu.VMEM((N_EXPERTS, LANES), F32)],
        compiler_params=pltpu.CompilerParams(dimension_semantics=("arbitrary",), vmem_limit_bytes=VMEM_LIMIT),
        name="mix_route",
    )(x2d, a2d, u2d, u2d, w_pool.astype(BF16), pool_scale.reshape(1, D_POOL).astype(F32),
      w_out.astype(BF16), g_ffn.reshape(1, d).astype(F32), wr_hi, wr_lo, br)


ISSUE_UNROLL = 8
COMBINE_CHUNKS = 2
EXPERT_CHUNKS = 2


def _pos_kernel(offs_ref, meta_ref, pos_ref):
    meta = meta_ref[...]
    start = jnp.zeros(meta.shape, jnp.int32)
    for e in range(N_EXPERTS):
        start = jnp.where(meta == e, offs_ref[e], start)
    pos_ref[...] = start[0:2, :] + meta[2:4, :]


def _moe_positions(offs, metat, tm):
    t = metat.shape[1]
    pos = pl.pallas_call(
        _pos_kernel,
        in_specs=[pl.BlockSpec(memory_space=pltpu.SMEM), pl.BlockSpec(memory_space=pltpu.VMEM)],
        out_specs=pl.BlockSpec(memory_space=pltpu.VMEM),
        out_shape=jax.ShapeDtypeStruct((2, t), jnp.int32),
        name="moe_pos",
    )(offs, metat)
    return pos.reshape(2, t // tm, tm).transpose(1, 0, 2).reshape(2 * t)


def _scatter_kernel(pos_ref, h2_ref, xs_ref, sem, *, tm):
    def issue(r, carry):
        for k in range(2):
            _token_copy(h2_ref, r, xs_ref, pos_ref[k * tm + r], sem).start(priority=k)
        return carry

    lax.fori_loop(0, tm, issue, 0, unroll=ISSUE_UNROLL)

    def drain(r, carry):
        for k in range(2):
            _token_copy(h2_ref, 0, xs_ref, 0, sem).wait()
        return carry

    lax.fori_loop(0, tm, drain, 0, unroll=ISSUE_UNROLL)


def _moe_scatter(pos, h2t):
    rows = h2t.shape[0]
    tm = TM
    kern = functools.partial(_scatter_kernel, tm=tm)
    return pl.pallas_call(
        kern,
        grid=(rows // (tm * TOKEN_ROWS),),
        in_specs=[pl.BlockSpec((2 * tm,), lambda i: (i,), memory_space=pltpu.SMEM),
                  pl.BlockSpec((tm * TOKEN_ROWS, LANES), lambda i: (i, 0))],
        out_specs=pl.BlockSpec(memory_space=pl.ANY),
        out_shape=jax.ShapeDtypeStruct((2 * rows, LANES), U32),
        scratch_shapes=[pltpu.SemaphoreType.DMA(())],
        compiler_params=pltpu.CompilerParams(dimension_semantics=("arbitrary",), vmem_limit_bytes=VMEM_LIMIT),
        name="moe_scatter",
    )(pos, h2t)


def _expert_kernel(tile_ref, exp_ref, lo_ref, hi_ref, next_ref, xs_ref, wg_hbm, wu_hbm, wd_hbm, ys_ref,
                   wg_f, wu_f, wd_f, wg_s, wu_s, wd_s, sem, *, tm):
    s = pl.program_id(0)
    lo = lo_ref[s]
    hi = hi_ref[s]
    new_expert = jnp.logical_or(s == 0, exp_ref[s] != exp_ref[jnp.maximum(s - 1, 0)])

    def weight_copies(e):
        return [pltpu.make_async_copy(src.at[e], dst, sem.at[k])
                for k, (src, dst) in enumerate(((wg_hbm, wg_f), (wu_hbm, wu_f), (wd_hbm, wd_f)))]

    @pl.when(s == 0)
    def _first_fetch():
        for cp in weight_copies(exp_ref[0]):
            cp.start()

    @pl.when(jnp.logical_and(hi > 0, new_expert))
    def _round_weights():
        for cp in weight_copies(exp_ref[s]):
            cp.wait()
        wg_s[...] = wg_f[...].astype(BF16)
        wu_s[...] = wu_f[...].astype(BF16)
        wd_s[...] = wd_f[...].astype(BF16)

        @pl.when(next_ref[s] >= 0)
        def _fetch_next():
            for cp in weight_copies(next_ref[s]):
                cp.start()

    n = tm // EXPERT_CHUNKS

    def expert_rows(first):
        x = _load_token_tiles(xs_ref, n, first).astype(BF16)
        a = _dot(x, wg_s[...])
        u = _dot(x, wu_s[...])
        hid = (a * jax.nn.sigmoid(a) * u).astype(BF16)
        return _dot(hid, wd_s[...])

    for ch in range(EXPERT_CHUNKS):
        first = ch * n
        active = jnp.logical_and(lo < first + n, hi > first)

        @pl.when(jnp.logical_and(active, lo <= first))
        def _whole_chunk():
            _store_token_tiles(ys_ref, expert_rows(first), first)

        @pl.when(jnp.logical_and(active, lo > first))
        def _chunk_tail():
            row = first + lax.broadcasted_iota(jnp.int32, (n, 1), 0)
            kept = _load_token_tiles(ys_ref, n, first)
            _store_token_tiles(ys_ref, jnp.where(row >= lo, expert_rows(first), kept), first)


def _moe_experts(step_tile, step_exp, step_lo, step_hi, step_next, xs, w_gate, w_up, w_down):
    n_rows = xs.shape[0]
    tm = TM_EXPERT
    _, d, f = w_gate.shape
    n_steps = step_tile.shape[0]
    kern = functools.partial(_expert_kernel, tm=tm)
    rows = lambda s, tl, ex, lo, hi, nx: (tl[s], 0)
    grid_spec = pltpu.PrefetchScalarGridSpec(
        num_scalar_prefetch=5,
        grid=(n_steps,),
        in_specs=[pl.BlockSpec((tm * TOKEN_ROWS, LANES), rows),
                  pl.BlockSpec(memory_space=pl.ANY), pl.BlockSpec(memory_space=pl.ANY),
                  pl.BlockSpec(memory_space=pl.ANY)],
        out_specs=pl.BlockSpec((tm * TOKEN_ROWS, LANES), rows),
        scratch_shapes=[pltpu.VMEM((d, f), F32), pltpu.VMEM((d, f), F32), pltpu.VMEM((f, d), F32),
                        pltpu.VMEM((d, f), BF16), pltpu.VMEM((d, f), BF16), pltpu.VMEM((f, d), BF16),
                        pltpu.SemaphoreType.DMA((3,))],
    )
    return pl.pallas_call(
        kern,
        grid_spec=grid_spec,
        out_shape=jax.ShapeDtypeStruct((n_rows, LANES), U32),
        compiler_params=pltpu.CompilerParams(dimension_semantics=("arbitrary",), vmem_limit_bytes=VMEM_LIMIT),
        name="moe_experts",
    )(step_tile, step_exp, step_lo, step_hi, step_next, xs, w_gate, w_up, w_down)


def _expert_steps(counts, n_rows, tm):
    n_e = counts.shape[0]
    n_tiles = n_rows // tm
    n_steps = n_tiles + n_e
    ends = jnp.cumsum(counts)
    starts = ends - counts
    first_tile = starts // tm
    last_tile = jnp.maximum(ends - 1, 0) // tm
    tiles_e = jnp.where(counts > 0, last_tile - first_tile + 1, 0)
    step_end = jnp.cumsum(tiles_e)
    step_start = step_end - tiles_e
    total = step_end[-1]
    s = jnp.arange(n_steps, dtype=jnp.int32)
    s_eff = jnp.minimum(s, total - 1)
    mine = ((step_start[None, :] <= s_eff[:, None]) & (s_eff[:, None] < step_end[None, :])).astype(jnp.int32)
    pick = lambda table: jnp.sum(mine * table[None, :], axis=1)
    exp = pick(jnp.arange(n_e, dtype=jnp.int32))
    tile = pick(first_tile) + (s_eff - pick(step_start))
    lo = jnp.maximum(pick(starts) - tile * tm, 0)
    hi = jnp.minimum(pick(ends) - tile * tm, tm)
    lo = jnp.where(s < total, lo, tm)
    hi = jnp.where(s < total, hi, 0)
    e_ids = jnp.arange(n_e, dtype=jnp.int32)
    later = (e_ids[None, :] > e_ids[:, None]) & (counts[None, :] > 0)
    next_e = jnp.min(jnp.where(later, e_ids[None, :], n_e), axis=1)
    nxt = pick(jnp.where(next_e < n_e, next_e, -1))
    return (tile.astype(jnp.int32), exp, lo.astype(jnp.int32), hi.astype(jnp.int32), nxt.astype(jnp.int32),
            starts.astype(jnp.int32))


def _combine_kernel(pos_ref, pos_next_ref, x2_ref, gates_ref, p_ref, wproj_ref, wgate_ref, gple_ref, gfin_ref,
                    ys_ref, out_ref, rows_s, sem, *, tm):
    i = pl.program_id(0)
    slot = i % 2

    def gather(p_ref, sl):
        def issue(r, carry):
            for k in range(2):
                _token_copy(ys_ref, p_ref[k * tm + r], rows_s.at[sl, k], r, sem.at[sl]).start(priority=k)
            return carry

        lax.fori_loop(0, tm, issue, 0, unroll=ISSUE_UNROLL)

    @pl.when(i == 0)
    def _first_tile():
        gather(pos_ref, 0)

    @pl.when(i + 1 < pl.num_programs(0))
    def _next_tile():
        gather(pos_next_ref, 1 - slot)

    def drain(r, carry):
        for k in range(2):
            _token_copy(ys_ref, 0, rows_s.at[slot, k], 0, sem.at[slot]).wait()
        return carry

    lax.fori_loop(0, tm, drain, 0, unroll=ISSUE_UNROLL)

    n = tm // COMBINE_CHUNKS
    for ch in range(COMBINE_CHUNKS):
        rows = slice(ch * n, (ch + 1) * n)
        gates = gates_ref[rows, :]
        y = (gates[:, 4:5] * _load_token_tiles(rows_s.at[slot, 0], n, ch * n)
             + gates[:, 5:6] * _load_token_tiles(rows_s.at[slot, 1], n, ch * n))
        x3 = x2_ref[rows, :] + y
        h3 = _rms(x3, gple_ref[...]).astype(BF16)
        gate = jax.nn.sigmoid(_dot(h3, wgate_ref[...]))
        pe = _dot(p_ref[rows, :].astype(BF16), wproj_ref[...])
        x4 = x3 + pe * gate
        out_ref[rows, :] = _rms(x4, gfin_ref[...])


def _combine_ple(pos, x2, meta, p2d, w_ple_proj, w_ple_gate, g_ple, g_final, ys):
    t, d = x2.shape
    d_ple = p2d.shape[1]
    tm = TM
    n_tiles = t // tm
    kern = functools.partial(_combine_kernel, tm=tm)
    tok = lambda w: pl.BlockSpec((tm, w), lambda i: (i, 0))
    full = lambda *shape: pl.BlockSpec(shape, lambda i: (0,) * len(shape))
    return pl.pallas_call(
        kern,
        grid=(t // tm,),
        in_specs=[pl.BlockSpec((2 * tm,), lambda i: (i,), memory_space=pltpu.SMEM),
                  pl.BlockSpec((2 * tm,), lambda i: (jnp.minimum(i + 1, n_tiles - 1),), memory_space=pltpu.SMEM),
                  tok(d), tok(LANES), tok(d_ple), full(d_ple, d), full(d, d), full(1, d), full(1, d),
                  pl.BlockSpec(memory_space=pl.ANY)],
        out_specs=tok(d),
        out_shape=jax.ShapeDtypeStruct((t, d), F32),
        scratch_shapes=[pltpu.VMEM((2, 2, tm * TOKEN_ROWS, LANES), U32), pltpu.SemaphoreType.DMA((2,))],
        compiler_params=pltpu.CompilerParams(dimension_semantics=("arbitrary",), vmem_limit_bytes=VMEM_LIMIT),
        name="combine_ple",
    )(pos, pos, x2, meta, p2d, w_ple_proj.astype(BF16), w_ple_gate.astype(BF16),
      g_ple.reshape(1, d).astype(F32), g_final.reshape(1, d).astype(F32), ys)


def _layer(x2d, p2d, seq, g_mix, w_in, w_pool, pool_scale, w_out, near_bias, g_ffn,
           w_r1, b_r1, w_r2, b_r2, w_gate, w_up, w_down, g_ple, w_ple_proj, w_ple_gate, g_final):
    t, d = x2d.shape
    batch = t // seq
    qkv, u = _in_proj(x2d, g_mix, w_in)
    a = _moba_attention(qkv.reshape(batch, seq, 3 * D_ATTN), near_bias).reshape(t, D_ATTN)

    assert N_GROUPS <= SUBLANES and E_PER_GROUP == SUBLANES
    pad_g = SUBLANES - N_GROUPS
    pad_e = LANES - SUBLANES - N_EXPERTS
    wr = jnp.concatenate([w_r1, jnp.zeros((d, pad_g), F32),
                          jnp.transpose(w_r2, (1, 0, 2)).reshape(d, N_EXPERTS),
                          jnp.zeros((d, pad_e), F32)], axis=1).astype(F32)
    br = jnp.concatenate([b_r1, jnp.zeros((pad_g,), F32), b_r2.reshape(N_EXPERTS), jnp.zeros((pad_e,), F32)])
    br = br.astype(F32).reshape(1, LANES)
    assert d == 2 * TOKEN_ROWS * LANES, "token tiles hold d/2 words in TOKEN_ROWS rows of 128"
    x2, h2t, meta, metat, cnt = _mix_route(x2d, a, u, w_pool, pool_scale, w_out, g_ffn, wr, br, seq)

    counts = cnt[:, 0].astype(jnp.int32)
    step_tile, step_exp, step_lo, step_hi, step_next, offs = _expert_steps(counts, 2 * t, TM_EXPERT)
    pos = _moe_positions(offs, metat, TM)
    xs = _moe_scatter(pos, h2t)
    f = w_gate.shape[-1]
    ys = _moe_experts(step_tile, step_exp, step_lo, step_hi, step_next, xs,
                      w_gate.reshape(N_EXPERTS, d, f), w_up.reshape(N_EXPERTS, d, f),
                      w_down.reshape(N_EXPERTS, f, d))
    return _combine_ple(pos, x2, meta, p2d, w_ple_proj, w_ple_gate, g_ple, g_final, ys)


def kernel(x, p, g_mix, w_in, w_pool, pool_scale, w_out, rel_bias, g_ffn, w_r1, b_r1, w_r2, b_r2,
           w_gate, w_up, w_down, g_ple, w_ple_proj, w_ple_gate, g_final):
    batch, seq, d = x.shape
    depth = p.shape[0]
    assert depth == 1, "the final norm is fused into the last stage of a single layer"
    near_bias = _bias_tables(rel_bias, seq)
    out = _layer(x.reshape(batch * seq, d), p[0].reshape(batch * seq, -1), seq,
                 g_mix[0], w_in[0], w_pool[0], pool_scale[0], w_out[0], near_bias, g_ffn[0],
                 w_r1[0], b_r1[0], w_r2[0], b_r2[0], w_gate[0], w_up[0], w_down[0],
                 g_ple[0], w_ple_proj[0], w_ple_gate[0], g_final)
    return out.reshape(batch, seq, d)
```

```python
import functools
import math

import numpy as np
import jax
import jax.numpy as jnp
from jax import lax
from jax.experimental import pallas as pl
from jax.experimental.pallas import tpu as pltpu

F32 = jnp.float32
BF16 = jnp.bfloat16

HEAD_DIM = 64
N_HEADS = 8
D_ATTN = N_HEADS * HEAD_DIM
POOL_WINDOWS = (2, 4, 8, 16)
POOL_GROUP_DIM = 128
D_POOL = POOL_GROUP_DIM * len(POOL_WINDOWS)
MOBA_BLOCK = 256
MOBA_TOPK = 3
REL_BUCKETS = 32
REL_MAX_DIST = 128
N_GROUPS = 4
E_PER_GROUP = 8
N_EXPERTS = N_GROUPS * E_PER_GROUP
RMS_EPS = 1e-6

LANES = 128
SUBLANES = 8
V7X_VMEM_BYTES = 64 * 1024 * 1024
VMEM_LIMIT = V7X_VMEM_BYTES // 8 * 7

MASK_VALUE = -1e30
LOG2E = math.log2(math.e)
HALO = max(POOL_WINDOWS)
TM_PROJ = 1024
TM = 1024
TM_EXPERT = 512
PAIR = 2 * HEAD_DIM


def _rms(x, g):
    return x * lax.rsqrt(jnp.mean(x * x, axis=-1, keepdims=True) + RMS_EPS) * g


def _dot(a, b):
    return jnp.dot(a, b, preferred_element_type=F32)


def _dot_nt(a, b):
    return lax.dot_general(a, b, (((1,), (1,)), ((), ())), preferred_element_type=F32)


TOKEN_ROWS = 4
U32 = jnp.uint32
HIGH_HALF = 0xFFFF0000


def _pack_bf16_pairs(x):
    half = x.shape[1] // 2
    bits = lambda v: lax.bitcast_convert_type(v.astype(BF16).astype(F32), U32)
    return (bits(x[:, :half]) >> 16) | (bits(x[:, half:]) & U32(HIGH_HALF))


def _unpack_bf16_pairs(w):
    low = lax.bitcast_convert_type(w << 16, F32)
    high = lax.bitcast_convert_type(w & U32(HIGH_HALF), F32)
    return jnp.concatenate([low, high], axis=1)


def _store_token_tiles(ref, x, first=0):
    n = x.shape[0]
    w = _pack_bf16_pairs(x)
    for c in range(TOKEN_ROWS):
        ref[pl.ds(first * TOKEN_ROWS + c, n, stride=TOKEN_ROWS), :] = w[:, c * LANES:(c + 1) * LANES]


def _load_token_tiles(ref, n, first=0):
    w = jnp.concatenate([ref[pl.ds(first * TOKEN_ROWS + c, n, stride=TOKEN_ROWS), :] for c in range(TOKEN_ROWS)],
                        axis=1)
    return _unpack_bf16_pairs(w)


def _token_copy(src_ref, src_tok, dst_ref, dst_tok, sem):
    src = src_ref.at[pl.ds(pl.multiple_of(src_tok * TOKEN_ROWS, TOKEN_ROWS), TOKEN_ROWS)]
    dst = dst_ref.at[pl.ds(pl.multiple_of(dst_tok * TOKEN_ROWS, TOKEN_ROWS), TOKEN_ROWS)]
    return pltpu.make_async_copy(src, dst, sem)


def _rel_bucket_np(n):
    n = np.maximum(n, 0)
    max_exact = REL_BUCKETS // 2
    nf = np.maximum(n, 1).astype(np.float32)
    large = max_exact + (np.log(nf / np.float32(max_exact)) / np.float32(math.log(REL_MAX_DIST / max_exact))
                         * np.float32(REL_BUCKETS - max_exact)).astype(np.int32)
    large = np.minimum(large, REL_BUCKETS - 1)
    return np.where(n < max_exact, n, large).astype(np.int32)


def _bucket_tiles(seq):
    r = np.arange(MOBA_BLOCK)
    dist = r[None, :] - r[:, None]
    own = np.where(dist >= 0, _rel_bucket_np(dist), -1).astype(np.int32)
    prev = _rel_bucket_np(dist + MOBA_BLOCK)
    far = _rel_bucket_np(np.arange(MOBA_BLOCK + 1, max(seq, MOBA_BLOCK + 2)))
    assert np.all(far == REL_BUCKETS - 1)
    return np.stack([own, prev])


def _bias_kernel(rb_ref, bucket_ref, near_ref):
    h = pl.program_id(0)
    far = rb_ref[REL_BUCKETS - 1, h]
    tiles = []
    for which in range(2):
        bucket = bucket_ref[which]
        tile = jnp.where(bucket < 0, MASK_VALUE, 0.0).astype(F32)
        for b in range(REL_BUCKETS):
            tile = jnp.where(bucket == b, (rb_ref[b, h] - far) * LOG2E, tile)
        tiles.append(tile)
    own, prev = tiles
    blk = own.shape[0]
    near_ref[0, :blk, :] = own
    near_ref[0, blk:, :] = jnp.full(own.shape, MASK_VALUE, F32)
    near_ref[1, :blk, :] = prev
    near_ref[1, blk:, :] = own


def _bias_tables(rel_bias, seq):
    buckets = _bucket_tiles(seq)
    blk = MOBA_BLOCK
    return pl.pallas_call(
        _bias_kernel,
        grid=(N_HEADS,),
        in_specs=[pl.BlockSpec(memory_space=pltpu.SMEM),
                  pl.BlockSpec((2, blk, blk), lambda h: (0, 0, 0))],
        out_specs=pl.BlockSpec((None, 2, 2 * blk, blk), lambda h: (h, 0, 0, 0)),
        out_shape=jax.ShapeDtypeStruct((N_HEADS, 2, 2 * blk, blk), F32),
        name="bias_tables",
    )(rel_bias.astype(F32), jnp.asarray(buckets))


def _inproj_kernel(x_ref, g_ref, w_ref, qkv_ref, u_ref):
    hb = _rms(x_ref[...], g_ref[...]).astype(BF16)
    scale = HEAD_DIM ** -0.5 * LOG2E
    qkv_ref[:, :D_ATTN] = (_dot(hb, w_ref[:, :D_ATTN]) * scale).astype(BF16)
    qkv_ref[:, D_ATTN:2 * D_ATTN] = _dot(hb, w_ref[:, D_ATTN:2 * D_ATTN]).astype(BF16)
    qkv_ref[:, 2 * D_ATTN:] = _dot(hb, w_ref[:, 2 * D_ATTN:3 * D_ATTN]).astype(BF16)
    u_ref[...] = _dot(hb, w_ref[:, 3 * D_ATTN:])


def _in_proj(x2d, g_mix, w_in):
    t, d = x2d.shape
    n_in = w_in.shape[1]
    tm = min(TM_PROJ, t)
    return pl.pallas_call(
        _inproj_kernel,
        grid=(t // tm,),
        in_specs=[pl.BlockSpec((tm, d), lambda i: (i, 0)),
                  pl.BlockSpec((1, d), lambda i: (0, 0)),
                  pl.BlockSpec((d, n_in), lambda i: (0, 0))],
        out_specs=[pl.BlockSpec((tm, 3 * D_ATTN), lambda i: (i, 0)),
                   pl.BlockSpec((tm, D_POOL), lambda i: (i, 0))],
        out_shape=[jax.ShapeDtypeStruct((t, 3 * D_ATTN), BF16),
                   jax.ShapeDtypeStruct((t, D_POOL), F32)],
        compiler_params=pltpu.CompilerParams(dimension_semantics=("parallel",), vmem_limit_bytes=VMEM_LIMIT),
        name="in_proj",
    )(x2d, g_mix.reshape(1, d).astype(F32), w_in.astype(BF16))


FAR_BLOCKS = 2
PAIRS_PER_STEP = 2
HEADS_PER_STEP = 2 * PAIRS_PER_STEP
QBLOCKS_PER_STEP = 2
STREAMS = QBLOCKS_PER_STEP * HEADS_PER_STEP
ONES_ROWS = 16
VT_ROWS = HEAD_DIM + ONES_ROWS


def _attn_kernel(q_ref, k_ref, v_ref, nearb0_ref, nearb1_ref, o_ref,
                 kaug, vt, kmean, qnear, qfar, sbuf, m_s, acc_s, *, nb):
    blk = MOBA_BLOCK
    step = pl.program_id(2)
    pair_lanes = [slice(pp * LANES, (pp + 1) * LANES) for pp in range(PAIRS_PER_STEP)]
    q_rows = [slice(qb * blk, (qb + 1) * blk) for qb in range(QBLOCKS_PER_STEP)]
    near_bias = (nearb0_ref, nearb1_ref)

    @pl.when(step == 0)
    def _build_keys():
        kmean[...] = jnp.zeros(kmean.shape, F32)
        lane_b = lax.broadcasted_iota(jnp.int32, (blk, LANES), 1)

        def body(j, carry):
            rows = pl.ds(pl.multiple_of(j * blk, blk), blk)
            kmean[pl.ds(j, 1), :] = jnp.mean(k_ref[rows, :].astype(F32), axis=0, keepdims=True)
            for pp in range(PAIRS_PER_STEP):
                kaug[pp, rows, :LANES] = k_ref[rows, pair_lanes[pp]]
                kaug[pp, rows, LANES:] = jnp.where(lane_b == j, 1.0, 0.0).astype(BF16)
                v_t = jnp.transpose(v_ref[rows, pair_lanes[pp]].astype(F32))
                for half in range(2):
                    hh = 2 * pp + half
                    vt[hh, j, :HEAD_DIM, :] = v_t[half * HEAD_DIM:(half + 1) * HEAD_DIM, :].astype(BF16)
                    vt[hh, j, HEAD_DIM:, :] = jnp.ones((ONES_ROWS, blk), BF16)
            return carry

        lax.fori_loop(0, nb, body, 0)

    lane = lax.broadcasted_iota(jnp.int32, (blk, LANES), 1)
    nbp = kmean.shape[0]
    blk_id = lax.broadcasted_iota(jnp.int32, (nbp, blk), 0)

    for st in range(STREAMS):
        qb, hh = divmod(st, HEADS_PER_STEP)
        pp, half = divmod(hh, 2)
        i = QBLOCKS_PER_STEP * step + qb
        q = q_ref[q_rows[qb], pair_lanes[pp]]
        in_head = (lane < HEAD_DIM) if half == 0 else (lane >= HEAD_DIM)
        qm = jnp.where(in_head, q, jnp.zeros_like(q))
        kmean_b = kmean[:, pair_lanes[pp]].astype(BF16)
        gate = jnp.where(blk_id < i, _dot_nt(kmean_b, qm), -jnp.inf)
        chosen_t = jnp.zeros((nbp, blk), F32)
        for _ in range(MOBA_TOPK):
            top = jnp.max(gate, axis=0, keepdims=True)
            idx = jnp.min(jnp.where(gate == top, blk_id, nbp), axis=0, keepdims=True)
            hit = blk_id == idx
            chosen_t = jnp.where(hit, 1.0, chosen_t)
            gate = jnp.where(hit, -jnp.inf, gate)
        if nbp < LANES:
            chosen_t = jnp.concatenate([chosen_t, jnp.zeros((LANES - nbp, blk), F32)], axis=0)
        penalty = jnp.where(jnp.transpose(chosen_t) > 0.0, 0.0, MASK_VALUE)
        qnear[st, :, :LANES] = qm
        qnear[st, :, LANES:] = jnp.where(lane < i, penalty, 0.0).astype(BF16)
        qfar[st, :, :LANES] = qm
        qfar[st, :, LANES:] = jnp.where(lane < i - 1, penalty, MASK_VALUE).astype(BF16)

    def values_t(hh, first_block):
        return jnp.concatenate([vt[hh, first_block + k] for k in range(FAR_BLOCKS)], axis=1)

    def fold(st, s_t, v_t, first):
        m_blk = jnp.max(s_t, axis=0, keepdims=True)
        if first:
            m_new = m_blk
        else:
            m_old = m_s[st][0:1]
            m_new = jnp.maximum(m_old, m_blk)
            alpha = jnp.exp2(m_old - m_new)
        p_t = jnp.exp2(s_t - m_new)
        pv = _dot(v_t, p_t.astype(BF16))
        acc_s[st] = pv if first else alpha * acc_s[st] + pv
        m_s[st] = jnp.broadcast_to(m_new, m_s.shape[1:])

    near_blocks = [jnp.maximum(QBLOCKS_PER_STEP * step + qb - 1, 0) for qb in range(QBLOCKS_PER_STEP)]
    near_scores = []
    for st in range(STREAMS):
        qb, hh = divmod(st, HEADS_PER_STEP)
        near = pl.ds(pl.multiple_of(near_blocks[qb] * blk, blk), 2 * blk)
        near_scores.append(_dot_nt(kaug[hh // 2, near, :], qnear[st]) + near_bias[qb][hh])

    chunk = FAR_BLOCKS * blk
    last_chunk = kaug.shape[1] // chunk - 1
    n_far = (QBLOCKS_PER_STEP * step + QBLOCKS_PER_STEP - 1 + FAR_BLOCKS - 2) // FAR_BLOCKS

    def far_scores(c, slot):
        rows = pl.ds(pl.multiple_of(jnp.minimum(c, last_chunk) * chunk, chunk), chunk)
        for st in range(STREAMS):
            sbuf[slot, st] = _dot_nt(kaug[(st % HEADS_PER_STEP) // 2, rows, :], qfar[st])

    def far_fold(c, slot):
        first_block = jnp.minimum(c, last_chunk) * FAR_BLOCKS
        for st in range(STREAMS):
            fold(st, sbuf[slot, st], values_t(st % HEADS_PER_STEP, first_block), False)

    far_scores(0, 0)
    for st in range(STREAMS):
        qb, hh = divmod(st, HEADS_PER_STEP)
        fold(st, near_scores[st], values_t(hh, near_blocks[qb]), True)

    def far_body(cc, carry):
        c = 2 * cc
        far_scores(c + 1, 1)
        far_fold(c, 0)
        far_scores(c + 2, 0)
        far_fold(c + 1, 1)
        return carry

    lax.fori_loop(0, n_far // 2, far_body, 0)

    @pl.when(n_far % 2 == 1)
    def _last_chunk():
        far_fold(n_far - 1, 0)

    for qb in range(QBLOCKS_PER_STEP):
        for pp in range(PAIRS_PER_STEP):
            outs = []
            for hh in (2 * pp, 2 * pp + 1):
                acc = acc_s[qb * HEADS_PER_STEP + hh]
                outs.append(acc[:HEAD_DIM] / acc[HEAD_DIM:HEAD_DIM + 1])
            o_ref[q_rows[qb], pair_lanes[pp]] = jnp.transpose(jnp.concatenate(outs, axis=0)).astype(o_ref.dtype)


def _moba_attention(qkv, near_bias):
    b, s, _ = qkv.shape
    blk = MOBA_BLOCK
    nb = s // blk
    nbp = -(-nb // SUBLANES) * SUBLANES
    width = PAIRS_PER_STEP * PAIR
    n_groups = D_ATTN // width
    qrows = QBLOCKS_PER_STEP * blk
    assert FAR_BLOCKS == 2 and QBLOCKS_PER_STEP == 2, "near / far chunk bookkeeping assumes pairs of blocks"
    assert s % (FAR_BLOCKS * blk) == 0 and s % qrows == 0 and nbp <= LANES and D_ATTN % width == 0
    kern = functools.partial(_attn_kernel, nb=nb)
    hps = HEADS_PER_STEP
    bias_tile = (hps, None, 2 * blk, blk)
    return pl.pallas_call(
        kern,
        grid=(b, n_groups, nb // QBLOCKS_PER_STEP),
        in_specs=[pl.BlockSpec((None, qrows, width), lambda bi, g, i: (bi, i, g)),
                  pl.BlockSpec((None, s, width), lambda bi, g, i: (bi, 0, n_groups + g)),
                  pl.BlockSpec((None, s, width), lambda bi, g, i: (bi, 0, 2 * n_groups + g)),
                  pl.BlockSpec(bias_tile, lambda bi, g, i: (g, jnp.minimum(i, 1), 0, 0)),
                  pl.BlockSpec(bias_tile, lambda bi, g, i: (g, 1, 0, 0))],
        out_specs=pl.BlockSpec((None, qrows, width), lambda bi, g, i: (bi, i, g)),
        out_shape=jax.ShapeDtypeStruct((b, s, D_ATTN), BF16),
        scratch_shapes=[pltpu.VMEM((PAIRS_PER_STEP, s, 2 * LANES), BF16),
                        pltpu.VMEM((hps, nb, VT_ROWS, blk), BF16),
                        pltpu.VMEM((nbp, width), F32),
                        pltpu.VMEM((STREAMS, blk, 2 * LANES), BF16),
                        pltpu.VMEM((STREAMS, blk, 2 * LANES), BF16),
                        pltpu.VMEM((2, STREAMS, FAR_BLOCKS * blk, blk), F32),
                        pltpu.VMEM((STREAMS, SUBLANES, blk), F32),
                        pltpu.VMEM((STREAMS, VT_ROWS, blk), F32)],
        compiler_params=pltpu.CompilerParams(
            dimension_semantics=("parallel", "parallel", "arbitrary"), vmem_limit_bytes=VMEM_LIMIT),
        name="moba_attn",
    )(qkv, qkv, qkv, near_bias, near_bias)


def _mix_kernel(x_ref, a_ref, u_ref, uh_ref, wpool_ref, pscale_ref, wout_ref, gffn_ref,
                wr_hi_ref, wr_lo_ref, br_ref,
                x2_ref, h2_ref, meta_ref, metat_ref, cnt_ref, ext_s, carry_s, *, tm, seq):
    t = pl.program_id(0)

    @pl.when(t == 0)
    def _init():
        carry_s[...] = jnp.zeros(carry_s.shape, F32)

    pos0 = (t * tm) % seq
    ext_s[0:HALO, :] = jnp.where(pos0 == 0, 0.0, uh_ref[...])
    ext_s[HALO:HALO + tm, :] = u_ref[...]
    pos = pos0 + lax.broadcasted_iota(jnp.int32, (tm, 1), 0)
    parts = []
    for gi, w in enumerate(POOL_WINDOWS):
        c0 = gi * POOL_GROUP_DIM
        cols = slice(c0, c0 + POOL_GROUP_DIM)
        tok = ext_s[HALO:HALO + tm, cols]
        win = tok
        for d in range(1, w):
            win = win + ext_s[HALO - d:HALO - d + tm, cols]
        cnt = jnp.minimum(pos + 1, w).astype(F32)
        pooled = win / cnt - tok
        parts.append(_dot(pooled.astype(BF16), wpool_ref[gi]) * pscale_ref[:, cols])
    b = jnp.concatenate(parts, axis=1).astype(BF16)

    mix = _dot(a_ref[...], wout_ref[:D_ATTN, :]) + _dot(b, wout_ref[D_ATTN:, :])
    x2 = x_ref[...] + mix
    x2_ref[...] = x2
    h2 = _rms(x2, gffn_ref[...])
    _store_token_tiles(h2_ref, h2)

    hi = h2.astype(BF16)
    lo = (h2 - hi.astype(F32)).astype(BF16)
    logits = jnp.transpose(_dot(hi, wr_hi_ref[...]) + _dot(lo, wr_hi_ref[...]) + _dot(hi, wr_lo_ref[...])
                           + br_ref[...])
    row = lax.broadcasted_iota(jnp.int32, (SUBLANES, tm), 0)
    lg = jnp.where(row < N_GROUPS, logits[0:SUBLANES], -jnp.inf)
    top_g = jnp.max(lg, axis=0, keepdims=True)
    g_idx = jnp.min(jnp.where(lg == top_g, row, SUBLANES), axis=0, keepdims=True)
    p_g = 1.0 / jnp.sum(jnp.exp(lg - top_g), axis=0, keepdims=True)
    l2 = logits[SUBLANES:2 * SUBLANES]
    for g in range(1, N_GROUPS):
        l2 = jnp.where(g_idx == g, logits[(g + 1) * SUBLANES:(g + 2) * SUBLANES], l2)
    v1 = jnp.max(l2, axis=0, keepdims=True)
    i1 = jnp.min(jnp.where(l2 == v1, row, SUBLANES), axis=0, keepdims=True)
    l2 = jnp.where(row == i1, -jnp.inf, l2)
    v2 = jnp.max(l2, axis=0, keepdims=True)
    i2 = jnp.min(jnp.where(l2 == v2, row, SUBLANES), axis=0, keepdims=True)
    e1 = g_idx * E_PER_GROUP + i1
    e2 = g_idx * E_PER_GROUP + i2
    t2 = jnp.exp(v2 - v1)
    w1 = 1.0 / (1.0 + t2)
    gate1 = p_g * w1
    gate2 = p_g * (t2 * w1)

    e_row = lax.broadcasted_iota(jnp.int32, (N_EXPERTS, tm), 0)
    oh1 = jnp.where(e_row == e1, 1.0, 0.0)
    oh2 = jnp.where(e_row == e2, 1.0, 0.0)
    oh = oh1 + oh2
    src = lax.broadcasted_iota(jnp.int32, (tm, tm), 0)
    dst = lax.broadcasted_iota(jnp.int32, (tm, tm), 1)
    earlier = jnp.where(src < dst, 1.0, 0.0).astype(BF16)
    seen = _dot(oh.astype(BF16), earlier) + carry_s[:, 0:1]
    r1 = jnp.sum(oh1 * seen, axis=0, keepdims=True)
    r2 = jnp.sum(oh2 * seen, axis=0, keepdims=True)
    carry_s[...] = carry_s[...] + jnp.sum(oh, axis=1, keepdims=True)
    cnt_ref[...] = carry_s[...]

    rows = (e1.astype(F32), e2.astype(F32), r1, r2, gate1, gate2)
    meta_t = jnp.zeros((SUBLANES, tm), F32)
    for k, val in enumerate(rows):
        meta_t = jnp.where(row == k, val, meta_t)
    metat_ref[...] = meta_t.astype(jnp.int32)
    meta_ref[...] = jnp.transpose(jnp.concatenate([meta_t, jnp.zeros((LANES - SUBLANES, tm), F32)], axis=0))


def _mix_route(x2d, a2d, u2d, w_pool, pool_scale, w_out, g_ffn, wr, br, seq):
    t, d = x2d.shape
    tm = TM
    assert seq % tm == 0 and tm % HALO == 0
    hb = tm // HALO
    kern = functools.partial(_mix_kernel, tm=tm, seq=seq)
    tok = lambda w: pl.BlockSpec((tm, w), lambda i: (i, 0))
    full = lambda *shape: pl.BlockSpec(shape, lambda i: (0,) * len(shape))
    wr_hi = wr.astype(BF16)
    wr_lo = (wr - wr_hi.astype(F32)).astype(BF16)
    return pl.pallas_call(
        kern,
        grid=(t // tm,),
        in_specs=[tok(d), tok(D_ATTN), tok(D_POOL),
                  pl.BlockSpec((HALO, D_POOL), lambda i: (jnp.maximum(i * hb - 1, 0), 0)),
                  full(len(POOL_WINDOWS), POOL_GROUP_DIM, POOL_GROUP_DIM), full(1, D_POOL),
                  full(D_ATTN + D_POOL, d), full(1, d), full(d, LANES), full(d, LANES), full(1, LANES)],
        out_specs=[tok(d), pl.BlockSpec((tm * TOKEN_ROWS, LANES), lambda i: (i, 0)), tok(LANES),
                   pl.BlockSpec((SUBLANES, tm), lambda i: (0, i)),
                   full(N_EXPERTS, LANES)],
        out_shape=[jax.ShapeDtypeStruct((t, d), F32), jax.ShapeDtypeStruct((t * TOKEN_ROWS, LANES), U32),
                   jax.ShapeDtypeStruct((t, LANES), F32),
                   jax.ShapeDtypeStruct((SUBLANES, t), jnp.int32),
                   jax.ShapeDtypeStruct((N_EXPERTS, LANES), F32)],
        scratch_shapes=[pltpu.VMEM((HALO + tm, D_POOL), F32), pltpu.VMEM((N_EXPERTS, LANES), F32)],
        compiler_params=pltpu.CompilerParams(dimension_semantics=("arbitrary",), vmem_limit_bytes=VMEM_LIMIT),
        name="mix_route",
    )(x2d, a2d, u2d, u2d, w_pool.astype(BF16), pool_scale.reshape(1, D_POOL).astype(F32),
      w_out.astype(BF16), g_ffn.reshape(1, d).astype(F32), wr_hi, wr_lo, br)


ISSUE_UNROLL = 8
COMBINE_CHUNKS = 2
EXPERT_CHUNKS = 2


def _pos_kernel(offs_ref, meta_ref, pos_ref):
    meta = meta_ref[...]
    start = jnp.zeros(meta.shape, jnp.int32)
    for e in range(N_EXPERTS):
        start = jnp.where(meta == e, offs_ref[e], start)
    pos_ref[...] = start[0:2, :] + meta[2:4, :]


def _moe_positions(offs, metat, tm):
    t = metat.shape[1]
    pos = pl.pallas_call(
        _pos_kernel,
        in_specs=[pl.BlockSpec(memory_space=pltpu.SMEM), pl.BlockSpec(memory_space=pltpu.VMEM)],
        out_specs=pl.BlockSpec(memory_space=pltpu.VMEM),
        out_shape=jax.ShapeDtypeStruct((2, t), jnp.int32),
        name="moe_pos",
    )(offs, metat)
    return pos.reshape(2, t // tm, tm).transpose(1, 0, 2).reshape(2 * t)


def _scatter_kernel(pos_ref, h2_ref, xs_ref, sem, *, tm):
    def issue(r, carry):
        for k in range(2):
            _token_copy(h2_ref, r, xs_ref, pos_ref[k * tm + r], sem).start(priority=k)
        return carry

    lax.fori_loop(0, tm, issue, 0, unroll=ISSUE_UNROLL)

    def drain(r, carry):
        for k in range(2):
            _token_copy(h2_ref, 0, xs_ref, 0, sem).wait()
        return carry

    lax.fori_loop(0, tm, drain, 0, unroll=ISSUE_UNROLL)


def _moe_scatter(pos, h2t):
    rows = h2t.shape[0]
    tm = TM
    kern = functools.partial(_scatter_kernel, tm=tm)
    return pl.pallas_call(
        kern,
        grid=(rows // (tm * TOKEN_ROWS),),
        in_specs=[pl.BlockSpec((2 * tm,), lambda i: (i,), memory_space=pltpu.SMEM),
                  pl.BlockSpec((tm * TOKEN_ROWS, LANES), lambda i: (i, 0))],
        out_specs=pl.BlockSpec(memory_space=pl.ANY),
        out_shape=jax.ShapeDtypeStruct((2 * rows, LANES), U32),
        scratch_shapes=[pltpu.SemaphoreType.DMA(())],
        compiler_params=pltpu.CompilerParams(dimension_semantics=("arbitrary",), vmem_limit_bytes=VMEM_LIMIT),
        name="moe_scatter",
    )(pos, h2t)


def _expert_kernel(tile_ref, exp_ref, lo_ref, hi_ref, next_ref, xs_ref, wg_hbm, wu_hbm, wd_hbm, ys_ref,
                   wg_f, wu_f, wd_f, wg_s, wu_s, wd_s, sem, *, tm):
    s = pl.program_id(0)
    lo = lo_ref[s]
    hi = hi_ref[s]
    new_expert = jnp.logical_or(s == 0, exp_ref[s] != exp_ref[jnp.maximum(s - 1, 0)])

    def weight_copies(e):
        return [pltpu.make_async_copy(src.at[e], dst, sem.at[k])
                for k, (src, dst) in enumerate(((wg_hbm, wg_f), (wu_hbm, wu_f), (wd_hbm, wd_f)))]

    @pl.when(s == 0)
    def _first_fetch():
        for cp in weight_copies(exp_ref[0]):
            cp.start()

    @pl.when(jnp.logical_and(hi > 0, new_expert))
    def _round_weights():
        for cp in weight_copies(exp_ref[s]):
            cp.wait()
        wg_s[...] = wg_f[...].astype(BF16)
        wu_s[...] = wu_f[...].astype(BF16)
        wd_s[...] = wd_f[...].astype(BF16)

        @pl.when(next_ref[s] >= 0)
        def _fetch_next():
            for cp in weight_copies(next_ref[s]):
                cp.start()

    n = tm // EXPERT_CHUNKS

    def expert_rows(first):
        x = _load_token_tiles(xs_ref, n, first).astype(BF16)
        a = _dot(x, wg_s[...])
        u = _dot(x, wu_s[...])
        hid = (a * jax.nn.sigmoid(a) * u).astype(BF16)
        return _dot(hid, wd_s[...])

    for ch in range(EXPERT_CHUNKS):
        first = ch * n
        active = jnp.logical_and(lo < first + n, hi > first)

        @pl.when(jnp.logical_and(active, lo <= first))
        def _whole_chunk():
            _store_token_tiles(ys_ref, expert_rows(first), first)

        @pl.when(jnp.logical_and(active, lo > first))
        def _chunk_tail():
            row = first + lax.broadcasted_iota(jnp.int32, (n, 1), 0)
            kept = _load_token_tiles(ys_ref, n, first)
            _store_token_tiles(ys_ref, jnp.where(row >= lo, expert_rows(first), kept), first)


def _moe_experts(step_tile, step_exp, step_lo, step_hi, step_next, xs, w_gate, w_up, w_down):
    n_rows = xs.shape[0]
    tm = TM_EXPERT
    _, d, f = w_gate.shape
    n_steps = step_tile.shape[0]
    kern = functools.partial(_expert_kernel, tm=tm)
    rows = lambda s, tl, ex, lo, hi, nx: (tl[s], 0)
    grid_spec = pltpu.PrefetchScalarGridSpec(
        num_scalar_prefetch=5,
        grid=(n_steps,),
        in_specs=[pl.BlockSpec((tm * TOKEN_ROWS, LANES), rows),
                  pl.BlockSpec(memory_space=pl.ANY), pl.BlockSpec(memory_space=pl.ANY),
                  pl.BlockSpec(memory_space=pl.ANY)],
        out_specs=pl.BlockSpec((tm * TOKEN_ROWS, LANES), rows),
        scratch_shapes=[pltpu.VMEM((d, f), F32), pltpu.VMEM((d, f), F32), pltpu.VMEM((f, d), F32),
                        pltpu.VMEM((d, f), BF16), pltpu.VMEM((d, f), BF16), pltpu.VMEM((f, d), BF16),
                        pltpu.SemaphoreType.DMA((3,))],
    )
    return pl.pallas_call(
        kern,
        grid_spec=grid_spec,
        out_shape=jax.ShapeDtypeStruct((n_rows, LANES), U32),
        compiler_params=pltpu.CompilerParams(dimension_semantics=("arbitrary",), vmem_limit_bytes=VMEM_LIMIT),
        name="moe_experts",
    )(step_tile, step_exp, step_lo, step_hi, step_next, xs, w_gate, w_up, w_down)


def _expert_steps(counts, n_rows, tm):
    n_e = counts.shape[0]
    n_tiles = n_rows // tm
    n_steps = n_tiles + n_e
    ends = jnp.cumsum(counts)
    starts = ends - counts
    first_tile = starts // tm
    last_tile = jnp.maximum(ends - 1, 0) // tm
    tiles_e = jnp.where(counts > 0, last_tile - first_tile + 1, 0)
    step_end = jnp.cumsum(tiles_e)
    step_start = step_end - tiles_e
    total = step_end[-1]
    s = jnp.arange(n_steps, dtype=jnp.int32)
    s_eff = jnp.minimum(s, total - 1)
    mine = ((step_start[None, :] <= s_eff[:, None]) & (s_eff[:, None] < step_end[None, :])).astype(jnp.int32)
    pick = lambda table: jnp.sum(mine * table[None, :], axis=1)
    exp = pick(jnp.arange(n_e, dtype=jnp.int32))
    tile = pick(first_tile) + (s_eff - pick(step_start))
    lo = jnp.maximum(pick(starts) - tile * tm, 0)
    hi = jnp.minimum(pick(ends) - tile * tm, tm)
    lo = jnp.where(s < total, lo, tm)
    hi = jnp.where(s < total, hi, 0)
    e_ids = jnp.arange(n_e, dtype=jnp.int32)
    later = (e_ids[None, :] > e_ids[:, None]) & (counts[None, :] > 0)
    next_e = jnp.min(jnp.where(later, e_ids[None, :], n_e), axis=1)
    nxt = pick(jnp.where(next_e < n_e, next_e, -1))
    return (tile.astype(jnp.int32), exp, lo.astype(jnp.int32), hi.astype(jnp.int32), nxt.astype(jnp.int32),
            starts.astype(jnp.int32))


def _combine_kernel(pos_ref, pos_next_ref, x2_ref, gates_ref, p_ref, wproj_ref, wgate_ref, gple_ref, gfin_ref,
                    ys_ref, out_ref, rows_s, sem, *, tm):
    i = pl.program_id(0)
    slot = i % 2

    def gather(p_ref, sl):
        def issue(r, carry):
            for k in range(2):
                _token_copy(ys_ref, p_ref[k * tm + r], rows_s.at[sl, k], r, sem.at[sl]).start(priority=k)
            return carry

        lax.fori_loop(0, tm, issue, 0, unroll=ISSUE_UNROLL)

    @pl.when(i == 0)
    def _first_tile():
        gather(pos_ref, 0)

    @pl.when(i + 1 < pl.num_programs(0))
    def _next_tile():
        gather(pos_next_ref, 1 - slot)

    def drain(r, carry):
        for k in range(2):
            _token_copy(ys_ref, 0, rows_s.at[slot, k], 0, sem.at[slot]).wait()
        return carry

    lax.fori_loop(0, tm, drain, 0, unroll=ISSUE_UNROLL)

    n = tm // COMBINE_CHUNKS
    for ch in range(COMBINE_CHUNKS):
        rows = slice(ch * n, (ch + 1) * n)
        gates = gates_ref[rows, :]
        y = (gates[:, 4:5] * _load_token_tiles(rows_s.at[slot, 0], n, ch * n)
             + gates[:, 5:6] * _load_token_tiles(rows_s.at[slot, 1], n, ch * n))
        x3 = x2_ref[rows, :] + y
        h3 = _rms(x3, gple_ref[...]).astype(BF16)
        gate = jax.nn.sigmoid(_dot(h3, wgate_ref[...]))
        pe = _dot(p_ref[rows, :].astype(BF16), wproj_ref[...])
        x4 = x3 + pe * gate
        out_ref[rows, :] = _rms(x4, gfin_ref[...])


def _combine_ple(pos, x2, meta, p2d, w_ple_proj, w_ple_gate, g_ple, g_final, ys):
    t, d = x2.shape
    d_ple = p2d.shape[1]
    tm = TM
    n_tiles = t // tm
    kern = functools.partial(_combine_kernel, tm=tm)
    tok = lambda w: pl.BlockSpec((tm, w), lambda i: (i, 0))
    full = lambda *shape: pl.BlockSpec(shape, lambda i: (0,) * len(shape))
    return pl.pallas_call(
        kern,
        grid=(t // tm,),
        in_specs=[pl.BlockSpec((2 * tm,), lambda i: (i,), memory_space=pltpu.SMEM),
                  pl.BlockSpec((2 * tm,), lambda i: (jnp.minimum(i + 1, n_tiles - 1),), memory_space=pltpu.SMEM),
                  tok(d), tok(LANES), tok(d_ple), full(d_ple, d), full(d, d), full(1, d), full(1, d),
                  pl.BlockSpec(memory_space=pl.ANY)],
        out_specs=tok(d),
        out_shape=jax.ShapeDtypeStruct((t, d), F32),
        scratch_shapes=[pltpu.VMEM((2, 2, tm * TOKEN_ROWS, LANES), U32), pltpu.SemaphoreType.DMA((2,))],
        compiler_params=pltpu.CompilerParams(dimension_semantics=("arbitrary",), vmem_limit_bytes=VMEM_LIMIT),
        name="combine_ple",
    )(pos, pos, x2, meta, p2d, w_ple_proj.astype(BF16), w_ple_gate.astype(BF16),
      g_ple.reshape(1, d).astype(F32), g_final.reshape(1, d).astype(F32), ys)


def _layer(x2d, p2d, seq, g_mix, w_in, w_pool, pool_scale, w_out, near_bias, g_ffn,
           w_r1, b_r1, w_r2, b_r2, w_gate, w_up, w_down, g_ple, w_ple_proj, w_ple_gate, g_final):
    t, d = x2d.shape
    batch = t // seq
    qkv, u = _in_proj(x2d, g_mix, w_in)
    a = _moba_attention(qkv.reshape(batch, seq, 3 * D_ATTN), near_bias).reshape(t, D_ATTN)

    assert N_GROUPS <= SUBLANES and E_PER_GROUP == SUBLANES
    pad_g = SUBLANES - N_GROUPS
    pad_e = LANES - SUBLANES - N_EXPERTS
    wr = jnp.concatenate([w_r1, jnp.zeros((d, pad_g), F32),
                          jnp.transpose(w_r2, (1, 0, 2)).reshape(d, N_EXPERTS),
                          jnp.zeros((d, pad_e), F32)], axis=1).astype(F32)
    br = jnp.concatenate([b_r1, jnp.zeros((pad_g,), F32), b_r2.reshape(N_EXPERTS), jnp.zeros((pad_e,), F32)])
    br = br.astype(F32).reshape(1, LANES)
    assert d == 2 * TOKEN_ROWS * LANES, "token tiles hold d/2 words in TOKEN_ROWS rows of 128"
    x2, h2t, meta, metat, cnt = _mix_route(x2d, a, u, w_pool, pool_scale, w_out, g_ffn, wr, br, seq)

    counts = cnt[:, 0].astype(jnp.int32)
    step_tile, step_exp, step_lo, step_hi, step_next, offs = _expert_steps(counts, 2 * t, TM_EXPERT)
    pos = _moe_positions(offs, metat, TM)
    xs = _moe_scatter(pos, h2t)
    f = w_gate.shape[-1]
    ys = _moe_experts(step_tile, step_exp, step_lo, step_hi, step_next, xs,
                      w_gate.reshape(N_EXPERTS, d, f), w_up.reshape(N_EXPERTS, d, f),
                      w_down.reshape(N_EXPERTS, f, d))
    return _combine_ple(pos, x2, meta, p2d, w_ple_proj, w_ple_gate, g_ple, g_final, ys)


def kernel(x, p, g_mix, w_in, w_pool, pool_scale, w_out, rel_bias, g_ffn, w_r1, b_r1, w_r2, b_r2,
           w_gate, w_up, w_down, g_ple, w_ple_proj, w_ple_gate, g_final):
    batch, seq, d = x.shape
    depth = p.shape[0]
    assert depth == 1, "the final norm is fused into the last stage of a single layer"
    near_bias = _bias_tables(rel_bias, seq)
    out = _layer(x.reshape(batch * seq, d), p[0].reshape(batch * seq, -1), seq,
                 g_mix[0], w_in[0], w_pool[0], pool_scale[0], w_out[0], near_bias, g_ffn[0],
                 w_r1[0], b_r1[0], w_r2[0], b_r2[0], w_gate[0], w_up[0], w_down[0],
                 g_ple[0], w_ple_proj[0], w_ple_gate[0], g_final)
    return out.reshape(batch, seq, d)
```

```python
import functools
import math

import numpy as np
import jax
import jax.numpy as jnp
from jax import lax
from jax.experimental import pallas as pl
from jax.experimental.pallas import tpu as pltpu

F32 = jnp.float32
BF16 = jnp.bfloat16

HEAD_DIM = 64
N_HEADS = 8
D_ATTN = N_HEADS * HEAD_DIM
POOL_WINDOWS = (2, 4, 8, 16)
POOL_GROUP_DIM = 128
D_POOL = POOL_GROUP_DIM * len(POOL_WINDOWS)
MOBA_BLOCK = 256
MOBA_TOPK = 3
REL_BUCKETS = 32
REL_MAX_DIST = 128
N_GROUPS = 4
E_PER_GROUP = 8
N_EXPERTS = N_GROUPS * E_PER_GROUP
RMS_EPS = 1e-6

LANES = 128
SUBLANES = 8
V7X_VMEM_BYTES = 64 * 1024 * 1024
VMEM_LIMIT = V7X_VMEM_BYTES // 8 * 7

MASK_VALUE = -1e30
LOG2E = math.log2(math.e)
HALO = max(POOL_WINDOWS)
TM_PROJ = 1024
TM_MIX = 1024
TM_SCATTER = 1024
TM_COMBINE = 512
TM_EXPERT = 512
PAIR = 2 * HEAD_DIM


def _rms(x, g):
    return x * lax.rsqrt(jnp.mean(x * x, axis=-1, keepdims=True) + RMS_EPS) * g


def _dot(a, b):
    return jnp.dot(a, b, preferred_element_type=F32)


def _dot_nt(a, b):
    return lax.dot_general(a, b, (((1,), (1,)), ((), ())), preferred_element_type=F32)


TOKEN_ROWS = 4
U32 = jnp.uint32
HIGH_HALF = 0xFFFF0000


def _pack_bf16_pairs(x):
    half = x.shape[1] // 2
    bits = lambda v: lax.bitcast_convert_type(v.astype(BF16).astype(F32), U32)
    return (bits(x[:, :half]) >> 16) | (bits(x[:, half:]) & U32(HIGH_HALF))


def _unpack_bf16_pairs(w):
    low = lax.bitcast_convert_type(w << 16, F32)
    high = lax.bitcast_convert_type(w & U32(HIGH_HALF), F32)
    return jnp.concatenate([low, high], axis=1)


def _store_token_tiles(ref, x, first=0):
    n = x.shape[0]
    w = _pack_bf16_pairs(x)
    for c in range(TOKEN_ROWS):
        ref[pl.ds(first * TOKEN_ROWS + c, n, stride=TOKEN_ROWS), :] = w[:, c * LANES:(c + 1) * LANES]


def _load_token_tiles(ref, n, first=0):
    w = jnp.concatenate([ref[pl.ds(first * TOKEN_ROWS + c, n, stride=TOKEN_ROWS), :] for c in range(TOKEN_ROWS)],
                        axis=1)
    return _unpack_bf16_pairs(w)


def _token_copy(src_ref, src_tok, dst_ref, dst_tok, sem):
    src = src_ref.at[pl.ds(pl.multiple_of(src_tok * TOKEN_ROWS, TOKEN_ROWS), TOKEN_ROWS)]
    dst = dst_ref.at[pl.ds(pl.multiple_of(dst_tok * TOKEN_ROWS, TOKEN_ROWS), TOKEN_ROWS)]
    return pltpu.make_async_copy(src, dst, sem)


def _rel_bucket_np(n):
    n = np.maximum(n, 0)
    max_exact = REL_BUCKETS // 2
    nf = np.maximum(n, 1).astype(np.float32)
    large = max_exact + (np.log(nf / np.float32(max_exact)) / np.float32(math.log(REL_MAX_DIST / max_exact))
                         * np.float32(REL_BUCKETS - max_exact)).astype(np.int32)
    large = np.minimum(large, REL_BUCKETS - 1)
    return np.where(n < max_exact, n, large).astype(np.int32)


def _bucket_tiles(seq):
    r = np.arange(MOBA_BLOCK)
    dist = r[None, :] - r[:, None]
    own = np.where(dist >= 0, _rel_bucket_np(dist), -1).astype(np.int32)
    prev = _rel_bucket_np(dist + MOBA_BLOCK)
    far = _rel_bucket_np(np.arange(MOBA_BLOCK + 1, max(seq, MOBA_BLOCK + 2)))
    assert np.all(far == REL_BUCKETS - 1)
    return np.stack([own, prev])


def _bias_kernel(rb_ref, bucket_ref, near_ref):
    h = pl.program_id(0)
    far = rb_ref[REL_BUCKETS - 1, h]
    tiles = []
    for which in range(2):
        bucket = bucket_ref[which]
        tile = jnp.where(bucket < 0, MASK_VALUE, 0.0).astype(F32)
        for b in range(REL_BUCKETS):
            tile = jnp.where(bucket == b, (rb_ref[b, h] - far) * LOG2E, tile)
        tiles.append(tile)
    own, prev = tiles
    blk = own.shape[0]
    near_ref[0, :blk, :] = own
    near_ref[0, blk:, :] = jnp.full(own.shape, MASK_VALUE, F32)
    near_ref[1, :blk, :] = prev
    near_ref[1, blk:, :] = own


def _bias_tables(rel_bias, seq):
    buckets = _bucket_tiles(seq)
    blk = MOBA_BLOCK
    return pl.pallas_call(
        _bias_kernel,
        grid=(N_HEADS,),
        in_specs=[pl.BlockSpec(memory_space=pltpu.SMEM),
                  pl.BlockSpec((2, blk, blk), lambda h: (0, 0, 0))],
        out_specs=pl.BlockSpec((None, 2, 2 * blk, blk), lambda h: (h, 0, 0, 0)),
        out_shape=jax.ShapeDtypeStruct((N_HEADS, 2, 2 * blk, blk), F32),
        name="bias_tables",
    )(rel_bias.astype(F32), jnp.asarray(buckets))


def _inproj_kernel(x_ref, g_ref, w_ref, qkv_ref, u_ref):
    hb = _rms(x_ref[...], g_ref[...]).astype(BF16)
    scale = HEAD_DIM ** -0.5 * LOG2E
    qkv_ref[:, :D_ATTN] = (_dot(hb, w_ref[:, :D_ATTN]) * scale).astype(BF16)
    qkv_ref[:, D_ATTN:2 * D_ATTN] = _dot(hb, w_ref[:, D_ATTN:2 * D_ATTN]).astype(BF16)
    qkv_ref[:, 2 * D_ATTN:] = _dot(hb, w_ref[:, 2 * D_ATTN:3 * D_ATTN]).astype(BF16)
    u_ref[...] = _dot(hb, w_ref[:, 3 * D_ATTN:])


def _in_proj(x2d, g_mix, w_in):
    t, d = x2d.shape
    n_in = w_in.shape[1]
    tm = min(TM_PROJ, t)
    return pl.pallas_call(
        _inproj_kernel,
        grid=(t // tm,),
        in_specs=[pl.BlockSpec((tm, d), lambda i: (i, 0)),
                  pl.BlockSpec((1, d), lambda i: (0, 0)),
                  pl.BlockSpec((d, n_in), lambda i: (0, 0))],
        out_specs=[pl.BlockSpec((tm, 3 * D_ATTN), lambda i: (i, 0)),
                   pl.BlockSpec((tm, D_POOL), lambda i: (i, 0))],
        out_shape=[jax.ShapeDtypeStruct((t, 3 * D_ATTN), BF16),
                   jax.ShapeDtypeStruct((t, D_POOL), F32)],
        compiler_params=pltpu.CompilerParams(dimension_semantics=("parallel",), vmem_limit_bytes=VMEM_LIMIT),
        name="in_proj",
    )(x2d, g_mix.reshape(1, d).astype(F32), w_in.astype(BF16))


FAR_BLOCKS = 2
PAIRS_PER_STEP = 2
HEADS_PER_STEP = 2 * PAIRS_PER_STEP
QBLOCKS_PER_STEP = 2
STREAMS = QBLOCKS_PER_STEP * HEADS_PER_STEP
ONES_ROWS = 16
VT_ROWS = HEAD_DIM + ONES_ROWS


def _attn_kernel(q_ref, k_ref, v_ref, nearb0_ref, nearb1_ref, o_ref,
                 kaug, vt, kmean, qnear, qfar, sbuf, m_s, acc_s, *, nb):
    blk = MOBA_BLOCK
    step = pl.program_id(2)
    pair_lanes = [slice(pp * LANES, (pp + 1) * LANES) for pp in range(PAIRS_PER_STEP)]
    q_rows = [slice(qb * blk, (qb + 1) * blk) for qb in range(QBLOCKS_PER_STEP)]
    near_bias = (nearb0_ref, nearb1_ref)

    @pl.when(step == 0)
    def _build_keys():
        kmean[...] = jnp.zeros(kmean.shape, F32)
        lane_b = lax.broadcasted_iota(jnp.int32, (blk, LANES), 1)

        def body(j, carry):
            rows = pl.ds(pl.multiple_of(j * blk, blk), blk)
            kmean[pl.ds(j, 1), :] = jnp.mean(k_ref[rows, :].astype(F32), axis=0, keepdims=True)
            for pp in range(PAIRS_PER_STEP):
                kaug[pp, rows, :LANES] = k_ref[rows, pair_lanes[pp]]
                kaug[pp, rows, LANES:] = jnp.where(lane_b == j, 1.0, 0.0).astype(BF16)
                v_t = jnp.transpose(v_ref[rows, pair_lanes[pp]].astype(F32))
                for half in range(2):
                    hh = 2 * pp + half
                    vt[hh, j, :HEAD_DIM, :] = v_t[half * HEAD_DIM:(half + 1) * HEAD_DIM, :].astype(BF16)
                    vt[hh, j, HEAD_DIM:, :] = jnp.ones((ONES_ROWS, blk), BF16)
            return carry

        lax.fori_loop(0, nb, body, 0)

    lane = lax.broadcasted_iota(jnp.int32, (blk, LANES), 1)
    nbp = kmean.shape[0]
    blk_id = lax.broadcasted_iota(jnp.int32, (nbp, blk), 0)

    for st in range(STREAMS):
        qb, hh = divmod(st, HEADS_PER_STEP)
        pp, half = divmod(hh, 2)
        i = QBLOCKS_PER_STEP * step + qb
        q = q_ref[q_rows[qb], pair_lanes[pp]]
        in_head = (lane < HEAD_DIM) if half == 0 else (lane >= HEAD_DIM)
        qm = jnp.where(in_head, q, jnp.zeros_like(q))
        kmean_b = kmean[:, pair_lanes[pp]].astype(BF16)
        gate = jnp.where(blk_id < i, _dot_nt(kmean_b, qm), -jnp.inf)
        chosen_t = jnp.zeros((nbp, blk), F32)
        for _ in range(MOBA_TOPK):
            top = jnp.max(gate, axis=0, keepdims=True)
            idx = jnp.min(jnp.where(gate == top, blk_id, nbp), axis=0, keepdims=True)
            hit = blk_id == idx
            chosen_t = jnp.where(hit, 1.0, chosen_t)
            gate = jnp.where(hit, -jnp.inf, gate)
        if nbp < LANES:
            chosen_t = jnp.concatenate([chosen_t, jnp.zeros((LANES - nbp, blk), F32)], axis=0)
        penalty = jnp.where(jnp.transpose(chosen_t) > 0.0, 0.0, MASK_VALUE)
        qnear[st, :, :LANES] = qm
        qnear[st, :, LANES:] = jnp.where(lane < i, penalty, 0.0).astype(BF16)
        qfar[st, :, :LANES] = qm
        qfar[st, :, LANES:] = jnp.where(lane < i - 1, penalty, MASK_VALUE).astype(BF16)

    def values_t(hh, first_block):
        return jnp.concatenate([vt[hh, first_block + k] for k in range(FAR_BLOCKS)], axis=1)

    def fold(st, s_t, v_t, first):
        m_blk = jnp.max(s_t, axis=0, keepdims=True)
        if first:
            m_new = m_blk
        else:
            m_old = m_s[st][0:1]
            m_new = jnp.maximum(m_old, m_blk)
            alpha = jnp.exp2(m_old - m_new)
        p_t = jnp.exp2(s_t - m_new)
        pv = _dot(v_t, p_t.astype(BF16))
        acc_s[st] = pv if first else alpha * acc_s[st] + pv
        m_s[st] = jnp.broadcast_to(m_new, m_s.shape[1:])

    near_blocks = [jnp.maximum(QBLOCKS_PER_STEP * step + qb - 1, 0) for qb in range(QBLOCKS_PER_STEP)]
    near_scores = []
    for st in range(STREAMS):
        qb, hh = divmod(st, HEADS_PER_STEP)
        near = pl.ds(pl.multiple_of(near_blocks[qb] * blk, blk), 2 * blk)
        near_scores.append(_dot_nt(kaug[hh // 2, near, :], qnear[st]) + near_bias[qb][hh])

    chunk = FAR_BLOCKS * blk
    last_chunk = kaug.shape[1] // chunk - 1
    n_far = (QBLOCKS_PER_STEP * step + QBLOCKS_PER_STEP - 1 + FAR_BLOCKS - 2) // FAR_BLOCKS

    def far_scores(c, slot):
        rows = pl.ds(pl.multiple_of(jnp.minimum(c, last_chunk) * chunk, chunk), chunk)
        for st in range(STREAMS):
            sbuf[slot, st] = _dot_nt(kaug[(st % HEADS_PER_STEP) // 2, rows, :], qfar[st])

    def far_fold(c, slot):
        first_block = jnp.minimum(c, last_chunk) * FAR_BLOCKS
        for st in range(STREAMS):
            fold(st, sbuf[slot, st], values_t(st % HEADS_PER_STEP, first_block), False)

    far_scores(0, 0)
    for st in range(STREAMS):
        qb, hh = divmod(st, HEADS_PER_STEP)
        fold(st, near_scores[st], values_t(hh, near_blocks[qb]), True)

    def far_body(cc, carry):
        c = 2 * cc
        far_scores(c + 1, 1)
        far_fold(c, 0)
        far_scores(c + 2, 0)
        far_fold(c + 1, 1)
        return carry

    lax.fori_loop(0, n_far // 2, far_body, 0)

    @pl.when(n_far % 2 == 1)
    def _last_chunk():
        far_fold(n_far - 1, 0)

    for qb in range(QBLOCKS_PER_STEP):
        for pp in range(PAIRS_PER_STEP):
            outs = []
            for hh in (2 * pp, 2 * pp + 1):
                acc = acc_s[qb * HEADS_PER_STEP + hh]
                outs.append(acc[:HEAD_DIM] / acc[HEAD_DIM:HEAD_DIM + 1])
            o_ref[q_rows[qb], pair_lanes[pp]] = jnp.transpose(jnp.concatenate(outs, axis=0)).astype(o_ref.dtype)


def _moba_attention(qkv, near_bias):
    b, s, _ = qkv.shape
    blk = MOBA_BLOCK
    nb = s // blk
    nbp = -(-nb // SUBLANES) * SUBLANES
    width = PAIRS_PER_STEP * PAIR
    n_groups = D_ATTN // width
    qrows = QBLOCKS_PER_STEP * blk
    assert FAR_BLOCKS == 2 and QBLOCKS_PER_STEP == 2, "near / far chunk bookkeeping assumes pairs of blocks"
    assert s % (FAR_BLOCKS * blk) == 0 and s % qrows == 0 and nbp <= LANES and D_ATTN % width == 0
    kern = functools.partial(_attn_kernel, nb=nb)
    hps = HEADS_PER_STEP
    bias_tile = (hps, None, 2 * blk, blk)
    return pl.pallas_call(
        kern,
        grid=(b, n_groups, nb // QBLOCKS_PER_STEP),
        in_specs=[pl.BlockSpec((None, qrows, width), lambda bi, g, i: (bi, i, g)),
                  pl.BlockSpec((None, s, width), lambda bi, g, i: (bi, 0, n_groups + g)),
                  pl.BlockSpec((None, s, width), lambda bi, g, i: (bi, 0, 2 * n_groups + g)),
                  pl.BlockSpec(bias_tile, lambda bi, g, i: (g, jnp.minimum(i, 1), 0, 0)),
                  pl.BlockSpec(bias_tile, lambda bi, g, i: (g, 1, 0, 0))],
        out_specs=pl.BlockSpec((None, qrows, width), lambda bi, g, i: (bi, i, g)),
        out_shape=jax.ShapeDtypeStruct((b, s, D_ATTN), BF16),
        scratch_shapes=[pltpu.VMEM((PAIRS_PER_STEP, s, 2 * LANES), BF16),
                        pltpu.VMEM((hps, nb, VT_ROWS, blk), BF16),
                        pltpu.VMEM((nbp, width), F32),
                        pltpu.VMEM((STREAMS, blk, 2 * LANES), BF16),
                        pltpu.VMEM((STREAMS, blk, 2 * LANES), BF16),
                        pltpu.VMEM((2, STREAMS, FAR_BLOCKS * blk, blk), F32),
                        pltpu.VMEM((STREAMS, SUBLANES, blk), F32),
                        pltpu.VMEM((STREAMS, VT_ROWS, blk), F32)],
        compiler_params=pltpu.CompilerParams(
            dimension_semantics=("parallel", "parallel", "arbitrary"), vmem_limit_bytes=VMEM_LIMIT),
        name="moba_attn",
    )(qkv, qkv, qkv, near_bias, near_bias)


def _mix_kernel(x_ref, a_ref, u_ref, uh_ref, wpool_ref, pscale_ref, wout_ref, gffn_ref,
                wr_hi_ref, wr_lo_ref, br_ref,
                x2_ref, h2_ref, meta_ref, metat_ref, cnt_ref, ext_s, carry_s, *, tm, seq):
    t = pl.program_id(0)

    @pl.when(t == 0)
    def _init():
        carry_s[...] = jnp.zeros(carry_s.shape, F32)

    pos0 = (t * tm) % seq
    ext_s[0:HALO, :] = jnp.where(pos0 == 0, 0.0, uh_ref[...])
    ext_s[HALO:HALO + tm, :] = u_ref[...]
    pos = pos0 + lax.broadcasted_iota(jnp.int32, (tm, 1), 0)
    parts = []
    for gi, w in enumerate(POOL_WINDOWS):
        c0 = gi * POOL_GROUP_DIM
        cols = slice(c0, c0 + POOL_GROUP_DIM)
        tok = ext_s[HALO:HALO + tm, cols]
        win = tok
        for d in range(1, w):
            win = win + ext_s[HALO - d:HALO - d + tm, cols]
        cnt = jnp.minimum(pos + 1, w).astype(F32)
        pooled = win / cnt - tok
        parts.append(_dot(pooled.astype(BF16), wpool_ref[gi]) * pscale_ref[:, cols])
    b = jnp.concatenate(parts, axis=1).astype(BF16)

    mix = _dot(a_ref[...], wout_ref[:D_ATTN, :]) + _dot(b, wout_ref[D_ATTN:, :])
    x2 = x_ref[...] + mix
    x2_ref[...] = x2
    h2 = _rms(x2, gffn_ref[...])
    _store_token_tiles(h2_ref, h2)

    hi = h2.astype(BF16)
    lo = (h2 - hi.astype(F32)).astype(BF16)
    logits = jnp.transpose(_dot(hi, wr_hi_ref[...]) + _dot(lo, wr_hi_ref[...]) + _dot(hi, wr_lo_ref[...])
                           + br_ref[...])
    row = lax.broadcasted_iota(jnp.int32, (SUBLANES, tm), 0)
    lg = jnp.where(row < N_GROUPS, logits[0:SUBLANES], -jnp.inf)
    top_g = jnp.max(lg, axis=0, keepdims=True)
    g_idx = jnp.min(jnp.where(lg == top_g, row, SUBLANES), axis=0, keepdims=True)
    p_g = 1.0 / jnp.sum(jnp.exp(lg - top_g), axis=0, keepdims=True)
    l2 = logits[SUBLANES:2 * SUBLANES]
    for g in range(1, N_GROUPS):
        l2 = jnp.where(g_idx == g, logits[(g + 1) * SUBLANES:(g + 2) * SUBLANES], l2)
    v1 = jnp.max(l2, axis=0, keepdims=True)
    i1 = jnp.min(jnp.where(l2 == v1, row, SUBLANES), axis=0, keepdims=True)
    l2 = jnp.where(row == i1, -jnp.inf, l2)
    v2 = jnp.max(l2, axis=0, keepdims=True)
    i2 = jnp.min(jnp.where(l2 == v2, row, SUBLANES), axis=0, keepdims=True)
    e1 = g_idx * E_PER_GROUP + i1
    e2 = g_idx * E_PER_GROUP + i2
    t2 = jnp.exp(v2 - v1)
    w1 = 1.0 / (1.0 + t2)
    gate1 = p_g * w1
    gate2 = p_g * (t2 * w1)

    e_row = lax.broadcasted_iota(jnp.int32, (N_EXPERTS, tm), 0)
    oh1 = jnp.where(e_row == e1, 1.0, 0.0)
    oh2 = jnp.where(e_row == e2, 1.0, 0.0)
    oh = oh1 + oh2
    src = lax.broadcasted_iota(jnp.int32, (tm, tm), 0)
    dst = lax.broadcasted_iota(jnp.int32, (tm, tm), 1)
    earlier = jnp.where(src < dst, 1.0, 0.0).astype(BF16)
    seen = _dot(oh.astype(BF16), earlier) + carry_s[:, 0:1]
    r1 = jnp.sum(oh1 * seen, axis=0, keepdims=True)
    r2 = jnp.sum(oh2 * seen, axis=0, keepdims=True)
    carry_s[...] = carry_s[...] + jnp.sum(oh, axis=1, keepdims=True)
    cnt_ref[...] = carry_s[...]

    rows = (e1.astype(F32), e2.astype(F32), r1, r2, gate1, gate2)
    meta_t = jnp.zeros((SUBLANES, tm), F32)
    for k, val in enumerate(rows):
        meta_t = jnp.where(row == k, val, meta_t)
    metat_ref[...] = meta_t.astype(jnp.int32)
    meta_ref[...] = jnp.transpose(jnp.concatenate([meta_t, jnp.zeros((LANES - SUBLANES, tm), F32)], axis=0))


def _mix_route(x2d, a2d, u2d, w_pool, pool_scale, w_out, g_ffn, wr, br, seq):
    t, d = x2d.shape
    tm = TM_MIX
    assert seq % tm == 0 and tm % HALO == 0
    hb = tm // HALO
    kern = functools.partial(_mix_kernel, tm=tm, seq=seq)
    tok = lambda w: pl.BlockSpec((tm, w), lambda i: (i, 0))
    full = lambda *shape: pl.BlockSpec(shape, lambda i: (0,) * len(shape))
    wr_hi = wr.astype(BF16)
    wr_lo = (wr - wr_hi.astype(F32)).astype(BF16)
    return pl.pallas_call(
        kern,
        grid=(t // tm,),
        in_specs=[tok(d), tok(D_ATTN), tok(D_POOL),
                  pl.BlockSpec((HALO, D_POOL), lambda i: (jnp.maximum(i * hb - 1, 0), 0)),
                  full(len(POOL_WINDOWS), POOL_GROUP_DIM, POOL_GROUP_DIM), full(1, D_POOL),
                  full(D_ATTN + D_POOL, d), full(1, d), full(d, LANES), full(d, LANES), full(1, LANES)],
        out_specs=[tok(d), pl.BlockSpec((tm * TOKEN_ROWS, LANES), lambda i: (i, 0)), tok(LANES),
                   pl.BlockSpec((SUBLANES, tm), lambda i: (0, i)),
                   full(N_EXPERTS, LANES)],
        out_shape=[jax.ShapeDtypeStruct((t, d), F32), jax.ShapeDtypeStruct((t * TOKEN_ROWS, LANES), U32),
                   jax.ShapeDtypeStruct((t, LANES), F32),
                   jax.ShapeDtypeStruct((SUBLANES, t), jnp.int32),
                   jax.ShapeDtypeStruct((N_EXPERTS, LANES), F32)],
        scratch_shapes=[pltpu.VMEM((HALO + tm, D_POOL), F32), pltpu.VMEM((N_EXPERTS, LANES), F32)],
        compiler_params=pltpu.CompilerParams(dimension_semantics=("arbitrary",), vmem_limit_bytes=VMEM_LIMIT),
        name="mix_route",
    )(x2d, a2d, u2d, u2d, w_pool.astype(BF16), pool_scale.reshape(1, D_POOL).astype(F32),
      w_out.astype(BF16), g_ffn.reshape(1, d).astype(F32), wr_hi, wr_lo, br)


ISSUE_UNROLL = 8
COMBINE_CHUNKS = 2
EXPERT_CHUNKS = 2


def _pos_kernel(offs_ref, meta_ref, pos_ref):
    meta = meta_ref[...]
    start = jnp.zeros(meta.shape, jnp.int32)
    for e in range(N_EXPERTS):
        start = jnp.where(meta == e, offs_ref[e], start)
    pos_ref[...] = start[0:2, :] + meta[2:4, :]


def _moe_positions(offs, metat):
    t = metat.shape[1]
    return pl.pallas_call(
        _pos_kernel,
        in_specs=[pl.BlockSpec(memory_space=pltpu.SMEM), pl.BlockSpec(memory_space=pltpu.VMEM)],
        out_specs=pl.BlockSpec(memory_space=pltpu.VMEM),
        out_shape=jax.ShapeDtypeStruct((2, t), jnp.int32),
        name="moe_pos",
    )(offs, metat)


def _tile_positions(pos, tm):
    t = pos.shape[1]
    return pos.reshape(2, t // tm, tm).transpose(1, 0, 2).reshape(2 * t)


def _scatter_kernel(pos_ref, h2_ref, xs_ref, sem, *, tm):
    def issue(r, carry):
        for k in range(2):
            _token_copy(h2_ref, r, xs_ref, pos_ref[k * tm + r], sem).start(priority=k)
        return carry

    lax.fori_loop(0, tm, issue, 0, unroll=ISSUE_UNROLL)

    def drain(r, carry):
        for k in range(2):
            _token_copy(h2_ref, 0, xs_ref, 0, sem).wait()
        return carry

    lax.fori_loop(0, tm, drain, 0, unroll=ISSUE_UNROLL)


def _moe_scatter(pos, h2t):
    rows = h2t.shape[0]
    tm = TM_SCATTER
    kern = functools.partial(_scatter_kernel, tm=tm)
    return pl.pallas_call(
        kern,
        grid=(rows // (tm * TOKEN_ROWS),),
        in_specs=[pl.BlockSpec((2 * tm,), lambda i: (i,), memory_space=pltpu.SMEM),
                  pl.BlockSpec((tm * TOKEN_ROWS, LANES), lambda i: (i, 0))],
        out_specs=pl.BlockSpec(memory_space=pl.ANY),
        out_shape=jax.ShapeDtypeStruct((2 * rows, LANES), U32),
        scratch_shapes=[pltpu.SemaphoreType.DMA(())],
        compiler_params=pltpu.CompilerParams(dimension_semantics=("arbitrary",), vmem_limit_bytes=VMEM_LIMIT),
        name="moe_scatter",
    )(pos, h2t)


def _expert_kernel(tile_ref, exp_ref, lo_ref, hi_ref, next_ref, xs_ref, wg_hbm, wu_hbm, wd_hbm, ys_ref,
                   wg_f, wu_f, wd_f, wg_s, wu_s, wd_s, sem, *, tm):
    s = pl.program_id(0)
    lo = lo_ref[s]
    hi = hi_ref[s]
    new_expert = jnp.logical_or(s == 0, exp_ref[s] != exp_ref[jnp.maximum(s - 1, 0)])

    def weight_copies(e):
        return [pltpu.make_async_copy(src.at[e], dst, sem.at[k])
                for k, (src, dst) in enumerate(((wg_hbm, wg_f), (wu_hbm, wu_f), (wd_hbm, wd_f)))]

    @pl.when(s == 0)
    def _first_fetch():
        for cp in weight_copies(exp_ref[0]):
            cp.start()

    @pl.when(jnp.logical_and(hi > 0, new_expert))
    def _round_weights():
        for cp in weight_copies(exp_ref[s]):
            cp.wait()
        wg_s[...] = wg_f[...].astype(BF16)
        wu_s[...] = wu_f[...].astype(BF16)
        wd_s[...] = wd_f[...].astype(BF16)

        @pl.when(next_ref[s] >= 0)
        def _fetch_next():
            for cp in weight_copies(next_ref[s]):
                cp.start()

    n = tm // EXPERT_CHUNKS

    def expert_rows(first):
        x = _load_token_tiles(xs_ref, n, first).astype(BF16)
        a = _dot(x, wg_s[...])
        u = _dot(x, wu_s[...])
        hid = (a * jax.nn.sigmoid(a) * u).astype(BF16)
        return _dot(hid, wd_s[...])

    for ch in range(EXPERT_CHUNKS):
        first = ch * n
        active = jnp.logical_and(lo < first + n, hi > first)

        @pl.when(jnp.logical_and(active, lo <= first))
        def _whole_chunk():
            _store_token_tiles(ys_ref, expert_rows(first), first)

        @pl.when(jnp.logical_and(active, lo > first))
        def _chunk_tail():
            row = first + lax.broadcasted_iota(jnp.int32, (n, 1), 0)
            kept = _load_token_tiles(ys_ref, n, first)
            _store_token_tiles(ys_ref, jnp.where(row >= lo, expert_rows(first), kept), first)


def _moe_experts(step_tile, step_exp, step_lo, step_hi, step_next, xs, w_gate, w_up, w_down):
    n_rows = xs.shape[0]
    tm = TM_EXPERT
    _, d, f = w_gate.shape
    n_steps = step_tile.shape[0]
    kern = functools.partial(_expert_kernel, tm=tm)
    rows = lambda s, tl, ex, lo, hi, nx: (tl[s], 0)
    grid_spec = pltpu.PrefetchScalarGridSpec(
        num_scalar_prefetch=5,
        grid=(n_steps,),
        in_specs=[pl.BlockSpec((tm * TOKEN_ROWS, LANES), rows),
                  pl.BlockSpec(memory_space=pl.ANY), pl.BlockSpec(memory_space=pl.ANY),
                  pl.BlockSpec(memory_space=pl.ANY)],
        out_specs=pl.BlockSpec((tm * TOKEN_ROWS, LANES), rows),
        scratch_shapes=[pltpu.VMEM((d, f), F32), pltpu.VMEM((d, f), F32), pltpu.VMEM((f, d), F32),
                        pltpu.VMEM((d, f), BF16), pltpu.VMEM((d, f), BF16), pltpu.VMEM((f, d), BF16),
                        pltpu.SemaphoreType.DMA((3,))],
    )
    return pl.pallas_call(
        kern,
        grid_spec=grid_spec,
        out_shape=jax.ShapeDtypeStruct((n_rows, LANES), U32),
        compiler_params=pltpu.CompilerParams(dimension_semantics=("arbitrary",), vmem_limit_bytes=VMEM_LIMIT),
        name="moe_experts",
    )(step_tile, step_exp, step_lo, step_hi, step_next, xs, w_gate, w_up, w_down)


def _expert_steps(counts, n_rows, tm):
    n_e = counts.shape[0]
    n_tiles = n_rows // tm
    n_steps = n_tiles + n_e
    ends = jnp.cumsum(counts)
    starts = ends - counts
    first_tile = starts // tm
    last_tile = jnp.maximum(ends - 1, 0) // tm
    tiles_e = jnp.where(counts > 0, last_tile - first_tile + 1, 0)
    step_end = jnp.cumsum(tiles_e)
    step_start = step_end - tiles_e
    total = step_end[-1]
    s = jnp.arange(n_steps, dtype=jnp.int32)
    s_eff = jnp.minimum(s, total - 1)
    mine = ((step_start[None, :] <= s_eff[:, None]) & (s_eff[:, None] < step_end[None, :])).astype(jnp.int32)
    pick = lambda table: jnp.sum(mine * table[None, :], axis=1)
    exp = pick(jnp.arange(n_e, dtype=jnp.int32))
    tile = pick(first_tile) + (s_eff - pick(step_start))
    lo = jnp.maximum(pick(starts) - tile * tm, 0)
    hi = jnp.minimum(pick(ends) - tile * tm, tm)
    lo = jnp.where(s < total, lo, tm)
    hi = jnp.where(s < total, hi, 0)
    e_ids = jnp.arange(n_e, dtype=jnp.int32)
    later = (e_ids[None, :] > e_ids[:, None]) & (counts[None, :] > 0)
    next_e = jnp.min(jnp.where(later, e_ids[None, :], n_e), axis=1)
    nxt = pick(jnp.where(next_e < n_e, next_e, -1))
    return (tile.astype(jnp.int32), exp, lo.astype(jnp.int32), hi.astype(jnp.int32), nxt.astype(jnp.int32),
            starts.astype(jnp.int32))


def _combine_kernel(pos_ref, pos_next_ref, x2_ref, gates_ref, p_ref, wproj_ref, wgate_ref, gple_ref, gfin_ref,
                    ys_ref, out_ref, rows_s, sem, *, tm):
    i = pl.program_id(0)
    slot = i % 2

    def gather(p_ref, sl):
        def issue(r, carry):
            for k in range(2):
                _token_copy(ys_ref, p_ref[k * tm + r], rows_s.at[sl, k], r, sem.at[sl]).start(priority=k)
            return carry

        lax.fori_loop(0, tm, issue, 0, unroll=ISSUE_UNROLL)

    @pl.when(i == 0)
    def _first_tile():
        gather(pos_ref, 0)

    @pl.when(i + 1 < pl.num_programs(0))
    def _next_tile():
        gather(pos_next_ref, 1 - slot)

    def drain(r, carry):
        for k in range(2):
            _token_copy(ys_ref, 0, rows_s.at[slot, k], 0, sem.at[slot]).wait()
        return carry

    lax.fori_loop(0, tm, drain, 0, unroll=ISSUE_UNROLL)

    n = tm // COMBINE_CHUNKS
    for ch in range(COMBINE_CHUNKS):
        rows = slice(ch * n, (ch + 1) * n)
        gates = gates_ref[rows, :]
        y = (gates[:, 4:5] * _load_token_tiles(rows_s.at[slot, 0], n, ch * n)
             + gates[:, 5:6] * _load_token_tiles(rows_s.at[slot, 1], n, ch * n))
        x3 = x2_ref[rows, :] + y
        h3 = _rms(x3, gple_ref[...]).astype(BF16)
        gate = jax.nn.sigmoid(_dot(h3, wgate_ref[...]))
        pe = _dot(p_ref[rows, :].astype(BF16), wproj_ref[...])
        x4 = x3 + pe * gate
        out_ref[rows, :] = _rms(x4, gfin_ref[...])


def _combine_ple(pos, x2, meta, p2d, w_ple_proj, w_ple_gate, g_ple, g_final, ys):
    t, d = x2.shape
    d_ple = p2d.shape[1]
    tm = TM_COMBINE
    n_tiles = t // tm
    kern = functools.partial(_combine_kernel, tm=tm)
    tok = lambda w: pl.BlockSpec((tm, w), lambda i: (i, 0))
    full = lambda *shape: pl.BlockSpec(shape, lambda i: (0,) * len(shape))
    return pl.pallas_call(
        kern,
        grid=(t // tm,),
        in_specs=[pl.BlockSpec((2 * tm,), lambda i: (i,), memory_space=pltpu.SMEM),
                  pl.BlockSpec((2 * tm,), lambda i: (jnp.minimum(i + 1, n_tiles - 1),), memory_space=pltpu.SMEM),
                  tok(d), tok(LANES), tok(d_ple), full(d_ple, d), full(d, d), full(1, d), full(1, d),
                  pl.BlockSpec(memory_space=pl.ANY)],
        out_specs=tok(d),
        out_shape=jax.ShapeDtypeStruct((t, d), F32),
        scratch_shapes=[pltpu.VMEM((2, 2, tm * TOKEN_ROWS, LANES), U32), pltpu.SemaphoreType.DMA((2,))],
        compiler_params=pltpu.CompilerParams(dimension_semantics=("arbitrary",), vmem_limit_bytes=VMEM_LIMIT),
        name="combine_ple",
    )(pos, pos, x2, meta, p2d, w_ple_proj.astype(BF16), w_ple_gate.astype(BF16),
      g_ple.reshape(1, d).astype(F32), g_final.reshape(1, d).astype(F32), ys)


def _layer(x2d, p2d, seq, g_mix, w_in, w_pool, pool_scale, w_out, near_bias, g_ffn,
           w_r1, b_r1, w_r2, b_r2, w_gate, w_up, w_down, g_ple, w_ple_proj, w_ple_gate, g_final):
    t, d = x2d.shape
    batch = t // seq
    qkv, u = _in_proj(x2d, g_mix, w_in)
    a = _moba_attention(qkv.reshape(batch, seq, 3 * D_ATTN), near_bias).reshape(t, D_ATTN)

    assert N_GROUPS <= SUBLANES and E_PER_GROUP == SUBLANES
    pad_g = SUBLANES - N_GROUPS
    pad_e = LANES - SUBLANES - N_EXPERTS
    wr = jnp.concatenate([w_r1, jnp.zeros((d, pad_g), F32),
                          jnp.transpose(w_r2, (1, 0, 2)).reshape(d, N_EXPERTS),
                          jnp.zeros((d, pad_e), F32)], axis=1).astype(F32)
    br = jnp.concatenate([b_r1, jnp.zeros((pad_g,), F32), b_r2.reshape(N_EXPERTS), jnp.zeros((pad_e,), F32)])
    br = br.astype(F32).reshape(1, LANES)
    assert d == 2 * TOKEN_ROWS * LANES, "token tiles hold d/2 words in TOKEN_ROWS rows of 128"
    x2, h2t, meta, metat, cnt = _mix_route(x2d, a, u, w_pool, pool_scale, w_out, g_ffn, wr, br, seq)

    counts = cnt[:, 0].astype(jnp.int32)
    step_tile, step_exp, step_lo, step_hi, step_next, offs = _expert_steps(counts, 2 * t, TM_EXPERT)
    pos = _moe_positions(offs, metat)
    xs = _moe_scatter(_tile_positions(pos, TM_SCATTER), h2t)
    f = w_gate.shape[-1]
    ys = _moe_experts(step_tile, step_exp, step_lo, step_hi, step_next, xs,
                      w_gate.reshape(N_EXPERTS, d, f), w_up.reshape(N_EXPERTS, d, f),
                      w_down.reshape(N_EXPERTS, f, d))
    return _combine_ple(_tile_positions(pos, TM_COMBINE), x2, meta, p2d, w_ple_proj, w_ple_gate, g_ple, g_final, ys)


def kernel(x, p, g_mix, w_in, w_pool, pool_scale, w_out, rel_bias, g_ffn, w_r1, b_r1, w_r2, b_r2,
           w_gate, w_up, w_down, g_ple, w_ple_proj, w_ple_gate, g_final):
    batch, seq, d = x.shape
    depth = p.shape[0]
    assert depth == 1, "the final norm is fused into the last stage of a single layer"
    near_bias = _bias_tables(rel_bias, seq)
    out = _layer(x.reshape(batch * seq, d), p[0].reshape(batch * seq, -1), seq,
                 g_mix[0], w_in[0], w_pool[0], pool_scale[0], w_out[0], near_bias, g_ffn[0],
                 w_r1[0], b_r1[0], w_r2[0], b_r2[0], w_gate[0], w_up[0], w_down[0],
                 g_ple[0], w_ple_proj[0], w_ple_gate[0], g_final)
    return out.reshape(batch, seq, d)
```
